```python
import jax, jax.numpy as jnp
from jax import lax
import numpy as np

D_MODEL = 1024
BATCH = 4
SEQ = 8192
DEPTH = 1

CHUNK = 64
SSM_HEADS = 16
SSM_HEAD_DIM = 64
SSM_INNER = SSM_HEADS * SSM_HEAD_DIM
SSM_GROUPS = 2
SSM_STATE = 128
CONV_WIDTH = 4
CONV_DIM = SSM_INNER + 2 * SSM_GROUPS * SSM_STATE
SSM_CHUNK = CHUNK
ATTN_HEADS = 16
ATTN_HEAD_DIM = 64
ATTN_INNER = ATTN_HEADS * ATTN_HEAD_DIM
LEFT_CHUNKS = 8
BAND = (LEFT_CHUNKS + 1) * CHUNK
MAX_REL_DIST = 128
N_REL = 2 * MAX_REL_DIST + 1
D_FF = 2816
FFN_RES_SCALE = 0.5
N_BRANCHES = 2
RMS_EPS = 1e-6
IN_SPLIT_SIZES = (SSM_INNER, CONV_DIM, SSM_HEADS, ATTN_INNER, ATTN_INNER, ATTN_INNER, N_BRANCHES * D_MODEL)
IN_PROJ_DIM = SSM_INNER + CONV_DIM + SSM_HEADS + 3 * ATTN_INNER + N_BRANCHES * D_MODEL

kernel_name = "hybrid_ssd_chunkattn_macaron"


def rmsnorm(x, w):
    xf = x.astype(jnp.float32)
    y = xf * lax.rsqrt(jnp.mean(xf * xf, axis=-1, keepdims=True) + RMS_EPS)
    return (y * w.astype(jnp.float32)).astype(x.dtype)


def swiglu_ffn(h, w_gu, w_down):
    g, u = jnp.split(h @ w_gu, 2, axis=-1)
    return (jax.nn.silu(g) * u) @ w_down


def split_in_proj(p):
    offs = []
    acc = 0
    for sz in IN_SPLIT_SIZES[:-1]:
        acc += sz
        offs.append(acc)
    return jnp.split(p, offs, axis=-1)


def causal_depthwise_conv(x, w, b):
    y = lax.conv_general_dilated(
        x, w[:, None, :].astype(x.dtype), window_strides=(1,),
        padding=[(CONV_WIDTH - 1, 0)], dimension_numbers=('NWC', 'WIO', 'NWC'),
        feature_group_count=x.shape[-1])
    return y + b.astype(x.dtype)


def ssd_chunked(xh, dt, A, Bm, Cm):
    b, s = xh.shape[:2]
    nc = s // SSM_CHUNK
    r = SSM_HEADS // SSM_GROUPS
    X = (xh * dt[..., None]).reshape(b, nc, SSM_CHUNK, SSM_GROUPS, r, SSM_HEAD_DIM)
    a = (dt * A).reshape(b, nc, SSM_CHUNK, SSM_GROUPS, r)
    Bc = Bm.reshape(b, nc, SSM_CHUNK, SSM_GROUPS, SSM_STATE)
    Cc = Cm.reshape(b, nc, SSM_CHUNK, SSM_GROUPS, SSM_STATE)
    a_cum = jnp.cumsum(a, axis=2)
    seg = a_cum[:, :, :, None] - a_cum[:, :, None, :]
    causal = jnp.tril(jnp.ones((SSM_CHUNK, SSM_CHUNK), dtype=bool))[:, :, None, None]
    Ldec = jnp.exp(jnp.where(causal, seg, -jnp.inf))
    CB = jnp.einsum('bclgn,bcsgn->bclsg', Cc, Bc)
    y_diag = jnp.einsum('bclsg,bclsgr,bcsgrp->bclgrp', CB, Ldec, X)
    decay_to_end = jnp.exp(a_cum[:, :, -1:] - a_cum)
    states = jnp.einsum('bclgn,bclgr,bclgrp->bcgrpn', Bc, decay_to_end, X)
    chunk_decay = jnp.exp(a_cum[:, :, -1])

    def step(h, inp):
        st, dec = inp
        return h * dec[..., None, None] + st, h

    h0 = jnp.zeros((b, SSM_GROUPS, r, SSM_HEAD_DIM, SSM_STATE), dtype=X.dtype)
    _, prev = lax.scan(step, h0, (jnp.moveaxis(states, 1, 0), jnp.moveaxis(chunk_decay, 1, 0)))
    prev = jnp.moveaxis(prev, 0, 1)
    y_off = jnp.einsum('bclgn,bcgrpn,bclgr->bclgrp', Cc, prev, jnp.exp(a_cum))
    return (y_diag + y_off).reshape(b, s, SSM_HEADS, SSM_HEAD_DIM)


def ssd_branch(z, xBC, dt_raw, conv_w, conv_b, dt_bias, A_log, D_skip, norm_w):
    dtype = z.dtype
    b, s = z.shape[:2]
    xBC = jax.nn.silu(causal_depthwise_conv(xBC, conv_w, conv_b))
    xs, Bm, Cm = jnp.split(xBC, [SSM_INNER, SSM_INNER + SSM_GROUPS * SSM_STATE], axis=-1)
    xh = xs.reshape(b, s, SSM_HEADS, SSM_HEAD_DIM).astype(jnp.float32)
    Bm = Bm.reshape(b, s, SSM_GROUPS, SSM_STATE).astype(jnp.float32)
    Cm = Cm.reshape(b, s, SSM_GROUPS, SSM_STATE).astype(jnp.float32)
    dt = jax.nn.softplus(dt_raw.astype(jnp.float32) + dt_bias.astype(jnp.float32))
    A = -jnp.exp(A_log.astype(jnp.float32))
    y = ssd_chunked(xh, dt, A, Bm, Cm)
    y = y + D_skip.astype(jnp.float32)[:, None] * xh
    y = y.reshape(b, s, SSM_INNER) * jax.nn.silu(z.astype(jnp.float32))
    yg = y.reshape(b, s, SSM_GROUPS, SSM_INNER // SSM_GROUPS)
    yg = yg * lax.rsqrt(jnp.mean(yg * yg, axis=-1, keepdims=True) + RMS_EPS)
    y = yg.reshape(b, s, SSM_INNER) * norm_w.astype(jnp.float32)
    return y.astype(dtype)


def chunk_band_attention(q, k, v, rel_bias):
    dtype = q.dtype
    b, s = q.shape[:2]
    nc = s // CHUNK
    pad = LEFT_CHUNKS * CHUNK
    scale = ATTN_HEAD_DIM ** -0.5
    qh = q.reshape(b, s, ATTN_HEADS, ATTN_HEAD_DIM) * jnp.asarray(scale, dtype)
    kp = jnp.pad(k.reshape(b, s, ATTN_HEADS, ATTN_HEAD_DIM), ((0, 0), (pad, 0), (0, 0), (0, 0)))
    vp = jnp.pad(v.reshape(b, s, ATTN_HEADS, ATTN_HEAD_DIM), ((0, 0), (pad, 0), (0, 0), (0, 0)))
    qc = jnp.moveaxis(qh.reshape(b, nc, CHUNK, ATTN_HEADS, ATTN_HEAD_DIM), 1, 0)
    rel = (jnp.arange(CHUNK)[:, None] + pad) - jnp.arange(BAND)[None, :]
    idx = jnp.clip(rel, -MAX_REL_DIST, MAX_REL_DIST) + MAX_REL_DIST
    bias = rel_bias.astype(jnp.float32)[:, idx]
    band_pos = jnp.arange(BAND) - pad

    def one_chunk(args):
        c, q_c = args
        start = c * CHUNK
        k_b = lax.dynamic_slice_in_dim(kp, start, BAND, axis=1)
        v_b = lax.dynamic_slice_in_dim(vp, start, BAND, axis=1)
        sc = jnp.einsum('blhd,bjhd->bhlj', q_c, k_b).astype(jnp.float32) + bias[None]
        valid = (start + band_pos) >= 0
        sc = jnp.where(valid[None, None, None, :], sc, -jnp.inf)
        p = jax.nn.softmax(sc, axis=-1).astype(dtype)
        return jnp.einsum('bhlj,bjhd->blhd', p, v_b)

    out = lax.map(one_chunk, (jnp.arange(nc, dtype=jnp.int32), qc))
    return jnp.moveaxis(out, 0, 1).reshape(b, s, ATTN_INNER)


def setup_inputs(seed: int = 0) -> dict:
    key = jax.random.key(seed)
    ks = jax.random.split(key, 24)
    f32 = jnp.float32
    L = DEPTH

    def nrm(k, shape, fan_in):
        return jax.random.normal(k, shape, f32) * (fan_in ** -0.5)

    def gain(k, shape):
        return 1.0 + 0.02 * jax.random.normal(k, shape, f32)

    dt_init = jnp.exp(jax.random.uniform(ks[10], (L, SSM_HEADS), f32, np.log(1e-3), np.log(1e-1)))
    dt_bias = dt_init + jnp.log(-jnp.expm1(-dt_init))
    return {
        "x": jax.random.normal(ks[0], (BATCH, SEQ, D_MODEL), f32),
        "ffn1_norm_w": gain(ks[1], (L, D_MODEL)),
        "ffn1_w_gu": nrm(ks[2], (L, D_MODEL, 2 * D_FF), D_MODEL),
        "ffn1_w_down": nrm(ks[3], (L, D_FF, D_MODEL), D_FF),
        "mix_norm_w": gain(ks[4], (L, D_MODEL)),
        "w_in": nrm(ks[5], (L, D_MODEL, IN_PROJ_DIM), D_MODEL),
        "conv_w": nrm(ks[6], (L, CONV_WIDTH, CONV_DIM), CONV_WIDTH),
        "conv_b": 0.02 * jax.random.normal(ks[7], (L, CONV_DIM), f32),
        "dt_bias": dt_bias,
        "A_log": jnp.log(jax.random.uniform(ks[8], (L, SSM_HEADS), f32, 1.0, 16.0)),
        "D_skip": 1.0 + 0.1 * jax.random.normal(ks[9], (L, SSM_HEADS), f32),
        "ssm_norm_w": gain(ks[11], (L, SSM_INNER)),
        "rel_bias": 0.1 * jax.random.normal(ks[12], (L, ATTN_HEADS, N_REL), f32),
        "w_branch_ssm": nrm(ks[13], (L, SSM_INNER, D_MODEL), SSM_INNER),
        "w_branch_attn": nrm(ks[14], (L, ATTN_INNER, D_MODEL), ATTN_INNER),
        "w_out": nrm(ks[15], (L, D_MODEL, D_MODEL), D_MODEL),
        "ffn2_norm_w": gain(ks[16], (L, D_MODEL)),
        "ffn2_w_gu": nrm(ks[17], (L, D_MODEL, 2 * D_FF), D_MODEL),
        "ffn2_w_down": nrm(ks[18], (L, D_FF, D_MODEL), D_FF),
        "final_norm_w": gain(ks[19], (D_MODEL,)),
    }


def reference(x, ffn1_norm_w, ffn1_w_gu, ffn1_w_down, mix_norm_w, w_in, conv_w, conv_b,
              dt_bias, A_log, D_skip, ssm_norm_w, rel_bias, w_branch_ssm, w_branch_attn,
              w_out, ffn2_norm_w, ffn2_w_gu, ffn2_w_down, final_norm_w):
    for l in range(DEPTH):
        x = x + FFN_RES_SCALE * swiglu_ffn(rmsnorm(x, ffn1_norm_w[l]), ffn1_w_gu[l], ffn1_w_down[l])
        h = rmsnorm(x, mix_norm_w[l])
        z, xBC, dt_raw, q, k, v, g = split_in_proj(h @ w_in[l])
        y_ssm = ssd_branch(z, xBC, dt_raw, conv_w[l], conv_b[l], dt_bias[l], A_log[l],
                           D_skip[l], ssm_norm_w[l])
        y_attn = chunk_band_attention(q, k, v, rel_bias[l])
        g_ssm, g_attn = jnp.split(jax.nn.sigmoid(g), 2, axis=-1)
        merged = g_ssm * (y_ssm @ w_branch_ssm[l]) + g_attn * (y_attn @ w_branch_attn[l])
        x = x + merged @ w_out[l]
        x = x + FFN_RES_SCALE * swiglu_ffn(rmsnorm(x, ffn2_norm_w[l]), ffn2_w_gu[l], ffn2_w_down[l])
    return rmsnorm(x, final_norm_w)
```

```python
import functools

import jax
import jax.numpy as jnp
from jax import lax
from jax.experimental import pallas as pl
from jax.experimental.pallas import tpu as pltpu

F32 = jnp.float32
BF16 = jnp.bfloat16

CHUNK = 64
SSM_HEAD_DIM = 64
SSM_GROUPS = 2
SSM_STATE = 128
CONV_WIDTH = 4
ATTN_HEAD_DIM = 64
LEFT_CHUNKS = 8
MAX_REL_DIST = 128
FFN_RES_SCALE = 0.5
RMS_EPS = 1e-6

LANES = 128
SUBLANES = 8
MXU_DIM = 256
VMEM_BYTES = 64 * 1024 * 1024

ROW_TILE = 512
FFN_COL_TILE = 256
SSD_ROW_TILE = 512
ATTN_Q_CHUNKS = 4
ATTN_Q_TILE = ATTN_Q_CHUNKS * CHUNK
ATTN_K_BLOCKS = (LEFT_CHUNKS + ATTN_Q_CHUNKS) * CHUNK // ATTN_Q_TILE


def _vmem_limit(nbytes):
    return int(min(nbytes * 1.25 + (8 << 20), VMEM_BYTES - (6 << 20)))


def _const_spec(shape):
    zeros = (0,) * len(shape)
    return pl.BlockSpec(shape, lambda *_: zeros, pipeline_mode=pl.Buffered(1))


def _rmsnorm(x, w):
    ms = jnp.mean(x * x, axis=-1, keepdims=True)
    return x * lax.rsqrt(ms + RMS_EPS) * w


def _split3(x):
    hi = x.astype(BF16)
    r1 = x - hi.astype(F32)
    mid = r1.astype(BF16)
    lo = (r1 - mid.astype(F32)).astype(BF16)
    return hi, mid, lo


def _dot3(parts, w, *, parts_on_left):
    acc = None
    for p in parts:
        t = (jnp.dot(p, w, preferred_element_type=F32) if parts_on_left
             else jnp.dot(w, p, preferred_element_type=F32))
        acc = t if acc is None else acc + t
    return acc


def _ffn_kernel(x_ref, nw_ref, wg_ref, wu_ref, wd_ref, fnw_ref, o_ref, h_ref, acc_ref,
                *, n_col_tiles, final_norm):
    x = x_ref[...]
    h_ref[...] = _rmsnorm(x, nw_ref[...]).astype(BF16)
    acc_ref[...] = jnp.zeros_like(acc_ref)

    def body(j, carry):
        h = h_ref[...]
        g = jnp.dot(h, wg_ref[j], preferred_element_type=F32)
        u = jnp.dot(h, wu_ref[j], preferred_element_type=F32)
        a = (g * jax.nn.sigmoid(g) * u).astype(BF16)
        acc_ref[...] += jnp.dot(a, wd_ref[j], preferred_element_type=F32)
        return carry

    lax.fori_loop(0, n_col_tiles, body, 0)
    y = x + FFN_RES_SCALE * acc_ref[...]
    if final_norm:
        y = _rmsnorm(y, fnw_ref[...])
    o_ref[...] = y


def _ffn(x2d, norm_w, w_gu, w_down, final_norm_w, *, final_norm):
    t, d = x2d.shape
    d_ff = w_down.shape[0]
    tf = FFN_COL_TILE
    n_f = d_ff // tf
    assert n_f * tf == d_ff and t % ROW_TILE == 0
    tm = ROW_TILE
    wg = w_gu[:, :d_ff].reshape(d, n_f, tf).transpose(1, 0, 2).astype(BF16)
    wu = w_gu[:, d_ff:].reshape(d, n_f, tf).transpose(1, 0, 2).astype(BF16)
    wd = w_down.reshape(n_f, tf, d).astype(BF16)
    nbytes = 3 * d * d_ff * 2 + 4 * tm * d * 4 + tm * d * (2 + 4) + 3 * tm * tf * 4
    return pl.pallas_call(
        functools.partial(_ffn_kernel, n_col_tiles=n_f, final_norm=final_norm),
        grid=(t // tm,),
        in_specs=[
            pl.BlockSpec((tm, d), lambda i: (i, 0)),
            _const_spec((1, d)),
            _const_spec((n_f, d, tf)),
            _const_spec((n_f, d, tf)),
            _const_spec((n_f, tf, d)),
            _const_spec((1, d)),
        ],
        out_specs=pl.BlockSpec((tm, d), lambda i: (i, 0)),
        out_shape=jax.ShapeDtypeStruct((t, d), F32),
        scratch_shapes=[pltpu.VMEM((tm, d), BF16), pltpu.VMEM((tm, d), F32)],
        compiler_params=pltpu.CompilerParams(
            dimension_semantics=("arbitrary",), vmem_limit_bytes=_vmem_limit(nbytes)),
        name="ffn_final" if final_norm else "ffn",
    )(x2d, norm_w.reshape(1, d), wg, wu, wd, final_norm_w.reshape(1, d))


def _inproj_kernel(x_ref, nw_ref, wz_ref, wx_ref, wdt_ref, wq_ref, wk_ref, wv_ref, wg_ref,
                   z_ref, xbc_ref, dt_ref, q_ref, k_ref, v_ref, g_ref, h_ref, *, q_scale):
    h_ref[...] = _rmsnorm(x_ref[...], nw_ref[...]).astype(BF16)

    def project(w_ref, o_ref, scale=None):
        n = w_ref.shape[1]
        step = min(n, 2 * MXU_DIM)
        for c in range(0, n, step):
            r = jnp.dot(h_ref[...], w_ref[:, c:c + step], preferred_element_type=F32)
            if scale is not None:
                r = r * scale
            o_ref[:, c:c + step] = r.astype(o_ref.dtype)

    project(wz_ref, z_ref)
    project(wx_ref, xbc_ref)
    project(wdt_ref, dt_ref)
    project(wq_ref, q_ref, q_scale)
    project(wk_ref, k_ref)
    project(wv_ref, v_ref)
    project(wg_ref, g_ref)


def _in_proj(x2d, norm_w, w_in, *, ssm_inner, conv_dim, ssm_heads, attn_inner):
    t, d = x2d.shape
    tm = ROW_TILE
    sizes = (ssm_inner, conv_dim, ssm_heads, attn_inner, attn_inner, attn_inner)
    offs = [0]
    for s in sizes:
        offs.append(offs[-1] + s)
    offs.append(w_in.shape[1])
    ws = [w_in[:, offs[i]:offs[i + 1]].astype(BF16) for i in range(7)]
    widths = [w.shape[1] for w in ws]
    out_dtypes = [BF16, BF16, F32, BF16, BF16, BF16, BF16]
    nbytes = (sum(widths) * d * 2 + 2 * tm * d * 4 + tm * d * 2
              + 2 * sum(tm * w * jnp.dtype(dt).itemsize for w, dt in zip(widths, out_dtypes))
              + 2 * tm * 2 * MXU_DIM * 4)
    return pl.pallas_call(
        functools.partial(_inproj_kernel, q_scale=ATTN_HEAD_DIM ** -0.5),
        grid=(t // tm,),
        in_specs=[pl.BlockSpec((tm, d), lambda i: (i, 0)), _const_spec((1, d))]
                 + [_const_spec((d, w)) for w in widths],
        out_specs=[pl.BlockSpec((tm, w), lambda i: (i, 0)) for w in widths],
        out_shape=[jax.ShapeDtypeStruct((t, w), dt) for w, dt in zip(widths, out_dtypes)],
        scratch_shapes=[pltpu.VMEM((tm, d), BF16)],
        compiler_params=pltpu.CompilerParams(
            dimension_semantics=("arbitrary",), vmem_limit_bytes=_vmem_limit(nbytes)),
        name="in_proj",
    )(x2d, norm_w.reshape(1, d), *ws)


def _ssd_kernel(xbc_ref, z_ref, dtr_ref, convw_ref, convb_ref, dtb_ref, alog_ref, dskip_ref,
                normw_ref, expand_ref, o_ref,
                xpad_ref, xs_ref, bc_ref, dt_ref, state_ref,
                *, n_chunks, ssm_inner, heads):
    s_idx = pl.program_id(1)
    tm = xs_ref.shape[0]
    gn = SSM_GROUPS * SSM_STATE
    gw = ssm_inner // SSM_GROUPS
    halo = SUBLANES

    @pl.when(s_idx == 0)
    def _():
        state_ref[...] = jnp.zeros_like(state_ref)
        xpad_ref[0:halo, :] = jnp.zeros((halo, xpad_ref.shape[1]), F32)

    @pl.when(s_idx > 0)
    def _():
        xpad_ref[0:halo, :] = xpad_ref[tm:tm + halo, :]

    xpad_ref[halo:, :] = xbc_ref[...].astype(F32)

    conv = convb_ref[...]
    for kk in range(CONV_WIDTH):
        conv = conv + xpad_ref[pl.ds(halo - (CONV_WIDTH - 1) + kk, tm), :] * convw_ref[kk:kk + 1, :]
    conv = conv * jax.nn.sigmoid(conv)
    xs_ref[...] = conv[:, :ssm_inner]
    bc_ref[...] = conv[:, ssm_inner:]
    dt_ref[...] = jax.nn.softplus(dtr_ref[...] + dtb_ref[...])

    a_row = -jnp.exp(alog_ref[...])
    expand = expand_ref[...]
    row = lax.broadcasted_iota(jnp.int32, (CHUNK, CHUNK), 0)
    col = lax.broadcasted_iota(jnp.int32, (CHUNK, CHUNK), 1)
    tril = (col <= row).astype(BF16)
    prow = lax.broadcasted_iota(jnp.int32, (CHUNK, LANES), 0)
    plane = lax.broadcasted_iota(jnp.int32, (CHUNK, LANES), 1)
    causal_pair = (plane % SSM_HEAD_DIM) <= prow
    first_head = plane < SSM_HEAD_DIM
    drow = lax.broadcasted_iota(jnp.int32, (CHUNK, ssm_inner), 0)
    dlane = lax.broadcasted_iota(jnp.int32, (CHUNK, ssm_inner), 1)
    diag = (dlane % SSM_HEAD_DIM) == drow

    def chunk_body(c, carry):
        r0 = pl.multiple_of(c * CHUNK, CHUNK)
        xs = xs_ref[pl.ds(r0, CHUNK), :]
        bc = bc_ref[pl.ds(r0, CHUNK), :]
        dt = dt_ref[pl.ds(r0, CHUNK), :]
        dt_e = _dot3(_split3(dt), expand, parts_on_left=True)
        a_cum = _dot3(_split3(dt_e * a_row), tril, parts_on_left=False)
        a_last = a_cum[CHUNK - 1:CHUNK, :]
        xdt = xs * dt_e
        xdec = (xdt * jnp.exp(a_last - a_cum)).astype(BF16)
        e_cum = jnp.exp(a_cum)
        a_t = jnp.sum(jnp.where(diag, a_cum, 0.0), axis=0, keepdims=True)
        xdt_b = xdt.astype(BF16)
        ys = []
        for g in range(SSM_GROUPS):
            b_g = bc[:, g * SSM_STATE:(g + 1) * SSM_STATE]
            c_g = bc[:, gn + g * SSM_STATE:gn + (g + 1) * SSM_STATE].astype(BF16)
            cb = lax.dot_general(c_g, b_g.astype(BF16), (((1,), (1,)), ((), ())),
                                 preferred_element_type=F32)
            cb2 = jnp.concatenate([cb, cb], axis=1)
            g0 = g * gw
            prev = state_ref[:, g0:g0 + gw]
            y_off = jnp.dot(c_g, prev.astype(BF16), preferred_element_type=F32) * e_cum[:, g0:g0 + gw]
            new = jnp.dot(b_g.T.astype(BF16), xdec[:, g0:g0 + gw], preferred_element_type=F32)
            state_ref[:, g0:g0 + gw] = prev * jnp.exp(a_last[:, g0:g0 + gw]) + new
            for j in range(gw // LANES):
                c0 = g0 + j * LANES
                seg = a_cum[:, c0:c0 + LANES] - a_t[:, c0:c0 + LANES]
                ldec = jnp.exp(jnp.where(causal_pair, seg, -jnp.inf))
                m = (cb2 * ldec).astype(BF16)
                xp = xdt_b[:, c0:c0 + LANES]
                zero = jnp.zeros_like(xp)
                rhs = jnp.concatenate([jnp.where(first_head, xp, zero),
                                       jnp.where(first_head, zero, xp)], axis=0)
                ys.append(jnp.dot(m, rhs, preferred_element_type=F32)
                          + y_off[:, j * LANES:(j + 1) * LANES])
        y = jnp.concatenate(ys, axis=1) + dskip_ref[...] * xs
        zc = z_ref[pl.ds(r0, CHUNK), :].astype(F32)
        y = y * (zc * jax.nn.sigmoid(zc))
        outs = []
        for g in range(SSM_GROUPS):
            yg = y[:, g * gw:(g + 1) * gw]
            outs.append(yg * lax.rsqrt(jnp.mean(yg * yg, axis=-1, keepdims=True) + RMS_EPS))
        o_ref[pl.ds(r0, CHUNK), :] = (jnp.concatenate(outs, axis=1) * normw_ref[...]).astype(o_ref.dtype)
        return carry

    lax.fori_loop(0, n_chunks, chunk_body, 0)


def _ssd(xbc, z, dt_raw, conv_w, conv_b, dt_bias, a_log, d_skip, norm_w, *, batch, seq):
    t, conv_dim = xbc.shape
    inner = z.shape[1]
    heads = dt_raw.shape[1]
    tm = SSD_ROW_TILE
    assert seq % tm == 0 and inner == heads * SSM_HEAD_DIM
    n_s = seq // tm
    rep = lambda p: jnp.repeat(p.astype(F32), SSM_HEAD_DIM).reshape(1, inner)
    expand = jnp.repeat(jnp.eye(heads, dtype=BF16), SSM_HEAD_DIM, axis=1)
    row_map = lambda b, s: (b * n_s + s, 0)
    nbytes = (2 * tm * (conv_dim + 2 * inner) * 2 + 2 * tm * LANES * 4
              + (tm + SUBLANES) * conv_dim * 4 + tm * conv_dim * 4 + tm * LANES * 4
              + SSM_STATE * inner * 4 + 3 * tm * conv_dim * 4 + 24 * CHUNK * inner * 4)
    return pl.pallas_call(
        functools.partial(_ssd_kernel, n_chunks=tm // CHUNK, ssm_inner=inner, heads=heads),
        grid=(batch, n_s),
        in_specs=[
            pl.BlockSpec((tm, conv_dim), row_map),
            pl.BlockSpec((tm, inner), row_map),
            pl.BlockSpec((tm, heads), row_map),
            _const_spec((CONV_WIDTH, conv_dim)),
            _const_spec((1, conv_dim)),
            _const_spec((1, heads)),
            _const_spec((1, inner)),
            _const_spec((1, inner)),
            _const_spec((1, inner)),
            _const_spec((heads, inner)),
        ],
        out_specs=pl.BlockSpec((tm, inner), row_map),
        out_shape=jax.ShapeDtypeStruct((t, inner), BF16),
        scratch_shapes=[
            pltpu.VMEM((tm + SUBLANES, conv_dim), F32),
            pltpu.VMEM((tm, inner), F32),
            pltpu.VMEM((tm, conv_dim - inner), F32),
            pltpu.VMEM((tm, heads), F32),
            pltpu.VMEM((SSM_STATE, inner), F32),
        ],
        compiler_params=pltpu.CompilerParams(
            dimension_semantics=("arbitrary", "arbitrary"), vmem_limit_bytes=_vmem_limit(nbytes)),
        name="ssd",
    )(xbc, z, dt_raw, conv_w.astype(F32), conv_b.reshape(1, conv_dim).astype(F32),
      dt_bias.reshape(1, heads).astype(F32), rep(a_log), rep(d_skip),
      norm_w.reshape(1, inner).astype(F32), expand)


def _attn_kernel(q_ref, k0_ref, k1_ref, k2_ref, v0_ref, v1_ref, v2_ref, bias_ref, o_ref,
                 *, heads):
    i = pl.program_id(1)
    tq = q_ref.shape[0]
    nk = ATTN_K_BLOCKS * tq
    first_valid = jnp.maximum(ATTN_K_BLOCKS - 1 - i, 0) * tq
    key_pos = lax.broadcasted_iota(jnp.int32, (tq, nk), 1)
    valid = key_pos >= first_valid
    pair = LANES // ATTN_HEAD_DIM
    for hp in range(heads // pair):
        lo = hp * LANES
        qp = q_ref[:, lo:lo + LANES]
        kp = jnp.concatenate([k0_ref[:, lo:lo + LANES], k1_ref[:, lo:lo + LANES],
                              k2_ref[:, lo:lo + LANES]], axis=0)
        vp = jnp.concatenate([v0_ref[:, lo:lo + LANES], v1_ref[:, lo:lo + LANES],
                              v2_ref[:, lo:lo + LANES]], axis=0)
        outs = []
        for hh in range(pair):
            d0 = hh * ATTN_HEAD_DIM
            s = lax.dot_general(qp[:, d0:d0 + ATTN_HEAD_DIM], kp[:, d0:d0 + ATTN_HEAD_DIM],
                                (((1,), (1,)), ((), ())), preferred_element_type=F32)
            s = jnp.where(valid, s + bias_ref[hp * pair + hh], -jnp.inf)
            m = jnp.max(s, axis=-1, keepdims=True)
            p = jnp.exp(s - m)
            denom = jnp.sum(p, axis=-1, keepdims=True)
            o = jnp.dot(p.astype(BF16), vp[:, d0:d0 + ATTN_HEAD_DIM], preferred_element_type=F32)
            outs.append(o / denom)
        o_ref[:, lo:lo + LANES] = jnp.concatenate(outs, axis=1).astype(o_ref.dtype)


def _attn_bias_table(rel_bias):
    heads = rel_bias.shape[0]
    band = (LEFT_CHUNKS + 1) * CHUNK
    pad = LEFT_CHUNKS * CHUNK
    rel = (jnp.arange(CHUNK)[:, None] + pad) - jnp.arange(band)[None, :]
    idx = jnp.clip(rel, -MAX_REL_DIST, MAX_REL_DIST) + MAX_REL_DIST
    bias = rel_bias.astype(F32)[:, idx].reshape(heads, CHUNK, LEFT_CHUNKS + 1, CHUNK)
    n_kc = LEFT_CHUNKS + ATTN_Q_CHUNKS
    big = jnp.full((heads, ATTN_Q_CHUNKS, CHUNK, n_kc, CHUNK), -jnp.inf, F32)
    for qc in range(ATTN_Q_CHUNKS):
        big = big.at[:, qc, :, qc:qc + LEFT_CHUNKS + 1, :].set(bias)
    return big.reshape(heads, ATTN_Q_TILE, n_kc * CHUNK)


def _attention(q, k, v, rel_bias, *, batch, seq):
    t, inner = q.shape
    heads = rel_bias.shape[0]
    tq = ATTN_Q_TILE
    assert seq % tq == 0 and ATTN_K_BLOCKS == 3 and inner == heads * ATTN_HEAD_DIM
    n_q = seq // tq
    bias = _attn_bias_table(rel_bias)
    nk = ATTN_K_BLOCKS * tq

    def kmap(back):
        return lambda b, i: (b * n_q + jnp.maximum(i - back, 0), 0)

    blk = pl.BlockSpec((tq, inner), lambda b, i: (b * n_q + i, 0))
    kv_specs = [pl.BlockSpec((tq, inner), kmap(back)) for back in (2, 1, 0)]
    nbytes = 2 * 8 * tq * inner * 2 + heads * tq * nk * 4 + 6 * tq * nk * 4
    return pl.pallas_call(
        functools.partial(_attn_kernel, heads=heads),
        grid=(batch, n_q),
        in_specs=[blk] + kv_specs + kv_specs + [_const_spec((heads, tq, nk))],
        out_specs=blk,
        out_shape=jax.ShapeDtypeStruct((t, inner), BF16),
        compiler_params=pltpu.CompilerParams(
            dimension_semantics=("arbitrary", "arbitrary"), vmem_limit_bytes=_vmem_limit(nbytes)),
        name="band_attn",
    )(q, k, k, k, v, v, v, bias)


def _merge_kernel(x_ref, ys_ref, ya_ref, g_ref, ws_ref, wa_ref, wo_ref, o_ref):
    d = x_ref.shape[1]
    bs = jnp.dot(ys_ref[...], ws_ref[...], preferred_element_type=F32)
    ba = jnp.dot(ya_ref[...], wa_ref[...], preferred_element_type=F32)
    g = g_ref[...].astype(F32)
    merged = jax.nn.sigmoid(g[:, :d]) * bs + jax.nn.sigmoid(g[:, d:]) * ba
    o_ref[...] = x_ref[...] + jnp.dot(merged.astype(BF16), wo_ref[...], preferred_element_type=F32)


def _merge(x2d, y_ssm, y_attn, g, w_bs, w_ba, w_out):
    t, d = x2d.shape
    tm = ROW_TILE
    row = lambda w: pl.BlockSpec((tm, w), lambda i: (i, 0))
    nbytes = (w_bs.size + w_ba.size + w_out.size) * 2 + 2 * tm * (2 * d * 4 + 4 * d * 2) + 6 * tm * d * 4
    return pl.pallas_call(
        _merge_kernel,
        grid=(t // tm,),
        in_specs=[row(d), row(y_ssm.shape[1]), row(y_attn.shape[1]), row(2 * d),
                  _const_spec(w_bs.shape), _const_spec(w_ba.shape), _const_spec(w_out.shape)],
        out_specs=row(d),
        out_shape=jax.ShapeDtypeStruct((t, d), F32),
        compiler_params=pltpu.CompilerParams(
            dimension_semantics=("arbitrary",), vmem_limit_bytes=_vmem_limit(nbytes)),
        name="merge",
    )(x2d, y_ssm, y_attn, g, w_bs.astype(BF16), w_ba.astype(BF16), w_out.astype(BF16))


def kernel(x, ffn1_norm_w, ffn1_w_gu, ffn1_w_down, mix_norm_w, w_in, conv_w, conv_b, dt_bias, A_log,
           D_skip, ssm_norm_w, rel_bias, w_branch_ssm, w_branch_attn, w_out, ffn2_norm_w, ffn2_w_gu,
           ffn2_w_down, final_norm_w):
    batch, seq, d = x.shape
    depth = ffn1_w_gu.shape[0]
    ssm_heads = A_log.shape[1]
    ssm_inner = ssm_heads * SSM_HEAD_DIM
    conv_dim = conv_w.shape[2]
    attn_inner = rel_bias.shape[1] * ATTN_HEAD_DIM
    xf = x.reshape(batch * seq, d)
    for l in range(depth):
        last = l == depth - 1
        xf = _ffn(xf, ffn1_norm_w[l], ffn1_w_gu[l], ffn1_w_down[l], final_norm_w, final_norm=False)
        z, xbc, dt_raw, q, k, v, g = _in_proj(
            xf, mix_norm_w[l], w_in[l], ssm_inner=ssm_inner, conv_dim=conv_dim,
            ssm_heads=ssm_heads, attn_inner=attn_inner)
        y_ssm = _ssd(xbc, z, dt_raw, conv_w[l], conv_b[l], dt_bias[l], A_log[l], D_skip[l],
                     ssm_norm_w[l], batch=batch, seq=seq)
        y_attn = _attention(q, k, v, rel_bias[l], batch=batch, seq=seq)
        xf = _merge(xf, y_ssm, y_attn, g, w_branch_ssm[l], w_branch_attn[l], w_out[l])
        xf = _ffn(xf, ffn2_norm_w[l], ffn2_w_gu[l], ffn2_w_down[l], final_norm_w, final_norm=last)
    if depth == 0:
        raise ValueError("depth must be >= 1")
    return xf.reshape(batch, seq, d)
```

```python
import functools

import jax
import jax.numpy as jnp
from jax import lax
from jax.experimental import pallas as pl
from jax.experimental.pallas import tpu as pltpu

F32 = jnp.float32
BF16 = jnp.bfloat16

CHUNK = 64
SSM_HEAD_DIM = 64
SSM_GROUPS = 2
SSM_STATE = 128
CONV_WIDTH = 4
ATTN_HEAD_DIM = 64
LEFT_CHUNKS = 8
MAX_REL_DIST = 128
FFN_RES_SCALE = 0.5
RMS_EPS = 1e-6
LOG2_E = 1.4426950408889634

LANES = 128
SUBLANES = 8
MXU_DIM = 256
VMEM_BYTES = 64 * 1024 * 1024

ROW_TILE = 512
FFN_COL_TILE = 512
SSD_ROW_TILE = 512
ATTN_Q_CHUNKS = 4
ATTN_Q_TILE = ATTN_Q_CHUNKS * CHUNK
ATTN_SLOTS = 2
ATTN_K_BLOCKS = (LEFT_CHUNKS + ATTN_Q_CHUNKS) * CHUNK // ATTN_Q_TILE


def _vmem_limit(nbytes):
    return int(min(nbytes * 1.25 + (8 << 20), VMEM_BYTES - (6 << 20)))


def _const_spec(shape):
    zeros = (0,) * len(shape)
    return pl.BlockSpec(shape, lambda *_: zeros, pipeline_mode=pl.Buffered(1))


def _rmsnorm(x, w):
    ms = jnp.mean(x * x, axis=-1, keepdims=True)
    return x * lax.rsqrt(ms + RMS_EPS) * w


def _split3(x):
    hi = x.astype(BF16)
    r1 = x - hi.astype(F32)
    mid = r1.astype(BF16)
    lo = (r1 - mid.astype(F32)).astype(BF16)
    return hi, mid, lo


def _dot3(parts, w, *, parts_on_left):
    acc = None
    for p in parts:
        t = (jnp.dot(p, w, preferred_element_type=F32) if parts_on_left
             else jnp.dot(w, p, preferred_element_type=F32))
        acc = t if acc is None else acc + t
    return acc


def _ffn_kernel(x_ref, nw_ref, wgu_ref, wd_ref, fnw_ref, o_ref, h_ref, a_ref, *, d_ff, final_norm):
    h_ref[...] = _rmsnorm(x_ref[...], nw_ref[...]).astype(BF16)
    for c in range(0, d_ff, FFN_COL_TILE):
        w = min(FFN_COL_TILE, d_ff - c)
        g = jnp.dot(h_ref[...], wgu_ref[:, c:c + w], preferred_element_type=F32)
        u = jnp.dot(h_ref[...], wgu_ref[:, d_ff + c:d_ff + c + w], preferred_element_type=F32)
        a_ref[:, c:c + w] = (g * jax.nn.sigmoid(g) * u).astype(BF16)
    y = x_ref[...] + FFN_RES_SCALE * jnp.dot(a_ref[...], wd_ref[...], preferred_element_type=F32)
    if final_norm:
        y = _rmsnorm(y, fnw_ref[...])
    o_ref[...] = y


def _ffn(x2d, norm_w, w_gu, w_down, final_norm_w, *, final_norm):
    t, d = x2d.shape
    d_ff = w_down.shape[0]
    assert t % ROW_TILE == 0 and d_ff % LANES == 0
    tm = ROW_TILE
    nbytes = (3 * d * d_ff * 2 + 4 * tm * d * 4 + tm * d * 2 + tm * d_ff * 2
              + 2 * tm * FFN_COL_TILE * 4 + 2 * tm * d * 4)
    return pl.pallas_call(
        functools.partial(_ffn_kernel, d_ff=d_ff, final_norm=final_norm),
        grid=(t // tm,),
        in_specs=[
            pl.BlockSpec((tm, d), lambda i: (i, 0)),
            _const_spec((1, d)),
            _const_spec((d, 2 * d_ff)),
            _const_spec((d_ff, d)),
            _const_spec((1, d)),
        ],
        out_specs=pl.BlockSpec((tm, d), lambda i: (i, 0)),
        out_shape=jax.ShapeDtypeStruct((t, d), F32),
        scratch_shapes=[pltpu.VMEM((tm, d), BF16), pltpu.VMEM((tm, d_ff), BF16)],
        compiler_params=pltpu.CompilerParams(
            dimension_semantics=("arbitrary",), vmem_limit_bytes=_vmem_limit(nbytes)),
        name="ffn_final" if final_norm else "ffn",
    )(x2d, norm_w.reshape(1, d), w_gu.astype(BF16), w_down.astype(BF16), final_norm_w.reshape(1, d))


def _inproj_kernel(x_ref, nw_ref, wz_ref, wx_ref, wdt_ref, wq_ref, wk_ref, wv_ref, wg_ref,
                   z_ref, xbc_ref, dt_ref, q_ref, k_ref, v_ref, g_ref, h_ref, *, q_scale):
    h_ref[...] = _rmsnorm(x_ref[...], nw_ref[...]).astype(BF16)

    def project(w_ref, o_ref, scale=None):
        n = w_ref.shape[1]
        step = min(n, 2 * MXU_DIM)
        for c in range(0, n, step):
            r = jnp.dot(h_ref[...], w_ref[:, c:c + step], preferred_element_type=F32)
            if scale is not None:
                r = r * scale
            o_ref[:, c:c + step] = r.astype(o_ref.dtype)

    project(wz_ref, z_ref)
    project(wx_ref, xbc_ref)
    project(wdt_ref, dt_ref)
    project(wq_ref, q_ref, q_scale)
    project(wk_ref, k_ref)
    project(wv_ref, v_ref)
    project(wg_ref, g_ref)


def _in_proj(x2d, norm_w, w_in, *, ssm_inner, conv_dim, ssm_heads, attn_inner):
    t, d = x2d.shape
    tm = ROW_TILE
    sizes = (ssm_inner, conv_dim, ssm_heads, attn_inner, attn_inner, attn_inner)
    offs = [0]
    for s in sizes:
        offs.append(offs[-1] + s)
    offs.append(w_in.shape[1])
    ws = [w_in[:, offs[i]:offs[i + 1]].astype(BF16) for i in range(7)]
    widths = [w.shape[1] for w in ws]
    out_dtypes = [BF16, BF16, F32, BF16, BF16, BF16, BF16]
    nbytes = (sum(widths) * d * 2 + 2 * tm * d * 4 + tm * d * 2
              + 2 * sum(tm * w * jnp.dtype(dt).itemsize for w, dt in zip(widths, out_dtypes))
              + 2 * tm * 2 * MXU_DIM * 4)
    return pl.pallas_call(
        functools.partial(_inproj_kernel, q_scale=ATTN_HEAD_DIM ** -0.5 * LOG2_E),
        grid=(t // tm,),
        in_specs=[pl.BlockSpec((tm, d), lambda i: (i, 0)), _const_spec((1, d))]
                 + [_const_spec((d, w)) for w in widths],
        out_specs=[pl.BlockSpec((tm, w), lambda i: (i, 0)) for w in widths],
        out_shape=[jax.ShapeDtypeStruct((t, w), dt) for w, dt in zip(widths, out_dtypes)],
        scratch_shapes=[pltpu.VMEM((tm, d), BF16)],
        compiler_params=pltpu.CompilerParams(
            dimension_semantics=("arbitrary",), vmem_limit_bytes=_vmem_limit(nbytes)),
        name="in_proj",
    )(x2d, norm_w.reshape(1, d), *ws)


def _ssd_kernel(xbc_ref, z_ref, dtr_ref, convw_ref, convb_ref, dtb_ref, alog_ref, dskip_ref,
                normw_ref, expand_ref, o_ref,
                xpad_ref, xs_ref, bc_ref, dt_ref, state_ref,
                *, n_chunks, ssm_inner, heads):
    s_idx = pl.program_id(1)
    tm = xs_ref.shape[0]
    gn = SSM_GROUPS * SSM_STATE
    gw = ssm_inner // SSM_GROUPS
    halo = SUBLANES

    @pl.when(s_idx == 0)
    def _():
        state_ref[...] = jnp.zeros_like(state_ref)
        xpad_ref[0:halo, :] = jnp.zeros((halo, xpad_ref.shape[1]), F32)

    @pl.when(s_idx > 0)
    def _():
        xpad_ref[0:halo, :] = xpad_ref[tm:tm + halo, :]

    xpad_ref[halo:, :] = xbc_ref[...].astype(F32)

    conv = convb_ref[...]
    for kk in range(CONV_WIDTH):
        conv = conv + xpad_ref[pl.ds(halo - (CONV_WIDTH - 1) + kk, tm), :] * convw_ref[kk:kk + 1, :]
    conv = conv * jax.nn.sigmoid(conv)
    xs_ref[...] = conv[:, :ssm_inner]
    bc_ref[...] = conv[:, ssm_inner:]
    dt_ref[...] = jax.nn.softplus(dtr_ref[...] + dtb_ref[...])

    a_row = -jnp.exp(alog_ref[...])
    expand = expand_ref[...]
    row = lax.broadcasted_iota(jnp.int32, (CHUNK, CHUNK), 0)
    col = lax.broadcasted_iota(jnp.int32, (CHUNK, CHUNK), 1)
    tril = (col <= row).astype(BF16)
    prow = lax.broadcasted_iota(jnp.int32, (CHUNK, LANES), 0)
    plane = lax.broadcasted_iota(jnp.int32, (CHUNK, LANES), 1)
    causal_pair = (plane % SSM_HEAD_DIM) <= prow
    first_head = plane < SSM_HEAD_DIM
    drow = lax.broadcasted_iota(jnp.int32, (CHUNK, ssm_inner), 0)
    dlane = lax.broadcasted_iota(jnp.int32, (CHUNK, ssm_inner), 1)
    diag = (dlane % SSM_HEAD_DIM) == drow

    def chunk_body(c):
        r0 = c * CHUNK
        xs = xs_ref[pl.ds(r0, CHUNK), :]
        bc = bc_ref[pl.ds(r0, CHUNK), :]
        dt = dt_ref[pl.ds(r0, CHUNK), :]
        dt_e = _dot3(_split3(dt), expand, parts_on_left=True)
        a_cum = _dot3(_split3(dt_e * a_row), tril, parts_on_left=False)
        a_last = a_cum[CHUNK - 1:CHUNK, :]
        xdt = xs * dt_e
        xdec = (xdt * jnp.exp(a_last - a_cum)).astype(BF16)
        e_cum = jnp.exp(a_cum)
        a_t = jnp.sum(jnp.where(diag, a_cum, 0.0), axis=0, keepdims=True)
        xdt_b = xdt.astype(BF16)
        ys = []
        for g in range(SSM_GROUPS):
            b_g = bc[:, g * SSM_STATE:(g + 1) * SSM_STATE]
            c_g = bc[:, gn + g * SSM_STATE:gn + (g + 1) * SSM_STATE].astype(BF16)
            cb = lax.dot_general(c_g, b_g.astype(BF16), (((1,), (1,)), ((), ())),
                                 preferred_element_type=F32)
            cb2 = jnp.concatenate([cb, cb], axis=1)
            g0 = g * gw
            prev = state_ref[:, g0:g0 + gw]
            y_off = jnp.dot(c_g, prev.astype(BF16), preferred_element_type=F32) * e_cum[:, g0:g0 + gw]
            new = jnp.dot(b_g.T.astype(BF16), xdec[:, g0:g0 + gw], preferred_element_type=F32)
            state_ref[:, g0:g0 + gw] = prev * jnp.exp(a_last[:, g0:g0 + gw]) + new
            for j in range(gw // LANES):
                c0 = g0 + j * LANES
                seg = a_cum[:, c0:c0 + LANES] - a_t[:, c0:c0 + LANES]
                ldec = jnp.exp(jnp.where(causal_pair, seg, -jnp.inf))
                m = (cb2 * ldec).astype(BF16)
                xp = xdt_b[:, c0:c0 + LANES]
                zero = jnp.zeros_like(xp)
                rhs = jnp.concatenate([jnp.where(first_head, xp, zero),
                                       jnp.where(first_head, zero, xp)], axis=0)
                ys.append(jnp.dot(m, rhs, preferred_element_type=F32)
                          + y_off[:, j * LANES:(j + 1) * LANES])
        y = jnp.concatenate(ys, axis=1) + dskip_ref[...] * xs
        zc = z_ref[pl.ds(r0, CHUNK), :].astype(F32)
        y = y * (zc * jax.nn.sigmoid(zc))
        outs = []
        for g in range(SSM_GROUPS):
            yg = y[:, g * gw:(g + 1) * gw]
            outs.append(yg * lax.rsqrt(jnp.mean(yg * yg, axis=-1, keepdims=True) + RMS_EPS))
        o_ref[pl.ds(r0, CHUNK), :] = (jnp.concatenate(outs, axis=1) * normw_ref[...]).astype(o_ref.dtype)

    for c in range(n_chunks):
        chunk_body(c)


def _ssd(xbc, z, dt_raw, conv_w, conv_b, dt_bias, a_log, d_skip, norm_w, *, batch, seq):
    t, conv_dim = xbc.shape
    inner = z.shape[1]
    heads = dt_raw.shape[1]
    tm = SSD_ROW_TILE
    assert seq % tm == 0 and inner == heads * SSM_HEAD_DIM
    n_s = seq // tm
    rep = lambda p: jnp.repeat(p.astype(F32), SSM_HEAD_DIM).reshape(1, inner)
    expand = jnp.repeat(jnp.eye(heads, dtype=BF16), SSM_HEAD_DIM, axis=1)
    row_map = lambda b, s: (b * n_s + s, 0)
    nbytes = (2 * tm * (conv_dim + 2 * inner) * 2 + 2 * tm * LANES * 4
              + (tm + SUBLANES) * conv_dim * 4 + tm * conv_dim * 4 + tm * LANES * 4
              + SSM_STATE * inner * 4 + 3 * tm * conv_dim * 4 + 24 * CHUNK * inner * 4)
    return pl.pallas_call(
        functools.partial(_ssd_kernel, n_chunks=tm // CHUNK, ssm_inner=inner, heads=heads),
        grid=(batch, n_s),
        in_specs=[
            pl.BlockSpec((tm, conv_dim), row_map),
            pl.BlockSpec((tm, inner), row_map),
            pl.BlockSpec((tm, heads), row_map),
            _const_spec((CONV_WIDTH, conv_dim)),
            _const_spec((1, conv_dim)),
            _const_spec((1, heads)),
            _const_spec((1, inner)),
            _const_spec((1, inner)),
            _const_spec((1, inner)),
            _const_spec((heads, inner)),
        ],
        out_specs=pl.BlockSpec((tm, inner), row_map),
        out_shape=jax.ShapeDtypeStruct((t, inner), BF16),
        scratch_shapes=[
            pltpu.VMEM((tm + SUBLANES, conv_dim), F32),
            pltpu.VMEM((tm, inner), F32),
            pltpu.VMEM((tm, conv_dim - inner), F32),
            pltpu.VMEM((tm, heads), F32),
            pltpu.VMEM((SSM_STATE, inner), F32),
        ],
        compiler_params=pltpu.CompilerParams(
            dimension_semantics=("arbitrary", "arbitrary"), vmem_limit_bytes=_vmem_limit(nbytes)),
        name="ssd",
    )(xbc, z, dt_raw, conv_w.astype(F32), conv_b.reshape(1, conv_dim).astype(F32),
      dt_bias.reshape(1, heads).astype(F32), rep(a_log), rep(d_skip),
      norm_w.reshape(1, inner).astype(F32), expand)


def _attn_key_chunks(qv):
    q_per_group = LANES // CHUNK
    return range(qv * q_per_group, qv * q_per_group + LEFT_CHUNKS + q_per_group)


def _attn_kernel(q_ref, k0_ref, k1_ref, k2_ref, v0_ref, v1_ref, v2_ref, bias_ref, o_ref,
                 s_ref, p_ref, *, heads):
    tq = q_ref.shape[0]
    n_slots = s_ref.shape[0]
    n_kc = s_ref.shape[1] // CHUNK
    n_qv = tq // LANES
    pair = LANES // ATTN_HEAD_DIM
    lane_head = lax.broadcasted_iota(jnp.int32, (tq, LANES), 1) // ATTN_HEAD_DIM
    for slot in range(n_slots):
        for qv in range(n_qv):
            for kc in range(n_kc):
                if kc not in _attn_key_chunks(qv):
                    p_ref[slot, kc * CHUNK:(kc + 1) * CHUNK, qv * LANES:(qv + 1) * LANES] = jnp.zeros(
                        (CHUNK, LANES), BF16)

    def scores(h):
        lo = (h // pair) * LANES
        qp = q_ref[:, lo:lo + LANES]
        kp = jnp.concatenate([k0_ref[:, lo:lo + LANES], k1_ref[:, lo:lo + LANES],
                              k2_ref[:, lo:lo + LANES]], axis=0)
        qm = jnp.where(lane_head == h % pair, qp, jnp.zeros_like(qp))
        s_ref[h % n_slots] = lax.dot_general(kp, qm, (((1,), (1,)), ((), ())),
                                             preferred_element_type=F32)

    def softmax(h):
        slot = h % n_slots
        inv = []
        for qv in range(n_qv):
            cols = slice(qv * LANES, (qv + 1) * LANES)
            mx = None
            for kc in _attn_key_chunks(qv):
                rows = slice(kc * CHUNK, (kc + 1) * CHUNK)
                t = s_ref[slot, rows, cols] + bias_ref[0, h, rows, cols]
                mx = t if mx is None else jnp.maximum(mx, t)
            m = jnp.max(mx, axis=0, keepdims=True)
            acc = None
            for kc in _attn_key_chunks(qv):
                rows = slice(kc * CHUNK, (kc + 1) * CHUNK)
                p = jnp.exp2(s_ref[slot, rows, cols] + bias_ref[0, h, rows, cols] - m)
                acc = p if acc is None else acc + p
                p_ref[slot, rows, cols] = p.astype(BF16)
            inv.append(1.0 / jnp.sum(acc, axis=0, keepdims=True))
        return jnp.concatenate(inv, axis=1)

    def weighted_values(h, inv):
        lo = (h // pair) * LANES
        d0 = (h % pair) * ATTN_HEAD_DIM
        vpt = jnp.concatenate([v0_ref[:, lo:lo + LANES], v1_ref[:, lo:lo + LANES],
                               v2_ref[:, lo:lo + LANES]], axis=0).T
        o_t = jnp.dot(vpt[d0:d0 + ATTN_HEAD_DIM, :], p_ref[h % n_slots],
                      preferred_element_type=F32)
        return o_t * inv

    outs = []

    def finish(h, inv):
        outs.append(weighted_values(h, inv))
        if len(outs) == pair:
            lo = (h // pair) * LANES
            o_ref[:, lo:lo + LANES] = jnp.concatenate(outs, axis=0).T.astype(o_ref.dtype)
            outs.clear()

    scores(0)
    inv = None
    for h in range(heads):
        if h + 1 < heads:
            scores(h + 1)
        if h > 0:
            finish(h - 1, inv)
        inv = softmax(h)
    finish(heads - 1, inv)


def _attn_bias_table(rel_bias):
    heads = rel_bias.shape[0]
    band = (LEFT_CHUNKS + 1) * CHUNK
    pad = LEFT_CHUNKS * CHUNK
    m = jnp.arange(band + CHUNK - 1)
    rev = rel_bias.astype(F32)[:, jnp.clip(pad + CHUNK - 1 - m, -MAX_REL_DIST, MAX_REL_DIST) + MAX_REL_DIST]
    bias = jnp.stack([rev[:, CHUNK - 1 - l:CHUNK - 1 - l + band] for l in range(CHUNK)], axis=1)
    bias = bias.reshape(heads, CHUNK, LEFT_CHUNKS + 1, CHUNK)
    n_kc = LEFT_CHUNKS + ATTN_Q_CHUNKS
    big = jnp.full((heads, ATTN_Q_CHUNKS, CHUNK, n_kc, CHUNK), -jnp.inf, F32)
    for qc in range(ATTN_Q_CHUNKS):
        big = big.at[:, qc, :, qc:qc + LEFT_CHUNKS + 1, :].set(bias)
    big = big.reshape(heads, ATTN_Q_TILE, n_kc * CHUNK).transpose(0, 2, 1) * LOG2_E
    key = jnp.arange(n_kc * CHUNK)[None, :, None]
    variants = [jnp.where(key >= (ATTN_K_BLOCKS - 1 - v) * ATTN_Q_TILE, big, -jnp.inf)
                for v in range(ATTN_K_BLOCKS)]
    return jnp.stack(variants, axis=0)


def _attention(q, k, v, rel_bias, *, batch, seq):
    t, inner = q.shape
    heads = rel_bias.shape[0]
    tq = ATTN_Q_TILE
    assert seq % tq == 0 and ATTN_K_BLOCKS == 3 and inner == heads * ATTN_HEAD_DIM
    n_q = seq // tq
    bias = _attn_bias_table(rel_bias)
    nk = ATTN_K_BLOCKS * tq

    def kmap(back):
        return lambda b, i: (b * n_q + jnp.maximum(i - back, 0), 0)

    blk = pl.BlockSpec((tq, inner), lambda b, i: (b * n_q + i, 0))
    kv_specs = [pl.BlockSpec((tq, inner), kmap(back)) for back in (2, 1, 0)]
    bias_spec = pl.BlockSpec((1, heads, nk, tq),
                             lambda b, i: (jnp.minimum(i, ATTN_K_BLOCKS - 1), 0, 0, 0))
    nbytes = 2 * 8 * tq * inner * 2 + 2 * heads * tq * nk * 4 + tq * nk * (4 + 2) + 4 * tq * nk * 4
    return pl.pallas_call(
        functools.partial(_attn_kernel, heads=heads),
        grid=(batch, n_q),
        in_specs=[blk] + kv_specs + kv_specs + [bias_spec],
        out_specs=blk,
        out_shape=jax.ShapeDtypeStruct((t, inner), BF16),
        scratch_shapes=[pltpu.VMEM((ATTN_SLOTS, nk, tq), F32), pltpu.VMEM((ATTN_SLOTS, nk, tq), BF16)],
        compiler_params=pltpu.CompilerParams(
            dimension_semantics=("arbitrary", "arbitrary"), vmem_limit_bytes=_vmem_limit(nbytes)),
        name="band_attn",
    )(q, k, k, k, v, v, v, bias)


def _merge_kernel(x_ref, ys_ref, ya_ref, g_ref, ws_ref, wa_ref, wo_ref, o_ref):
    d = x_ref.shape[1]
    bs = jnp.dot(ys_ref[...], ws_ref[...], preferred_element_type=F32)
    ba = jnp.dot(ya_ref[...], wa_ref[...], preferred_element_type=F32)
    g = g_ref[...].astype(F32)
    merged = jax.nn.sigmoid(g[:, :d]) * bs + jax.nn.sigmoid(g[:, d:]) * ba
    o_ref[...] = x_ref[...] + jnp.dot(merged.astype(BF16), wo_ref[...], preferred_element_type=F32)


def _merge(x2d, y_ssm, y_attn, g, w_bs, w_ba, w_out):
    t, d = x2d.shape
    tm = ROW_TILE
    row = lambda w: pl.BlockSpec((tm, w), lambda i: (i, 0))
    nbytes = (w_bs.size + w_ba.size + w_out.size) * 2 + 2 * tm * (2 * d * 4 + 4 * d * 2) + 6 * tm * d * 4
    return pl.pallas_call(
        _merge_kernel,
        grid=(t // tm,),
        in_specs=[row(d), row(y_ssm.shape[1]), row(y_attn.shape[1]), row(2 * d),
                  _const_spec(w_bs.shape), _const_spec(w_ba.shape), _const_spec(w_out.shape)],
        out_specs=row(d),
        out_shape=jax.ShapeDtypeStruct((t, d), F32),
        compiler_params=pltpu.CompilerParams(
            dimension_semantics=("arbitrary",), vmem_limit_bytes=_vmem_limit(nbytes)),
        name="merge",
    )(x2d, y_ssm, y_attn, g, w_bs.astype(BF16), w_ba.astype(BF16), w_out.astype(BF16))


def kernel(x, ffn1_norm_w, ffn1_w_gu, ffn1_w_down, mix_norm_w, w_in, conv_w, conv_b, dt_bias, A_log,
           D_skip, ssm_norm_w, rel_bias, w_branch_ssm, w_branch_attn, w_out, ffn2_norm_w, ffn2_w_gu,
           ffn2_w_down, final_norm_w):
    batch, seq, d = x.shape
    depth = ffn1_w_gu.shape[0]
    ssm_heads = A_log.shape[1]
    ssm_inner = ssm_heads * SSM_HEAD_DIM
    conv_dim = conv_w.shape[2]
    attn_inner = rel_bias.shape[1] * ATTN_HEAD_DIM
    xf = x.reshape(batch * seq, d)
    for l in range(depth):
        last = l == depth - 1
        xf = _ffn(xf, ffn1_norm_w[l], ffn1_w_gu[l], ffn1_w_down[l], final_norm_w, final_norm=False)
        z, xbc, dt_raw, q, k, v, g = _in_proj(
            xf, mix_norm_w[l], w_in[l], ssm_inner=ssm_inner, conv_dim=conv_dim,
            ssm_heads=ssm_heads, attn_inner=attn_inner)
        y_ssm = _ssd(xbc, z, dt_raw, conv_w[l], conv_b[l], dt_bias[l], A_log[l], D_skip[l],
                     ssm_norm_w[l], batch=batch, seq=seq)
        y_attn = _attention(q, k, v, rel_bias[l], batch=batch, seq=seq)
        xf = _merge(xf, y_ssm, y_attn, g, w_branch_ssm[l], w_branch_attn[l], w_out[l])
        xf = _ffn(xf, ffn2_norm_w[l], ffn2_w_gu[l], ffn2_w_down[l], final_norm_w, final_norm=last)
    if depth == 0:
        raise ValueError("depth must be >= 1")
    return xf.reshape(batch, seq, d)
```

```python
import functools

import jax
import jax.numpy as jnp
from jax import lax
from jax.experimental import pallas as pl
from jax.experimental.pallas import tpu as pltpu

F32 = jnp.float32
BF16 = jnp.bfloat16

CHUNK = 64
SSM_HEAD_DIM = 64
SSM_GROUPS = 2
SSM_STATE = 128
CONV_WIDTH = 4
ATTN_HEAD_DIM = 64
LEFT_CHUNKS = 8
MAX_REL_DIST = 128
FFN_RES_SCALE = 0.5
RMS_EPS = 1e-6
LOG2_E = 1.4426950408889634

LANES = 128
SUBLANES = 8
MXU_DIM = 256
VMEM_BYTES = 64 * 1024 * 1024

ROW_TILE = 512
FFN_COL_TILE = 512
SSD_ROW_TILE = 512
ATTN_Q_CHUNKS = 4
ATTN_Q_TILE = ATTN_Q_CHUNKS * CHUNK
ATTN_SLOTS = 2
ATTN_K_BLOCKS = (LEFT_CHUNKS + ATTN_Q_CHUNKS) * CHUNK // ATTN_Q_TILE


def _vmem_limit(nbytes):
    return int(min(nbytes * 1.25 + (8 << 20), VMEM_BYTES - (6 << 20)))


def _const_spec(shape):
    zeros = (0,) * len(shape)
    return pl.BlockSpec(shape, lambda *_: zeros, pipeline_mode=pl.Buffered(1))


def _rmsnorm(x, w):
    ms = jnp.mean(x * x, axis=-1, keepdims=True)
    return x * lax.rsqrt(ms + RMS_EPS) * w


def _split3(x):
    hi = x.astype(BF16)
    r1 = x - hi.astype(F32)
    mid = r1.astype(BF16)
    lo = (r1 - mid.astype(F32)).astype(BF16)
    return hi, mid, lo


def _dot3(parts, w, *, parts_on_left):
    acc = None
    for p in parts:
        t = (jnp.dot(p, w, preferred_element_type=F32) if parts_on_left
             else jnp.dot(w, p, preferred_element_type=F32))
        acc = t if acc is None else acc + t
    return acc


def _ffn_kernel(x_ref, nw_ref, wgu_ref, wd_ref, fnw_ref, o_ref, h_ref, a_ref, *, d_ff, final_norm):
    h_ref[...] = _rmsnorm(x_ref[...], nw_ref[...]).astype(BF16)
    for c in range(0, d_ff, FFN_COL_TILE):
        w = min(FFN_COL_TILE, d_ff - c)
        g = jnp.dot(h_ref[...], wgu_ref[:, c:c + w], preferred_element_type=F32)
        u = jnp.dot(h_ref[...], wgu_ref[:, d_ff + c:d_ff + c + w], preferred_element_type=F32)
        a_ref[:, c:c + w] = (g * jax.nn.sigmoid(g) * u).astype(BF16)
    y = x_ref[...] + FFN_RES_SCALE * jnp.dot(a_ref[...], wd_ref[...], preferred_element_type=F32)
    if final_norm:
        y = _rmsnorm(y, fnw_ref[...])
    o_ref[...] = y


def _ffn(x2d, norm_w, w_gu, w_down, final_norm_w, *, final_norm):
    t, d = x2d.shape
    d_ff = w_down.shape[0]
    assert t % ROW_TILE == 0 and d_ff % LANES == 0
    tm = ROW_TILE
    nbytes = (3 * d * d_ff * 2 + 4 * tm * d * 4 + tm * d * 2 + tm * d_ff * 2
              + 2 * tm * FFN_COL_TILE * 4 + 2 * tm * d * 4)
    return pl.pallas_call(
        functools.partial(_ffn_kernel, d_ff=d_ff, final_norm=final_norm),
        grid=(t // tm,),
        in_specs=[
            pl.BlockSpec((tm, d), lambda i: (i, 0)),
            _const_spec((1, d)),
            _const_spec((d, 2 * d_ff)),
            _const_spec((d_ff, d)),
            _const_spec((1, d)),
        ],
        out_specs=pl.BlockSpec((tm, d), lambda i: (i, 0)),
        out_shape=jax.ShapeDtypeStruct((t, d), F32),
        scratch_shapes=[pltpu.VMEM((tm, d), BF16), pltpu.VMEM((tm, d_ff), BF16)],
        compiler_params=pltpu.CompilerParams(
            dimension_semantics=("arbitrary",), vmem_limit_bytes=_vmem_limit(nbytes)),
        name="ffn_final" if final_norm else "ffn",
    )(x2d, norm_w.reshape(1, d), w_gu.astype(BF16), w_down.astype(BF16), final_norm_w.reshape(1, d))


def _inproj_kernel(x_ref, nw_ref, wz_ref, wx_ref, wdt_ref, wq_ref, wk_ref, wv_ref, wg_ref,
                   z_ref, xbc_ref, dt_ref, q_ref, k_ref, v_ref, g_ref, h_ref, *, q_scale):
    h_ref[...] = _rmsnorm(x_ref[...], nw_ref[...]).astype(BF16)

    def project(w_ref, o_ref, scale=None):
        n = w_ref.shape[1]
        step = min(n, 2 * MXU_DIM)
        for c in range(0, n, step):
            r = jnp.dot(h_ref[...], w_ref[:, c:c + step], preferred_element_type=F32)
            if scale is not None:
                r = r * scale
            o_ref[:, c:c + step] = r.astype(o_ref.dtype)

    project(wz_ref, z_ref)
    project(wx_ref, xbc_ref)
    project(wdt_ref, dt_ref)
    project(wq_ref, q_ref, q_scale)
    project(wk_ref, k_ref)
    project(wv_ref, v_ref)
    project(wg_ref, g_ref)


def _in_proj(x2d, norm_w, w_in, *, ssm_inner, conv_dim, ssm_heads, attn_inner):
    t, d = x2d.shape
    tm = ROW_TILE
    sizes = (ssm_inner, conv_dim, ssm_heads, attn_inner, attn_inner, attn_inner)
    offs = [0]
    for s in sizes:
        offs.append(offs[-1] + s)
    offs.append(w_in.shape[1])
    ws = [w_in[:, offs[i]:offs[i + 1]].astype(BF16) for i in range(7)]
    widths = [w.shape[1] for w in ws]
    out_dtypes = [BF16, BF16, F32, BF16, BF16, BF16, BF16]
    nbytes = (sum(widths) * d * 2 + 2 * tm * d * 4 + tm * d * 2
              + 2 * sum(tm * w * jnp.dtype(dt).itemsize for w, dt in zip(widths, out_dtypes))
              + 2 * tm * 2 * MXU_DIM * 4)
    return pl.pallas_call(
        functools.partial(_inproj_kernel, q_scale=ATTN_HEAD_DIM ** -0.5 * LOG2_E),
        grid=(t // tm,),
        in_specs=[pl.BlockSpec((tm, d), lambda i: (i, 0)), _const_spec((1, d))]
                 + [_const_spec((d, w)) for w in widths],
        out_specs=[pl.BlockSpec((tm, w), lambda i: (i, 0)) for w in widths],
        out_shape=[jax.ShapeDtypeStruct((t, w), dt) for w, dt in zip(widths, out_dtypes)],
        scratch_shapes=[pltpu.VMEM((tm, d), BF16)],
        compiler_params=pltpu.CompilerParams(
            dimension_semantics=("arbitrary",), vmem_limit_bytes=_vmem_limit(nbytes)),
        name="in_proj",
    )(x2d, norm_w.reshape(1, d), *ws)


def _ssd_kernel(xbc_ref, z_ref, dtr_ref, convw_ref, convb_ref, dtb_ref, alog_ref, dskip_ref,
                normw_ref, expand_ref, o_ref,
                carry_ref, xs_ref, bc_ref, dt_ref, state_ref,
                *, n_chunks, ssm_inner, heads):
    s_idx = pl.program_id(1)
    tm = xs_ref.shape[0]
    gn = SSM_GROUPS * SSM_STATE
    gw = ssm_inner // SSM_GROUPS

    @pl.when(s_idx == 0)
    def _():
        state_ref[...] = jnp.zeros_like(state_ref)
        carry_ref[...] = jnp.zeros_like(carry_ref)

    x = xbc_ref[...].astype(F32)
    first_row = lax.broadcasted_iota(jnp.int32, (SUBLANES, x.shape[1]), 0) == 0

    def shift_rows(a, tap):
        rolled = pltpu.roll(a, 1, 0)
        head = jnp.where(first_row, carry_ref[tap:tap + 1, :], rolled[:SUBLANES])
        carry_ref[tap:tap + 1, :] = a[tm - 1:tm, :]
        return jnp.concatenate([head, rolled[SUBLANES:]], axis=0)

    part = x * convw_ref[0:1, :]
    for kk in range(1, CONV_WIDTH):
        part = x * convw_ref[kk:kk + 1, :] + shift_rows(part, kk - 1)
    conv = part + convb_ref[...]
    conv = conv * jax.nn.sigmoid(conv)
    xs_ref[...] = conv[:, :ssm_inner]
    bc_ref[...] = conv[:, ssm_inner:]
    dt_ref[...] = jax.nn.softplus(dtr_ref[...] + dtb_ref[...])

    a_head = -jnp.exp(alog_ref[...])
    expand = expand_ref[...]
    row = lax.broadcasted_iota(jnp.int32, (CHUNK, CHUNK), 0)
    col = lax.broadcasted_iota(jnp.int32, (CHUNK, CHUNK), 1)
    tril = (col <= row).astype(BF16)
    prow = lax.broadcasted_iota(jnp.int32, (CHUNK, LANES), 0)
    plane = lax.broadcasted_iota(jnp.int32, (CHUNK, LANES), 1)
    causal_pair = (plane % SSM_HEAD_DIM) <= prow
    first_head = plane < SSM_HEAD_DIM
    drow = lax.broadcasted_iota(jnp.int32, (CHUNK, ssm_inner), 0)
    dlane = lax.broadcasted_iota(jnp.int32, (CHUNK, ssm_inner), 1)
    diag = (dlane % SSM_HEAD_DIM) == drow

    def chunk_body(c):
        r0 = c * CHUNK
        xs = xs_ref[pl.ds(r0, CHUNK), :]
        bc = bc_ref[pl.ds(r0, CHUNK), :]
        dt = dt_ref[pl.ds(r0, CHUNK), :]
        dt_e = _dot3(_split3(dt), expand, parts_on_left=True)
        a_cum_h = _dot3(_split3(dt * a_head), tril, parts_on_left=False)
        a_cum = _dot3(_split3(a_cum_h), expand, parts_on_left=True)
        a_last = a_cum[CHUNK - 1:CHUNK, :]
        xdt = xs * dt_e
        xdec = (xdt * jnp.exp(a_last - a_cum)).astype(BF16)
        e_cum = jnp.exp(a_cum)
        a_t = jnp.sum(jnp.where(diag, a_cum, 0.0), axis=0, keepdims=True)
        xdt_b = xdt.astype(BF16)
        ys = []
        for g in range(SSM_GROUPS):
            b_g = bc[:, g * SSM_STATE:(g + 1) * SSM_STATE]
            c_g = bc[:, gn + g * SSM_STATE:gn + (g + 1) * SSM_STATE].astype(BF16)
            cb = lax.dot_general(c_g, b_g.astype(BF16), (((1,), (1,)), ((), ())),
                                 preferred_element_type=F32)
            cb2 = jnp.concatenate([cb, cb], axis=1)
            g0 = g * gw
            prev = state_ref[:, g0:g0 + gw]
            y_off = jnp.dot(c_g, prev.astype(BF16), preferred_element_type=F32) * e_cum[:, g0:g0 + gw]
            new = jnp.dot(b_g.T.astype(BF16), xdec[:, g0:g0 + gw], preferred_element_type=F32)
            state_ref[:, g0:g0 + gw] = prev * jnp.exp(a_last[:, g0:g0 + gw]) + new
            for j in range(gw // LANES):
                c0 = g0 + j * LANES
                seg = a_cum[:, c0:c0 + LANES] - a_t[:, c0:c0 + LANES]
                ldec = jnp.exp(jnp.where(causal_pair, seg, -jnp.inf))
                m = (cb2 * ldec).astype(BF16)
                xp = xdt_b[:, c0:c0 + LANES]
                zero = jnp.zeros_like(xp)
                rhs = jnp.concatenate([jnp.where(first_head, xp, zero),
                                       jnp.where(first_head, zero, xp)], axis=0)
                ys.append(jnp.dot(m, rhs, preferred_element_type=F32)
                          + y_off[:, j * LANES:(j + 1) * LANES])
        y = jnp.concatenate(ys, axis=1) + dskip_ref[...] * xs
        zc = z_ref[pl.ds(r0, CHUNK), :].astype(F32)
        y = y * (zc * jax.nn.sigmoid(zc))
        outs = []
        for g in range(SSM_GROUPS):
            yg = y[:, g * gw:(g + 1) * gw]
            outs.append(yg * lax.rsqrt(jnp.mean(yg * yg, axis=-1, keepdims=True) + RMS_EPS))
        o_ref[pl.ds(r0, CHUNK), :] = (jnp.concatenate(outs, axis=1) * normw_ref[...]).astype(o_ref.dtype)

    for c in range(n_chunks):
        chunk_body(c)


def _ssd(xbc, z, dt_raw, conv_w, conv_b, dt_bias, a_log, d_skip, norm_w, *, batch, seq):
    t, conv_dim = xbc.shape
    inner = z.shape[1]
    heads = dt_raw.shape[1]
    tm = SSD_ROW_TILE
    assert seq % tm == 0 and inner == heads * SSM_HEAD_DIM
    n_s = seq // tm
    rep = lambda p: jnp.repeat(p.astype(F32), SSM_HEAD_DIM).reshape(1, inner)
    expand = jnp.repeat(jnp.eye(heads, dtype=BF16), SSM_HEAD_DIM, axis=1)
    row_map = lambda b, s: (b * n_s + s, 0)
    nbytes = (2 * tm * (conv_dim + 2 * inner) * 2 + 2 * tm * LANES * 4
              + (tm + SUBLANES) * conv_dim * 4 + tm * conv_dim * 4 + tm * LANES * 4
              + SSM_STATE * inner * 4 + 3 * tm * conv_dim * 4 + 24 * CHUNK * inner * 4)
    return pl.pallas_call(
        functools.partial(_ssd_kernel, n_chunks=tm // CHUNK, ssm_inner=inner, heads=heads),
        grid=(batch, n_s),
        in_specs=[
            pl.BlockSpec((tm, conv_dim), row_map),
            pl.BlockSpec((tm, inner), row_map),
            pl.BlockSpec((tm, heads), row_map),
            _const_spec((CONV_WIDTH, conv_dim)),
            _const_spec((1, conv_dim)),
            _const_spec((1, heads)),
            _const_spec((1, heads)),
            _const_spec((1, inner)),
            _const_spec((1, inner)),
            _const_spec((heads, inner)),
        ],
        out_specs=pl.BlockSpec((tm, inner), row_map),
        out_shape=jax.ShapeDtypeStruct((t, inner), BF16),
        scratch_shapes=[
            pltpu.VMEM((SUBLANES, conv_dim), F32),
            pltpu.VMEM((tm, inner), F32),
            pltpu.VMEM((tm, conv_dim - inner), F32),
            pltpu.VMEM((tm, heads), F32),
            pltpu.VMEM((SSM_STATE, inner), F32),
        ],
        compiler_params=pltpu.CompilerParams(
            dimension_semantics=("arbitrary", "arbitrary"), vmem_limit_bytes=_vmem_limit(nbytes)),
        name="ssd",
    )(xbc, z, dt_raw, conv_w.astype(F32), conv_b.reshape(1, conv_dim).astype(F32),
      dt_bias.reshape(1, heads).astype(F32), a_log.reshape(1, heads).astype(F32), rep(d_skip),
      norm_w.reshape(1, inner).astype(F32), expand)


def _attn_key_chunks(qv):
    q_per_group = LANES // CHUNK
    return range(qv * q_per_group, qv * q_per_group + LEFT_CHUNKS + q_per_group)


def _attn_kernel(q_ref, k0_ref, k1_ref, k2_ref, v0_ref, v1_ref, v2_ref, bias_ref, o_ref,
                 s_ref, p_ref, *, heads):
    tq = q_ref.shape[0]
    n_slots = s_ref.shape[0]
    n_kc = s_ref.shape[1] // CHUNK
    n_qv = tq // LANES
    pair = LANES // ATTN_HEAD_DIM
    lane_head = lax.broadcasted_iota(jnp.int32, (tq, LANES), 1) // ATTN_HEAD_DIM
    for slot in range(n_slots):
        for qv in range(n_qv):
            for kc in range(n_kc):
                if kc not in _attn_key_chunks(qv):
                    p_ref[slot, kc * CHUNK:(kc + 1) * CHUNK, qv * LANES:(qv + 1) * LANES] = jnp.zeros(
                        (CHUNK, LANES), BF16)

    def scores(h):
        lo = (h // pair) * LANES
        qp = q_ref[:, lo:lo + LANES]
        kp = jnp.concatenate([k0_ref[:, lo:lo + LANES], k1_ref[:, lo:lo + LANES],
                              k2_ref[:, lo:lo + LANES]], axis=0)
        qm = jnp.where(lane_head == h % pair, qp, jnp.zeros_like(qp))
        s_ref[h % n_slots] = lax.dot_general(kp, qm, (((1,), (1,)), ((), ())),
                                             preferred_element_type=F32)

    def softmax(h):
        slot = h % n_slots
        inv = []
        for qv in range(n_qv):
            cols = slice(qv * LANES, (qv + 1) * LANES)
            mx = None
            for kc in _attn_key_chunks(qv):
                rows = slice(kc * CHUNK, (kc + 1) * CHUNK)
                t = s_ref[slot, rows, cols] + bias_ref[0, h, rows, cols]
                mx = t if mx is None else jnp.maximum(mx, t)
            m = jnp.max(mx, axis=0, keepdims=True)
            acc = None
            for kc in _attn_key_chunks(qv):
                rows = slice(kc * CHUNK, (kc + 1) * CHUNK)
                p = jnp.exp2(s_ref[slot, rows, cols] + bias_ref[0, h, rows, cols] - m)
                acc = p if acc is None else acc + p
                p_ref[slot, rows, cols] = p.astype(BF16)
            inv.append(1.0 / jnp.sum(acc, axis=0, keepdims=True))
        return jnp.concatenate(inv, axis=1)

    def weighted_values(h, inv):
        lo = (h // pair) * LANES
        d0 = (h % pair) * ATTN_HEAD_DIM
        vpt = jnp.concatenate([v0_ref[:, lo:lo + LANES], v1_ref[:, lo:lo + LANES],
                               v2_ref[:, lo:lo + LANES]], axis=0).T
        o_t = jnp.dot(vpt[d0:d0 + ATTN_HEAD_DIM, :], p_ref[h % n_slots],
                      preferred_element_type=F32)
        return o_t * inv

    outs = []

    def finish(h, inv):
        outs.append(weighted_values(h, inv))
        if len(outs) == pair:
            lo = (h // pair) * LANES
            o_ref[:, lo:lo + LANES] = jnp.concatenate(outs, axis=0).T.astype(o_ref.dtype)
            outs.clear()

    scores(0)
    inv = None
    for h in range(heads):
        if h + 1 < heads:
            scores(h + 1)
        if h > 0:
            finish(h - 1, inv)
        inv = softmax(h)
    finish(heads - 1, inv)


def _attn_bias_kernel(f_ref, o_ref):
    n_var, _, nk, tq = o_ref.shape
    width = f_ref.shape[2]
    x = jnp.broadcast_to(f_ref[0], (nk, width))
    rolled = pltpu.roll(x, 0, 1, stride=1, stride_axis=0)
    t = rolled[:, nk:] * LOG2_E
    key = lax.broadcasted_iota(jnp.int32, (nk, tq), 0)
    kc = key // CHUNK
    qc = lax.broadcasted_iota(jnp.int32, (nk, tq), 1) // CHUNK
    band = (kc >= qc) & (kc <= qc + LEFT_CHUNKS)
    for v in range(n_var):
        ok = band & (key >= (n_var - 1 - v) * tq)
        o_ref[v, 0] = jnp.where(ok, t, -jnp.inf)


def _attn_bias_table(rel_bias):
    heads = rel_bias.shape[0]
    nk = ATTN_K_BLOCKS * ATTN_Q_TILE
    width = nk + ATTN_Q_TILE
    dist = jnp.arange(width) - nk + LEFT_CHUNKS * CHUNK
    f = rel_bias.astype(F32)[:, jnp.clip(dist, -MAX_REL_DIST, MAX_REL_DIST) + MAX_REL_DIST]
    return pl.pallas_call(
        _attn_bias_kernel,
        grid=(heads,),
        in_specs=[pl.BlockSpec((1, 1, width), lambda h: (h, 0, 0))],
        out_specs=pl.BlockSpec((ATTN_K_BLOCKS, 1, nk, ATTN_Q_TILE), lambda h: (0, h, 0, 0)),
        out_shape=jax.ShapeDtypeStruct((ATTN_K_BLOCKS, heads, nk, ATTN_Q_TILE), F32),
        compiler_params=pltpu.CompilerParams(dimension_semantics=("arbitrary",)),
        name="attn_bias",
    )(f.reshape(heads, 1, width))


def _attention(q, k, v, rel_bias, *, batch, seq):
    t, inner = q.shape
    heads = rel_bias.shape[0]
    tq = ATTN_Q_TILE
    assert seq % tq == 0 and ATTN_K_BLOCKS == 3 and inner == heads * ATTN_HEAD_DIM
    n_q = seq // tq
    bias = _attn_bias_table(rel_bias)
    nk = ATTN_K_BLOCKS * tq

    def kmap(back):
        return lambda b, i: (b * n_q + jnp.maximum(i - back, 0), 0)

    blk = pl.BlockSpec((tq, inner), lambda b, i: (b * n_q + i, 0))
    kv_specs = [pl.BlockSpec((tq, inner), kmap(back)) for back in (2, 1, 0)]
    bias_spec = pl.BlockSpec((1, heads, nk, tq),
                             lambda b, i: (jnp.minimum(i, ATTN_K_BLOCKS - 1), 0, 0, 0))
    nbytes = 2 * 8 * tq * inner * 2 + 2 * heads * tq * nk * 4 + tq * nk * (4 + 2) + 4 * tq * nk * 4
    return pl.pallas_call(
        functools.partial(_attn_kernel, heads=heads),
        grid=(batch, n_q),
        in_specs=[blk] + kv_specs + kv_specs + [bias_spec],
        out_specs=blk,
        out_shape=jax.ShapeDtypeStruct((t, inner), BF16),
        scratch_shapes=[pltpu.VMEM((ATTN_SLOTS, nk, tq), F32), pltpu.VMEM((ATTN_SLOTS, nk, tq), BF16)],
        compiler_params=pltpu.CompilerParams(
            dimension_semantics=("arbitrary", "arbitrary"), vmem_limit_bytes=_vmem_limit(nbytes)),
        name="band_attn",
    )(q, k, k, k, v, v, v, bias)


def _merge_kernel(x_ref, ys_ref, ya_ref, g_ref, ws_ref, wa_ref, wo_ref, o_ref):
    d = x_ref.shape[1]
    bs = jnp.dot(ys_ref[...], ws_ref[...], preferred_element_type=F32)
    ba = jnp.dot(ya_ref[...], wa_ref[...], preferred_element_type=F32)
    g = g_ref[...].astype(F32)
    merged = jax.nn.sigmoid(g[:, :d]) * bs + jax.nn.sigmoid(g[:, d:]) * ba
    o_ref[...] = x_ref[...] + jnp.dot(merged.astype(BF16), wo_ref[...], preferred_element_type=F32)


def _merge(x2d, y_ssm, y_attn, g, w_bs, w_ba, w_out):
    t, d = x2d.shape
    tm = ROW_TILE
    row = lambda w: pl.BlockSpec((tm, w), lambda i: (i, 0))
    nbytes = (w_bs.size + w_ba.size + w_out.size) * 2 + 2 * tm * (2 * d * 4 + 4 * d * 2) + 6 * tm * d * 4
    return pl.pallas_call(
        _merge_kernel,
        grid=(t // tm,),
        in_specs=[row(d), row(y_ssm.shape[1]), row(y_attn.shape[1]), row(2 * d),
                  _const_spec(w_bs.shape), _const_spec(w_ba.shape), _const_spec(w_out.shape)],
        out_specs=row(d),
        out_shape=jax.ShapeDtypeStruct((t, d), F32),
        compiler_params=pltpu.CompilerParams(
            dimension_semantics=("arbitrary",), vmem_limit_bytes=_vmem_limit(nbytes)),
        name="merge",
    )(x2d, y_ssm, y_attn, g, w_bs.astype(BF16), w_ba.astype(BF16), w_out.astype(BF16))


def kernel(x, ffn1_norm_w, ffn1_w_gu, ffn1_w_down, mix_norm_w, w_in, conv_w, conv_b, dt_bias, A_log,
           D_skip, ssm_norm_w, rel_bias, w_branch_ssm, w_branch_attn, w_out, ffn2_norm_w, ffn2_w_gu,
           ffn2_w_down, final_norm_w):
    batch, seq, d = x.shape
    depth = ffn1_w_gu.shape[0]
    ssm_heads = A_log.shape[1]
    ssm_inner = ssm_heads * SSM_HEAD_DIM
    conv_dim = conv_w.shape[2]
    attn_inner = rel_bias.shape[1] * ATTN_HEAD_DIM
    xf = x.reshape(batch * seq, d)
    for l in range(depth):
        last = l == depth - 1
        xf = _ffn(xf, ffn1_norm_w[l], ffn1_w_gu[l], ffn1_w_down[l], final_norm_w, final_norm=False)
        z, xbc, dt_raw, q, k, v, g = _in_proj(
            xf, mix_norm_w[l], w_in[l], ssm_inner=ssm_inner, conv_dim=conv_dim,
            ssm_heads=ssm_heads, attn_inner=attn_inner)
        y_ssm = _ssd(xbc, z, dt_raw, conv_w[l], conv_b[l], dt_bias[l], A_log[l], D_skip[l],
                     ssm_norm_w[l], batch=batch, seq=seq)
        y_attn = _attention(q, k, v, rel_bias[l], batch=batch, seq=seq)
        xf = _merge(xf, y_ssm, y_attn, g, w_branch_ssm[l], w_branch_attn[l], w_out[l])
        xf = _ffn(xf, ffn2_norm_w[l], ffn2_w_gu[l], ffn2_w_down[l], final_norm_w, final_norm=last)
    if depth == 0:
        raise ValueError("depth must be >= 1")
    return xf.reshape(batch, seq, d)
```

```python
import functools

import jax
import jax.numpy as jnp
from jax import lax
from jax.experimental import pallas as pl
from jax.experimental.pallas import tpu as pltpu

F32 = jnp.float32
BF16 = jnp.bfloat16

CHUNK = 64
SSM_HEAD_DIM = 64
SSM_GROUPS = 2
SSM_STATE = 128
CONV_WIDTH = 4
ATTN_HEAD_DIM = 64
LEFT_CHUNKS = 8
MAX_REL_DIST = 128
FFN_RES_SCALE = 0.5
RMS_EPS = 1e-6
LOG2_E = 1.4426950408889634

LANES = 128
SUBLANES = 8
BF16_ROWS = 16
MXU_DIM = 256
VMEM_BYTES = 64 * 1024 * 1024

ROW_TILE = 512
FFN_COL_TILE = 512
SSD_ROW_TILE = 512
CONV_ROW_BLOCK = 128
ATTN_Q_CHUNKS = 4
ATTN_Q_TILE = ATTN_Q_CHUNKS * CHUNK
ATTN_SLOTS = 2
ATTN_K_BLOCKS = (LEFT_CHUNKS + ATTN_Q_CHUNKS) * CHUNK // ATTN_Q_TILE


def _vmem_limit(nbytes):
    return int(min(nbytes * 1.25 + (8 << 20), VMEM_BYTES - (6 << 20)))


def _const_spec(shape):
    zeros = (0,) * len(shape)
    return pl.BlockSpec(shape, lambda *_: zeros, pipeline_mode=pl.Buffered(1))


def _rmsnorm(x, w):
    ms = jnp.mean(x * x, axis=-1, keepdims=True)
    return x * lax.rsqrt(ms + RMS_EPS) * w


def _split3(x):
    hi = x.astype(BF16)
    r1 = x - hi.astype(F32)
    mid = r1.astype(BF16)
    lo = (r1 - mid.astype(F32)).astype(BF16)
    return hi, mid, lo


def _dot3(parts, w, *, parts_on_left):
    acc = None
    for p in parts:
        t = (jnp.dot(p, w, preferred_element_type=F32) if parts_on_left
             else jnp.dot(w, p, preferred_element_type=F32))
        acc = t if acc is None else acc + t
    return acc


def _ffn_kernel(x_ref, nw_ref, wgu_ref, wd_ref, fnw_ref, o_ref, h_ref, a_ref, *, d_ff, final_norm):
    h_ref[...] = _rmsnorm(x_ref[...], nw_ref[...]).astype(BF16)
    for c in range(0, d_ff, FFN_COL_TILE):
        w = min(FFN_COL_TILE, d_ff - c)
        g = jnp.dot(h_ref[...], wgu_ref[:, c:c + w], preferred_element_type=F32)
        u = jnp.dot(h_ref[...], wgu_ref[:, d_ff + c:d_ff + c + w], preferred_element_type=F32)
        a_ref[:, c:c + w] = (g * jax.nn.sigmoid(g) * u).astype(BF16)
    y = x_ref[...] + FFN_RES_SCALE * jnp.dot(a_ref[...], wd_ref[...], preferred_element_type=F32)
    if final_norm:
        y = _rmsnorm(y, fnw_ref[...])
    o_ref[...] = y


def _ffn(x2d, norm_w, w_gu, w_down, final_norm_w, *, final_norm):
    t, d = x2d.shape
    d_ff = w_down.shape[0]
    assert t % ROW_TILE == 0 and d_ff % LANES == 0
    tm = ROW_TILE
    nbytes = (3 * d * d_ff * 2 + 4 * tm * d * 4 + tm * d * 2 + tm * d_ff * 2
              + 2 * tm * FFN_COL_TILE * 4 + 2 * tm * d * 4)
    return pl.pallas_call(
        functools.partial(_ffn_kernel, d_ff=d_ff, final_norm=final_norm),
        grid=(t // tm,),
        in_specs=[
            pl.BlockSpec((tm, d), lambda i: (i, 0)),
            _const_spec((1, d)),
            _const_spec((d, 2 * d_ff)),
            _const_spec((d_ff, d)),
            _const_spec((1, d)),
        ],
        out_specs=pl.BlockSpec((tm, d), lambda i: (i, 0)),
        out_shape=jax.ShapeDtypeStruct((t, d), F32),
        scratch_shapes=[pltpu.VMEM((tm, d), BF16), pltpu.VMEM((tm, d_ff), BF16)],
        compiler_params=pltpu.CompilerParams(
            dimension_semantics=("arbitrary",), vmem_limit_bytes=_vmem_limit(nbytes)),
        name="ffn_final" if final_norm else "ffn",
    )(x2d, norm_w.reshape(1, d), w_gu.astype(BF16), w_down.astype(BF16), final_norm_w.reshape(1, d))


def _inproj_kernel(x_ref, nw_ref, wz_ref, wx_ref, wdt_ref, wq_ref, wk_ref, wv_ref, wg_ref,
                   z_ref, xbc_ref, dt_ref, q_ref, k_ref, v_ref, g_ref, h_ref, *, q_scale):
    h_ref[...] = _rmsnorm(x_ref[...], nw_ref[...]).astype(BF16)

    def project(w_ref, o_ref, scale=None):
        n = w_ref.shape[1]
        step = min(n, 2 * MXU_DIM)
        for c in range(0, n, step):
            r = jnp.dot(h_ref[...], w_ref[:, c:c + step], preferred_element_type=F32)
            if scale is not None:
                r = r * scale
            o_ref[:, c:c + step] = r.astype(o_ref.dtype)

    project(wz_ref, z_ref)
    project(wx_ref, xbc_ref)
    project(wdt_ref, dt_ref)
    project(wq_ref, q_ref, q_scale)
    project(wk_ref, k_ref)
    project(wv_ref, v_ref)
    project(wg_ref, g_ref)


def _in_proj(x2d, norm_w, w_in, *, ssm_inner, conv_dim, ssm_heads, attn_inner):
    t, d = x2d.shape
    tm = ROW_TILE
    sizes = (ssm_inner, conv_dim, ssm_heads, attn_inner, attn_inner, attn_inner)
    offs = [0]
    for s in sizes:
        offs.append(offs[-1] + s)
    offs.append(w_in.shape[1])
    ws = [w_in[:, offs[i]:offs[i + 1]].astype(BF16) for i in range(7)]
    widths = [w.shape[1] for w in ws]
    out_dtypes = [BF16, BF16, F32, BF16, BF16, BF16, BF16]
    nbytes = (sum(widths) * d * 2 + 2 * tm * d * 4 + tm * d * 2
              + 2 * sum(tm * w * jnp.dtype(dt).itemsize for w, dt in zip(widths, out_dtypes))
              + 2 * tm * 2 * MXU_DIM * 4)
    return pl.pallas_call(
        functools.partial(_inproj_kernel, q_scale=ATTN_HEAD_DIM ** -0.5 * LOG2_E),
        grid=(t // tm,),
        in_specs=[pl.BlockSpec((tm, d), lambda i: (i, 0)), _const_spec((1, d))]
                 + [_const_spec((d, w)) for w in widths],
        out_specs=[pl.BlockSpec((tm, w), lambda i: (i, 0)) for w in widths],
        out_shape=[jax.ShapeDtypeStruct((t, w), dt) for w, dt in zip(widths, out_dtypes)],
        scratch_shapes=[pltpu.VMEM((tm, d), BF16)],
        compiler_params=pltpu.CompilerParams(
            dimension_semantics=("arbitrary",), vmem_limit_bytes=_vmem_limit(nbytes)),
        name="in_proj",
    )(x2d, norm_w.reshape(1, d), *ws)


def _ssd_kernel(xbc_ref, z_ref, dtr_ref, convw_ref, convb_ref, dtb_ref, alog_ref, dskip_ref,
                normw_ref, expand_ref, o_ref,
                xpad_ref, xs_ref, bc_ref, dt_ref, state_ref,
                *, n_chunks, ssm_inner, heads):
    s_idx = pl.program_id(1)
    tm = xs_ref.shape[0]
    gn = SSM_GROUPS * SSM_STATE
    gw = ssm_inner // SSM_GROUPS

    halo = xpad_ref.shape[0] - tm

    @pl.when(s_idx == 0)
    def _():
        state_ref[...] = jnp.zeros_like(state_ref)
        xpad_ref[0:halo, :] = jnp.zeros((halo, xpad_ref.shape[1]), BF16)

    @pl.when(s_idx > 0)
    def _():
        xpad_ref[0:halo, :] = xpad_ref[tm:tm + halo, :]

    xpad_ref[halo:, :] = xbc_ref[...]

    blk = CONV_ROW_BLOCK
    srow = lax.broadcasted_iota(jnp.int32, ((CONV_WIDTH - 1) * blk, blk + halo), 0)
    scol = lax.broadcasted_iota(jnp.int32, ((CONV_WIDTH - 1) * blk, blk + halo), 1)
    shift_stack = (scol == (srow % blk) + halo - (srow // blk + 1)).astype(BF16)
    def shifted_rows(r0):
        return jnp.dot(shift_stack, xpad_ref[r0:r0 + blk + halo, :], preferred_element_type=F32)

    shifted_next = shifted_rows(0)
    for r0 in range(0, tm, blk):
        shifted = shifted_next
        if r0 + blk < tm:
            shifted_next = shifted_rows(r0 + blk)
        xblk = xpad_ref[r0 + halo:r0 + halo + blk, :]
        conv = xblk.astype(F32) * convw_ref[CONV_WIDTH - 1:CONV_WIDTH, :] + convb_ref[...]
        for k in range(1, CONV_WIDTH):
            conv = conv + shifted[(k - 1) * blk:k * blk, :] * convw_ref[CONV_WIDTH - 1 - k:CONV_WIDTH - k, :]
        conv = conv * jax.nn.sigmoid(conv)
        xs_ref[r0:r0 + blk, :] = conv[:, :ssm_inner]
        bc_ref[r0:r0 + blk, :] = conv[:, ssm_inner:]
    dt_ref[...] = jax.nn.softplus(dtr_ref[...] + dtb_ref[...])

    a_head = -jnp.exp(alog_ref[...]) * LOG2_E
    expand = expand_ref[...]
    row = lax.broadcasted_iota(jnp.int32, (CHUNK, CHUNK), 0)
    col = lax.broadcasted_iota(jnp.int32, (CHUNK, CHUNK), 1)
    tril = (col <= row).astype(BF16)
    prow = lax.broadcasted_iota(jnp.int32, (CHUNK, LANES), 0)
    plane = lax.broadcasted_iota(jnp.int32, (CHUNK, LANES), 1)
    causal_pair = (plane % SSM_HEAD_DIM) <= prow
    first_head = plane < SSM_HEAD_DIM
    drow = lax.broadcasted_iota(jnp.int32, (CHUNK, ssm_inner), 0)
    dlane = lax.broadcasted_iota(jnp.int32, (CHUNK, ssm_inner), 1)
    diag = (dlane % SSM_HEAD_DIM) == drow

    def decay_stage(c):
        rows = pl.ds(c * CHUNK, CHUNK)
        dt = dt_ref[rows, :]
        bc = bc_ref[rows, :].astype(BF16)
        dt_e = _dot3(_split3(dt), expand, parts_on_left=True)
        a_cum_h = _dot3(_split3(dt * a_head), tril, parts_on_left=False)
        a_cum = _dot3(_split3(a_cum_h), expand, parts_on_left=True)
        cbs = []
        for g in range(SSM_GROUPS):
            b_g = bc[:, g * SSM_STATE:(g + 1) * SSM_STATE]
            c_g = bc[:, gn + g * SSM_STATE:gn + (g + 1) * SSM_STATE]
            cbs.append(lax.dot_general(c_g, b_g, (((1,), (1,)), ((), ())),
                                       preferred_element_type=F32))
        return dt_e, a_cum, cbs

    def operand_stage(c, dt_e, a_cum, cbs):
        rows = pl.ds(c * CHUNK, CHUNK)
        xs = xs_ref[rows, :]
        a_last = a_cum[CHUNK - 1:CHUNK, :]
        xdt = xs * dt_e
        xdec = (xdt * jnp.exp2(a_last - a_cum)).astype(BF16)
        xdt_b = xdt.astype(BF16)
        a_t = jnp.sum(jnp.where(diag, a_cum, 0.0), axis=0, keepdims=True)
        ms, rhss = [], []
        for g in range(SSM_GROUPS):
            cb2 = jnp.concatenate([cbs[g], cbs[g]], axis=1)
            for j in range(gw // LANES):
                c0 = g * gw + j * LANES
                seg = a_cum[:, c0:c0 + LANES] - a_t[:, c0:c0 + LANES]
                ldec = jnp.exp2(jnp.where(causal_pair, seg, -jnp.inf))
                ms.append((cb2 * ldec).astype(BF16))
                xp = xdt_b[:, c0:c0 + LANES]
                zero = jnp.zeros_like(xp)
                rhss.append(jnp.concatenate([jnp.where(first_head, xp, zero),
                                             jnp.where(first_head, zero, xp)], axis=0))
        return xdec, ms, rhss

    def matmul_stage(c, a_cum, xdec, ms, rhss):
        rows = pl.ds(c * CHUNK, CHUNK)
        bc = bc_ref[rows, :]
        a_last = a_cum[CHUNK - 1:CHUNK, :]
        y_offs, y_diags = [], []
        for g in range(SSM_GROUPS):
            g0 = g * gw
            b_g = bc[:, g * SSM_STATE:(g + 1) * SSM_STATE]
            c_g = bc[:, gn + g * SSM_STATE:gn + (g + 1) * SSM_STATE].astype(BF16)
            prev = state_ref[:, g0:g0 + gw]
            y_offs.append(jnp.dot(c_g, prev.astype(BF16), preferred_element_type=F32))
            new = jnp.dot(b_g.T.astype(BF16), xdec[:, g0:g0 + gw], preferred_element_type=F32)
            state_ref[:, g0:g0 + gw] = prev * jnp.exp2(a_last[:, g0:g0 + gw]) + new
        for m, rhs in zip(ms, rhss):
            y_diags.append(jnp.dot(m, rhs, preferred_element_type=F32))
        return jnp.concatenate(y_offs, axis=1), jnp.concatenate(y_diags, axis=1)

    def output_stage(c, a_cum, y_off, y_diag):
        rows = pl.ds(c * CHUNK, CHUNK)
        y = y_diag + y_off * jnp.exp2(a_cum) + dskip_ref[...] * xs_ref[rows, :]
        zc = z_ref[rows, :].astype(F32)
        y = y * (zc * jax.nn.sigmoid(zc))
        outs = []
        for g in range(SSM_GROUPS):
            yg = y[:, g * gw:(g + 1) * gw]
            outs.append(yg * lax.rsqrt(jnp.mean(yg * yg, axis=-1, keepdims=True) + RMS_EPS))
        o_ref[rows, :] = (jnp.concatenate(outs, axis=1) * normw_ref[...]).astype(o_ref.dtype)

    decays = {c: decay_stage(c) for c in range(min(2, n_chunks))}
    operands = {0: operand_stage(0, *decays[0])}
    for c in range(n_chunks):
        a_cum = decays[c][1]
        y_off, y_diag = matmul_stage(c, a_cum, *operands.pop(c))
        if c + 1 < n_chunks:
            operands[c + 1] = operand_stage(c + 1, *decays[c + 1])
        if c + 2 < n_chunks:
            decays[c + 2] = decay_stage(c + 2)
        output_stage(c, a_cum, y_off, y_diag)
        del decays[c]


def _ssd(xbc, z, dt_raw, conv_w, conv_b, dt_bias, a_log, d_skip, norm_w, *, batch, seq):
    t, conv_dim = xbc.shape
    inner = z.shape[1]
    heads = dt_raw.shape[1]
    tm = SSD_ROW_TILE
    assert seq % tm == 0 and inner == heads * SSM_HEAD_DIM
    n_s = seq // tm
    rep = lambda p: jnp.repeat(p.astype(F32), SSM_HEAD_DIM).reshape(1, inner)
    expand = jnp.repeat(jnp.eye(heads, dtype=BF16), SSM_HEAD_DIM, axis=1)
    row_map = lambda b, s: (b * n_s + s, 0)
    nbytes = (2 * tm * (conv_dim + 2 * inner) * 2 + 2 * tm * LANES * 4
              + (tm + SUBLANES) * conv_dim * 4 + tm * conv_dim * 4 + tm * LANES * 4
              + SSM_STATE * inner * 4 + 3 * tm * conv_dim * 4 + 24 * CHUNK * inner * 4)
    return pl.pallas_call(
        functools.partial(_ssd_kernel, n_chunks=tm // CHUNK, ssm_inner=inner, heads=heads),
        grid=(batch, n_s),
        in_specs=[
            pl.BlockSpec((tm, conv_dim), row_map),
            pl.BlockSpec((tm, inner), row_map),
            pl.BlockSpec((tm, heads), row_map),
            _const_spec((CONV_WIDTH, conv_dim)),
            _const_spec((1, conv_dim)),
            _const_spec((1, heads)),
            _const_spec((1, heads)),
            _const_spec((1, inner)),
            _const_spec((1, inner)),
            _const_spec((heads, inner)),
        ],
        out_specs=pl.BlockSpec((tm, inner), row_map),
        out_shape=jax.ShapeDtypeStruct((t, inner), BF16),
        scratch_shapes=[
            pltpu.VMEM((tm + BF16_ROWS, conv_dim), BF16),
            pltpu.VMEM((tm, inner), F32),
            pltpu.VMEM((tm, conv_dim - inner), F32),
            pltpu.VMEM((tm, heads), F32),
            pltpu.VMEM((SSM_STATE, inner), F32),
        ],
        compiler_params=pltpu.CompilerParams(
            dimension_semantics=("arbitrary", "arbitrary"), vmem_limit_bytes=_vmem_limit(nbytes)),
        name="ssd",
    )(xbc, z, dt_raw, conv_w.astype(F32), conv_b.reshape(1, conv_dim).astype(F32),
      dt_bias.reshape(1, heads).astype(F32), a_log.reshape(1, heads).astype(F32), rep(d_skip),
      norm_w.reshape(1, inner).astype(F32), expand)


def _attn_key_chunks(qv):
    q_per_group = LANES // CHUNK
    return range(qv * q_per_group, qv * q_per_group + LEFT_CHUNKS + q_per_group)


def _attn_kernel(q_ref, k0_ref, k1_ref, k2_ref, v0_ref, v1_ref, v2_ref, bias_ref, o_ref,
                 s_ref, p_ref, *, heads):
    tq = q_ref.shape[0]
    n_slots = s_ref.shape[0]
    n_kc = s_ref.shape[1] // CHUNK
    n_qv = tq // LANES
    pair = LANES // ATTN_HEAD_DIM
    lane_head = lax.broadcasted_iota(jnp.int32, (tq, LANES), 1) // ATTN_HEAD_DIM
    for slot in range(n_slots):
        for qv in range(n_qv):
            for kc in range(n_kc):
                if kc not in _attn_key_chunks(qv):
                    p_ref[slot, kc * CHUNK:(kc + 1) * CHUNK, qv * LANES:(qv + 1) * LANES] = jnp.zeros(
                        (CHUNK, LANES), BF16)

    def scores(h):
        lo = (h // pair) * LANES
        qp = q_ref[:, lo:lo + LANES]
        kp = jnp.concatenate([k0_ref[:, lo:lo + LANES], k1_ref[:, lo:lo + LANES],
                              k2_ref[:, lo:lo + LANES]], axis=0)
        qm = jnp.where(lane_head == h % pair, qp, jnp.zeros_like(qp))
        s_ref[h % n_slots] = lax.dot_general(kp, qm, (((1,), (1,)), ((), ())),
                                             preferred_element_type=F32)

    def softmax(h, qv):
        slot = h % n_slots
        cols = slice(qv * LANES, (qv + 1) * LANES)
        mx = None
        for kc in _attn_key_chunks(qv):
            rows = slice(kc * CHUNK, (kc + 1) * CHUNK)
            t = s_ref[slot, rows, cols] + bias_ref[0, h, rows, cols]
            s_ref[slot, rows, cols] = t
            mx = t if mx is None else jnp.maximum(mx, t)
        m = jnp.max(mx, axis=0, keepdims=True)
        acc = None
        for kc in _attn_key_chunks(qv):
            rows = slice(kc * CHUNK, (kc + 1) * CHUNK)
            p = jnp.exp2(s_ref[slot, rows, cols] - m)
            acc = p if acc is None else acc + p
            p_ref[slot, rows, cols] = p.astype(BF16)
        return 1.0 / jnp.sum(acc, axis=0, keepdims=True)

    def weighted_values(h, inv):
        lo = (h // pair) * LANES
        d0 = (h % pair) * ATTN_HEAD_DIM
        vpt = jnp.concatenate([v0_ref[:, lo:lo + LANES], v1_ref[:, lo:lo + LANES],
                               v2_ref[:, lo:lo + LANES]], axis=0).T
        o_t = jnp.dot(vpt[d0:d0 + ATTN_HEAD_DIM, :], p_ref[h % n_slots],
                      preferred_element_type=F32)
        return o_t * inv

    outs = []

    def finish(h, inv):
        outs.append(weighted_values(h, inv))
        if len(outs) == pair:
            lo = (h // pair) * LANES
            o_ref[:, lo:lo + LANES] = jnp.concatenate(outs, axis=0).T.astype(o_ref.dtype)
            outs.clear()

    scores(0)
    inv = None
    for h in range(heads):
        if h + 1 < heads:
            scores(h + 1)
        parts = [softmax(h, 0)]
        if h > 0:
            finish(h - 1, inv)
        parts += [softmax(h, qv) for qv in range(1, n_qv)]
        inv = jnp.concatenate(parts, axis=1)
    finish(heads - 1, inv)


def _attn_bias_kernel(f_ref, o_ref):
    n_var, _, nk, tq = o_ref.shape
    width = f_ref.shape[2]
    x = jnp.broadcast_to(f_ref[0], (nk, width))
    rolled = pltpu.roll(x, 0, 1, stride=1, stride_axis=0)
    t = rolled[:, nk:] * LOG2_E
    key = lax.broadcasted_iota(jnp.int32, (nk, tq), 0)
    kc = key // CHUNK
    qc = lax.broadcasted_iota(jnp.int32, (nk, tq), 1) // CHUNK
    band = (kc >= qc) & (kc <= qc + LEFT_CHUNKS)
    for v in range(n_var):
        ok = band & (key >= (n_var - 1 - v) * tq)
        o_ref[v, 0] = jnp.where(ok, t, -jnp.inf)


def _attn_bias_table(rel_bias):
    heads = rel_bias.shape[0]
    nk = ATTN_K_BLOCKS * ATTN_Q_TILE
    width = nk + ATTN_Q_TILE
    dist = jnp.arange(width) - nk + LEFT_CHUNKS * CHUNK
    f = rel_bias.astype(F32)[:, jnp.clip(dist, -MAX_REL_DIST, MAX_REL_DIST) + MAX_REL_DIST]
    return pl.pallas_call(
        _attn_bias_kernel,
        grid=(heads,),
        in_specs=[pl.BlockSpec((1, 1, width), lambda h: (h, 0, 0))],
        out_specs=pl.BlockSpec((ATTN_K_BLOCKS, 1, nk, ATTN_Q_TILE), lambda h: (0, h, 0, 0)),
        out_shape=jax.ShapeDtypeStruct((ATTN_K_BLOCKS, heads, nk, ATTN_Q_TILE), F32),
        compiler_params=pltpu.CompilerParams(dimension_semantics=("arbitrary",)),
        name="attn_bias",
    )(f.reshape(heads, 1, width))


def _attention(q, k, v, rel_bias, *, batch, seq):
    t, inner = q.shape
    heads = rel_bias.shape[0]
    tq = ATTN_Q_TILE
    assert seq % tq == 0 and ATTN_K_BLOCKS == 3 and inner == heads * ATTN_HEAD_DIM
    n_q = seq // tq
    bias = _attn_bias_table(rel_bias)
    nk = ATTN_K_BLOCKS * tq

    def kmap(back):
        return lambda i, b: (b * n_q + jnp.maximum(i - back, 0), 0)

    blk = pl.BlockSpec((tq, inner), lambda i, b: (b * n_q + i, 0))
    kv_specs = [pl.BlockSpec((tq, inner), kmap(back)) for back in (2, 1, 0)]
    bias_spec = pl.BlockSpec((1, heads, nk, tq),
                             lambda i, b: (jnp.minimum(i, ATTN_K_BLOCKS - 1), 0, 0, 0))
    nbytes = 2 * 8 * tq * inner * 2 + 2 * heads * tq * nk * 4 + tq * nk * (4 + 2) + 4 * tq * nk * 4
    return pl.pallas_call(
        functools.partial(_attn_kernel, heads=heads),
        grid=(n_q, batch),
        in_specs=[blk] + kv_specs + kv_specs + [bias_spec],
        out_specs=blk,
        out_shape=jax.ShapeDtypeStruct((t, inner), BF16),
        scratch_shapes=[pltpu.VMEM((ATTN_SLOTS, nk, tq), F32), pltpu.VMEM((ATTN_SLOTS, nk, tq), BF16)],
        compiler_params=pltpu.CompilerParams(
            dimension_semantics=("arbitrary", "arbitrary"), vmem_limit_bytes=_vmem_limit(nbytes)),
        name="band_attn",
    )(q, k, k, k, v, v, v, bias)


def _merge_kernel(x_ref, ys_ref, ya_ref, g_ref, ws_ref, wa_ref, wo_ref, o_ref):
    d = x_ref.shape[1]
    bs = jnp.dot(ys_ref[...], ws_ref[...], preferred_element_type=F32)
    ba = jnp.dot(ya_ref[...], wa_ref[...], preferred_element_type=F32)
    g = g_ref[...].astype(F32)
    merged = jax.nn.sigmoid(g[:, :d]) * bs + jax.nn.sigmoid(g[:, d:]) * ba
    o_ref[...] = x_ref[...] + jnp.dot(merged.astype(BF16), wo_ref[...], preferred_element_type=F32)


def _merge(x2d, y_ssm, y_attn, g, w_bs, w_ba, w_out):
    t, d = x2d.shape
    tm = ROW_TILE
    row = lambda w: pl.BlockSpec((tm, w), lambda i: (i, 0))
    nbytes = (w_bs.size + w_ba.size + w_out.size) * 2 + 2 * tm * (2 * d * 4 + 4 * d * 2) + 6 * tm * d * 4
    return pl.pallas_call(
        _merge_kernel,
        grid=(t // tm,),
        in_specs=[row(d), row(y_ssm.shape[1]), row(y_attn.shape[1]), row(2 * d),
                  _const_spec(w_bs.shape), _const_spec(w_ba.shape), _const_spec(w_out.shape)],
        out_specs=row(d),
        out_shape=jax.ShapeDtypeStruct((t, d), F32),
        compiler_params=pltpu.CompilerParams(
            dimension_semantics=("arbitrary",), vmem_limit_bytes=_vmem_limit(nbytes)),
        name="merge",
    )(x2d, y_ssm, y_attn, g, w_bs.astype(BF16), w_ba.astype(BF16), w_out.astype(BF16))


def kernel(x, ffn1_norm_w, ffn1_w_gu, ffn1_w_down, mix_norm_w, w_in, conv_w, conv_b, dt_bias, A_log,
           D_skip, ssm_norm_w, rel_bias, w_branch_ssm, w_branch_attn, w_out, ffn2_norm_w, ffn2_w_gu,
           ffn2_w_down, final_norm_w):
    batch, seq, d = x.shape
    depth = ffn1_w_gu.shape[0]
    ssm_heads = A_log.shape[1]
    ssm_inner = ssm_heads * SSM_HEAD_DIM
    conv_dim = conv_w.shape[2]
    attn_inner = rel_bias.shape[1] * ATTN_HEAD_DIM
    xf = x.reshape(batch * seq, d)
    for l in range(depth):
        last = l == depth - 1
        xf = _ffn(xf, ffn1_norm_w[l], ffn1_w_gu[l], ffn1_w_down[l], final_norm_w, final_norm=False)
        z, xbc, dt_raw, q, k, v, g = _in_proj(
            xf, mix_norm_w[l], w_in[l], ssm_inner=ssm_inner, conv_dim=conv_dim,
            ssm_heads=ssm_heads, attn_inner=attn_inner)
        y_ssm = _ssd(xbc, z, dt_raw, conv_w[l], conv_b[l], dt_bias[l], A_log[l], D_skip[l],
                     ssm_norm_w[l], batch=batch, seq=seq)
        y_attn = _attention(q, k, v, rel_bias[l], batch=batch, seq=seq)
        xf = _merge(xf, y_ssm, y_attn, g, w_branch_ssm[l], w_branch_attn[l], w_out[l])
        xf = _ffn(xf, ffn2_norm_w[l], ffn2_w_gu[l], ffn2_w_down[l], final_norm_w, final_norm=last)
    if depth == 0:
        raise ValueError("depth must be >= 1")
    return xf.reshape(batch, seq, d)
```

```python
import functools

import jax
import jax.numpy as jnp
from jax import lax
from jax.experimental import pallas as pl
from jax.experimental.pallas import tpu as pltpu

F32 = jnp.float32
BF16 = jnp.bfloat16

CHUNK = 64
SSM_HEAD_DIM = 64
SSM_GROUPS = 2
SSM_STATE = 128
CONV_WIDTH = 4
ATTN_HEAD_DIM = 64
LEFT_CHUNKS = 8
MAX_REL_DIST = 128
FFN_RES_SCALE = 0.5
RMS_EPS = 1e-6
LOG2_E = 1.4426950408889634

LANES = 128
SUBLANES = 8
BF16_ROWS = 16
MXU_DIM = 256
VMEM_BYTES = 64 * 1024 * 1024

ROW_TILE = 512
FFN_ROW_TILE = 1024
FFN_COL_TILE = 512
SSD_ROW_TILE = 512
CONV_ROW_BLOCK = 128
ATTN_Q_CHUNKS = 4
ATTN_Q_TILE = ATTN_Q_CHUNKS * CHUNK
ATTN_SLOTS = 2
ATTN_K_BLOCKS = (LEFT_CHUNKS + ATTN_Q_CHUNKS) * CHUNK // ATTN_Q_TILE


def _vmem_limit(nbytes):
    return int(min(nbytes * 1.25 + (8 << 20), VMEM_BYTES - (6 << 20)))


def _const_spec(shape):
    zeros = (0,) * len(shape)
    return pl.BlockSpec(shape, lambda *_: zeros, pipeline_mode=pl.Buffered(1))


def _rmsnorm(x, w):
    ms = jnp.mean(x * x, axis=-1, keepdims=True)
    return x * lax.rsqrt(ms + RMS_EPS) * w


def _split3(x):
    hi = x.astype(BF16)
    r1 = x - hi.astype(F32)
    mid = r1.astype(BF16)
    lo = (r1 - mid.astype(F32)).astype(BF16)
    return hi, mid, lo


def _dot3(parts, w, *, parts_on_left):
    acc = None
    for p in parts:
        t = (jnp.dot(p, w, preferred_element_type=F32) if parts_on_left
             else jnp.dot(w, p, preferred_element_type=F32))
        acc = t if acc is None else acc + t
    return acc


def _ffn_kernel(x_ref, nw_ref, wgu_ref, wd_ref, fnw_ref, o_ref, h_ref, a_ref, *, d_ff, final_norm):
    half = x_ref.shape[0] // 2
    halves = (slice(0, half), slice(half, 2 * half))
    col_tiles = [(c, min(FFN_COL_TILE, d_ff - c)) for c in range(0, d_ff, FFN_COL_TILE)]

    def prologue(rows):
        h_ref[rows, :] = _rmsnorm(x_ref[rows, :], nw_ref[...]).astype(BF16)

    def gate_up(rows, c, w):
        h = h_ref[rows, :]
        g = jnp.dot(h, wgu_ref[:, c:c + w], preferred_element_type=F32)
        u = jnp.dot(h, wgu_ref[:, d_ff + c:d_ff + c + w], preferred_element_type=F32)
        a_ref[rows, c:c + w] = (g * jax.nn.sigmoid(g) * u).astype(BF16)

    def down(rows):
        return jnp.dot(a_ref[rows, :], wd_ref[...], preferred_element_type=F32)

    def epilogue(rows, acc):
        y = x_ref[rows, :] + FFN_RES_SCALE * acc
        if final_norm:
            y = _rmsnorm(y, fnw_ref[...])
        o_ref[rows, :] = y

    prologue(halves[0])
    for i, (c, w) in enumerate(col_tiles):
        gate_up(halves[0], c, w)
        if i == 0:
            prologue(halves[1])
    acc0 = down(halves[0])
    for i, (c, w) in enumerate(col_tiles):
        gate_up(halves[1], c, w)
        if i == 0:
            epilogue(halves[0], acc0)
    epilogue(halves[1], down(halves[1]))


def _ffn(x2d, norm_w, w_gu, w_down, final_norm_w, *, final_norm):
    t, d = x2d.shape
    d_ff = w_down.shape[0]
    assert t % FFN_ROW_TILE == 0 and d_ff % LANES == 0
    tm = FFN_ROW_TILE
    nbytes = (3 * d * d_ff * 2 + 4 * tm * d * 4 + tm * d * 2 + tm * d_ff * 2
              + 2 * tm * FFN_COL_TILE * 4 + 2 * tm * d * 4)
    return pl.pallas_call(
        functools.partial(_ffn_kernel, d_ff=d_ff, final_norm=final_norm),
        grid=(t // tm,),
        in_specs=[
            pl.BlockSpec((tm, d), lambda i: (i, 0)),
            _const_spec((1, d)),
            _const_spec((d, 2 * d_ff)),
            _const_spec((d_ff, d)),
            _const_spec((1, d)),
        ],
        out_specs=pl.BlockSpec((tm, d), lambda i: (i, 0)),
        out_shape=jax.ShapeDtypeStruct((t, d), F32),
        scratch_shapes=[pltpu.VMEM((tm, d), BF16), pltpu.VMEM((tm, d_ff), BF16)],
        compiler_params=pltpu.CompilerParams(
            dimension_semantics=("arbitrary",), vmem_limit_bytes=_vmem_limit(nbytes)),
        name="ffn_final" if final_norm else "ffn",
    )(x2d, norm_w.reshape(1, d), w_gu.astype(BF16), w_down.astype(BF16), final_norm_w.reshape(1, d))


def _inproj_kernel(x_ref, nw_ref, w_ref, z_ref, xbc_ref, q_ref, k_ref, v_ref, g_ref, dt_ref, h_ref,
                   *, q_scale):
    h_ref[...] = _rmsnorm(x_ref[...], nw_ref[...]).astype(BF16)
    col = 0
    for o_ref in (z_ref, xbc_ref, q_ref, k_ref, v_ref, g_ref, dt_ref):
        n = o_ref.shape[1]
        step = min(n, 2 * MXU_DIM)
        for c in range(0, n, step):
            r = jnp.dot(h_ref[...], w_ref[:, col + c:col + c + step], preferred_element_type=F32)
            if o_ref is q_ref:
                r = r * q_scale
            o_ref[:, c:c + step] = r.astype(o_ref.dtype)
        col += n


def _in_proj(x2d, norm_w, w_in, *, ssm_inner, conv_dim, ssm_heads, attn_inner):
    t, d = x2d.shape
    tm = ROW_TILE
    dt_lo = ssm_inner + conv_dim
    dt_hi = dt_lo + ssm_heads
    assert dt_lo % LANES == 0
    w = jnp.concatenate([w_in[:, :dt_lo], w_in[:, dt_hi:], w_in[:, dt_lo:dt_hi]], axis=1).astype(BF16)
    widths = [ssm_inner, conv_dim, attn_inner, attn_inner, attn_inner, w_in.shape[1] - dt_hi - 3 * attn_inner,
              ssm_heads]
    out_dtypes = [BF16, BF16, BF16, BF16, BF16, BF16, F32]
    nbytes = (sum(widths) * d * 2 + 2 * tm * d * 4 + tm * d * 2
              + 2 * sum(tm * wd * jnp.dtype(dt).itemsize for wd, dt in zip(widths, out_dtypes))
              + 2 * tm * 2 * MXU_DIM * 4)
    z, xbc, q, k, v, g, dt_raw = pl.pallas_call(
        functools.partial(_inproj_kernel, q_scale=ATTN_HEAD_DIM ** -0.5 * LOG2_E),
        grid=(t // tm,),
        in_specs=[pl.BlockSpec((tm, d), lambda i: (i, 0)), _const_spec((1, d)),
                  _const_spec((d, sum(widths)))],
        out_specs=[pl.BlockSpec((tm, wd), lambda i: (i, 0)) for wd in widths],
        out_shape=[jax.ShapeDtypeStruct((t, wd), dt) for wd, dt in zip(widths, out_dtypes)],
        scratch_shapes=[pltpu.VMEM((tm, d), BF16)],
        compiler_params=pltpu.CompilerParams(
            dimension_semantics=("arbitrary",), vmem_limit_bytes=_vmem_limit(nbytes)),
        name="in_proj",
    )(x2d, norm_w.reshape(1, d), w)
    return z, xbc, dt_raw, q, k, v, g


def _ssd_kernel(xbc_ref, z_ref, dtr_ref, convw_ref, convb_ref, dtb_ref, alog_ref, dskip_ref,
                normw_ref, expand_ref, o_ref,
                xpad_ref, xs_ref, bc_ref, dt_ref, state_ref,
                *, n_chunks, ssm_inner, heads):
    s_idx = pl.program_id(1)
    tm = xs_ref.shape[0]
    gn = SSM_GROUPS * SSM_STATE
    gw = ssm_inner // SSM_GROUPS

    halo = xpad_ref.shape[0] - tm

    @pl.when(s_idx == 0)
    def _():
        state_ref[...] = jnp.zeros_like(state_ref)
        xpad_ref[0:halo, :] = jnp.zeros((halo, xpad_ref.shape[1]), BF16)

    @pl.when(s_idx > 0)
    def _():
        xpad_ref[0:halo, :] = xpad_ref[tm:tm + halo, :]

    xpad_ref[halo:, :] = xbc_ref[...]

    blk = CONV_ROW_BLOCK
    srow = lax.broadcasted_iota(jnp.int32, ((CONV_WIDTH - 1) * blk, blk + halo), 0)
    scol = lax.broadcasted_iota(jnp.int32, ((CONV_WIDTH - 1) * blk, blk + halo), 1)
    shift_stack = (scol == (srow % blk) + halo - (srow // blk + 1)).astype(BF16)
    def shifted_rows(r0):
        return jnp.dot(shift_stack, xpad_ref[r0:r0 + blk + halo, :], preferred_element_type=F32)

    shifted_next = shifted_rows(0)
    for r0 in range(0, tm, blk):
        shifted = shifted_next
        if r0 + blk < tm:
            shifted_next = shifted_rows(r0 + blk)
        xblk = xpad_ref[r0 + halo:r0 + halo + blk, :]
        conv = xblk.astype(F32) * convw_ref[CONV_WIDTH - 1:CONV_WIDTH, :] + convb_ref[...]
        for k in range(1, CONV_WIDTH):
            conv = conv + shifted[(k - 1) * blk:k * blk, :] * convw_ref[CONV_WIDTH - 1 - k:CONV_WIDTH - k, :]
        conv = conv * jax.nn.sigmoid(conv)
        xs_ref[r0:r0 + blk, :] = conv[:, :ssm_inner]
        bc_ref[r0:r0 + blk, :] = conv[:, ssm_inner:]
    dt_ref[...] = jax.nn.softplus(dtr_ref[...] + dtb_ref[...])

    a_head = -jnp.exp(alog_ref[...]) * LOG2_E
    expand = expand_ref[...]
    row = lax.broadcasted_iota(jnp.int32, (CHUNK, CHUNK), 0)
    col = lax.broadcasted_iota(jnp.int32, (CHUNK, CHUNK), 1)
    tril = (col <= row).astype(BF16)
    prow = lax.broadcasted_iota(jnp.int32, (CHUNK, LANES), 0)
    plane = lax.broadcasted_iota(jnp.int32, (CHUNK, LANES), 1)
    causal_pair = (plane % SSM_HEAD_DIM) <= prow
    first_head = plane < SSM_HEAD_DIM
    drow = lax.broadcasted_iota(jnp.int32, (CHUNK, ssm_inner), 0)
    dlane = lax.broadcasted_iota(jnp.int32, (CHUNK, ssm_inner), 1)
    diag = (dlane % SSM_HEAD_DIM) == drow

    def decay_stage(c):
        rows = pl.ds(c * CHUNK, CHUNK)
        dt = dt_ref[rows, :]
        bc = bc_ref[rows, :].astype(BF16)
        dt_e = _dot3(_split3(dt), expand, parts_on_left=True)
        a_cum_h = _dot3(_split3(dt * a_head), tril, parts_on_left=False)
        a_cum = _dot3(_split3(a_cum_h), expand, parts_on_left=True)
        cbs = []
        for g in range(SSM_GROUPS):
            b_g = bc[:, g * SSM_STATE:(g + 1) * SSM_STATE]
            c_g = bc[:, gn + g * SSM_STATE:gn + (g + 1) * SSM_STATE]
            cbs.append(lax.dot_general(c_g, b_g, (((1,), (1,)), ((), ())),
                                       preferred_element_type=F32))
        return dt_e, a_cum, cbs

    def operand_stage(c, dt_e, a_cum, cbs):
        rows = pl.ds(c * CHUNK, CHUNK)
        xs = xs_ref[rows, :]
        a_last = a_cum[CHUNK - 1:CHUNK, :]
        xdt = xs * dt_e
        xdec = (xdt * jnp.exp2(a_last - a_cum)).astype(BF16)
        xdt_b = xdt.astype(BF16)
        a_t = jnp.sum(jnp.where(diag, a_cum, 0.0), axis=0, keepdims=True)
        ms, rhss = [], []
        for g in range(SSM_GROUPS):
            cb2 = jnp.concatenate([cbs[g], cbs[g]], axis=1)
            for j in range(gw // LANES):
                c0 = g * gw + j * LANES
                seg = a_cum[:, c0:c0 + LANES] - a_t[:, c0:c0 + LANES]
                ldec = jnp.exp2(jnp.where(causal_pair, seg, -jnp.inf))
                ms.append((cb2 * ldec).astype(BF16))
                xp = xdt_b[:, c0:c0 + LANES]
                zero = jnp.zeros_like(xp)
                rhss.append(jnp.concatenate([jnp.where(first_head, xp, zero),
                                             jnp.where(first_head, zero, xp)], axis=0))
        return xdec, ms, rhss

    def matmul_stage(c, a_cum, xdec, ms, rhss):
        rows = pl.ds(c * CHUNK, CHUNK)
        bc = bc_ref[rows, :]
        a_last = a_cum[CHUNK - 1:CHUNK, :]
        y_offs, y_diags = [], []
        for g in range(SSM_GROUPS):
            g0 = g * gw
            b_g = bc[:, g * SSM_STATE:(g + 1) * SSM_STATE]
            c_g = bc[:, gn + g * SSM_STATE:gn + (g + 1) * SSM_STATE].astype(BF16)
            prev = state_ref[:, g0:g0 + gw]
            y_offs.append(jnp.dot(c_g, prev.astype(BF16), preferred_element_type=F32))
            new = jnp.dot(b_g.T.astype(BF16), xdec[:, g0:g0 + gw], preferred_element_type=F32)
            state_ref[:, g0:g0 + gw] = prev * jnp.exp2(a_last[:, g0:g0 + gw]) + new
        for m, rhs in zip(ms, rhss):
            y_diags.append(jnp.dot(m, rhs, preferred_element_type=F32))
        return jnp.concatenate(y_offs, axis=1), jnp.concatenate(y_diags, axis=1)

    def output_stage(c, a_cum, y_off, y_diag):
        rows = pl.ds(c * CHUNK, CHUNK)
        y = y_diag + y_off * jnp.exp2(a_cum) + dskip_ref[...] * xs_ref[rows, :]
        zc = z_ref[rows, :].astype(F32)
        y = y * (zc * jax.nn.sigmoid(zc))
        outs = []
        for g in range(SSM_GROUPS):
            yg = y[:, g * gw:(g + 1) * gw]
            outs.append(yg * lax.rsqrt(jnp.mean(yg * yg, axis=-1, keepdims=True) + RMS_EPS))
        o_ref[rows, :] = (jnp.concatenate(outs, axis=1) * normw_ref[...]).astype(o_ref.dtype)

    decays = {c: decay_stage(c) for c in range(min(2, n_chunks))}
    operands = {0: operand_stage(0, *decays[0])}
    for c in range(n_chunks):
        a_cum = decays[c][1]
        y_off, y_diag = matmul_stage(c, a_cum, *operands.pop(c))
        if c + 1 < n_chunks:
            operands[c + 1] = operand_stage(c + 1, *decays[c + 1])
        if c + 2 < n_chunks:
            decays[c + 2] = decay_stage(c + 2)
        output_stage(c, a_cum, y_off, y_diag)
        del decays[c]


def _ssd(xbc, z, dt_raw, conv_w, conv_b, dt_bias, a_log, d_skip, norm_w, *, batch, seq):
    t, conv_dim = xbc.shape
    inner = z.shape[1]
    heads = dt_raw.shape[1]
    tm = SSD_ROW_TILE
    assert seq % tm == 0 and inner == heads * SSM_HEAD_DIM
    n_s = seq // tm
    rep = lambda p: jnp.repeat(p.astype(F32), SSM_HEAD_DIM).reshape(1, inner)
    expand = jnp.repeat(jnp.eye(heads, dtype=BF16), SSM_HEAD_DIM, axis=1)
    row_map = lambda b, s: (b * n_s + s, 0)
    nbytes = (2 * tm * (conv_dim + 2 * inner) * 2 + 2 * tm * LANES * 4
              + (tm + SUBLANES) * conv_dim * 4 + tm * conv_dim * 4 + tm * LANES * 4
              + SSM_STATE * inner * 4 + 3 * tm * conv_dim * 4 + 24 * CHUNK * inner * 4)
    return pl.pallas_call(
        functools.partial(_ssd_kernel, n_chunks=tm // CHUNK, ssm_inner=inner, heads=heads),
        grid=(batch, n_s),
        in_specs=[
            pl.BlockSpec((tm, conv_dim), row_map),
            pl.BlockSpec((tm, inner), row_map),
            pl.BlockSpec((tm, heads), row_map),
            _const_spec((CONV_WIDTH, conv_dim)),
            _const_spec((1, conv_dim)),
            _const_spec((1, heads)),
            _const_spec((1, heads)),
            _const_spec((1, inner)),
            _const_spec((1, inner)),
            _const_spec((heads, inner)),
        ],
        out_specs=pl.BlockSpec((tm, inner), row_map),
        out_shape=jax.ShapeDtypeStruct((t, inner), BF16),
        scratch_shapes=[
            pltpu.VMEM((tm + BF16_ROWS, conv_dim), BF16),
            pltpu.VMEM((tm, inner), F32),
            pltpu.VMEM((tm, conv_dim - inner), F32),
            pltpu.VMEM((tm, heads), F32),
            pltpu.VMEM((SSM_STATE, inner), F32),
        ],
        compiler_params=pltpu.CompilerParams(
            dimension_semantics=("arbitrary", "arbitrary"), vmem_limit_bytes=_vmem_limit(nbytes)),
        name="ssd",
    )(xbc, z, dt_raw, conv_w.astype(F32), conv_b.reshape(1, conv_dim).astype(F32),
      dt_bias.reshape(1, heads).astype(F32), a_log.reshape(1, heads).astype(F32), rep(d_skip),
      norm_w.reshape(1, inner).astype(F32), expand)


def _attn_key_chunks(qv):
    q_per_group = LANES // CHUNK
    return range(qv * q_per_group, qv * q_per_group + LEFT_CHUNKS + q_per_group)


def _attn_kernel(q_ref, k0_ref, k1_ref, k2_ref, v0_ref, v1_ref, v2_ref, bias_ref, o_ref,
                 s_ref, p_ref, *, heads):
    tq = q_ref.shape[0]
    n_slots = s_ref.shape[0]
    n_kc = s_ref.shape[1] // CHUNK
    n_qv = tq // LANES
    pair = LANES // ATTN_HEAD_DIM
    lane_head = lax.broadcasted_iota(jnp.int32, (tq, LANES), 1) // ATTN_HEAD_DIM
    for slot in range(n_slots):
        for qv in range(n_qv):
            for kc in range(n_kc):
                if kc not in _attn_key_chunks(qv):
                    p_ref[slot, kc * CHUNK:(kc + 1) * CHUNK, qv * LANES:(qv + 1) * LANES] = jnp.zeros(
                        (CHUNK, LANES), BF16)

    def scores(h):
        lo = (h // pair) * LANES
        qp = q_ref[:, lo:lo + LANES]
        kp = jnp.concatenate([k0_ref[:, lo:lo + LANES], k1_ref[:, lo:lo + LANES],
                              k2_ref[:, lo:lo + LANES]], axis=0)
        qm = jnp.where(lane_head == h % pair, qp, jnp.zeros_like(qp))
        s_ref[h % n_slots] = lax.dot_general(kp, qm, (((1,), (1,)), ((), ())),
                                             preferred_element_type=F32)

    def softmax(h, qv):
        slot = h % n_slots
        cols = slice(qv * LANES, (qv + 1) * LANES)
        mx = None
        for kc in _attn_key_chunks(qv):
            rows = slice(kc * CHUNK, (kc + 1) * CHUNK)
            t = s_ref[slot, rows, cols] + bias_ref[0, h, rows, cols]
            s_ref[slot, rows, cols] = t
            mx = t if mx is None else jnp.maximum(mx, t)
        m = jnp.max(mx, axis=0, keepdims=True)
        acc = None
        for kc in _attn_key_chunks(qv):
            rows = slice(kc * CHUNK, (kc + 1) * CHUNK)
            p = jnp.exp2(s_ref[slot, rows, cols] - m)
            acc = p if acc is None else acc + p
            p_ref[slot, rows, cols] = p.astype(BF16)
        return 1.0 / jnp.sum(acc, axis=0, keepdims=True)

    def weighted_values(h, inv):
        lo = (h // pair) * LANES
        d0 = (h % pair) * ATTN_HEAD_DIM
        vpt = jnp.concatenate([v0_ref[:, lo:lo + LANES], v1_ref[:, lo:lo + LANES],
                               v2_ref[:, lo:lo + LANES]], axis=0).T
        o_t = jnp.dot(vpt[d0:d0 + ATTN_HEAD_DIM, :], p_ref[h % n_slots],
                      preferred_element_type=F32)
        return o_t * inv

    outs = []

    def finish(h, inv):
        outs.append(weighted_values(h, inv))
        if len(outs) == pair:
            lo = (h // pair) * LANES
            o_ref[:, lo:lo + LANES] = jnp.concatenate(outs, axis=0).T.astype(o_ref.dtype)
            outs.clear()

    scores(0)
    inv = None
    for h in range(heads):
        if h + 1 < heads:
            scores(h + 1)
        if h > 0:
            finish(h - 1, inv)
        inv = jnp.concatenate([softmax(h, qv) for qv in range(n_qv)], axis=1)
    finish(heads - 1, inv)


def _attn_bias_kernel(f_ref, o_ref):
    n_var, _, nk, tq = o_ref.shape
    width = f_ref.shape[2]
    x = jnp.broadcast_to(f_ref[0], (nk, width))
    rolled = pltpu.roll(x, 0, 1, stride=1, stride_axis=0)
    t = rolled[:, nk:] * LOG2_E
    key = lax.broadcasted_iota(jnp.int32, (nk, tq), 0)
    kc = key // CHUNK
    qc = lax.broadcasted_iota(jnp.int32, (nk, tq), 1) // CHUNK
    band = (kc >= qc) & (kc <= qc + LEFT_CHUNKS)
    for v in range(n_var):
        ok = band & (key >= (n_var - 1 - v) * tq)
        o_ref[v, 0] = jnp.where(ok, t, -jnp.inf)


def _attn_bias_table(rel_bias):
    heads = rel_bias.shape[0]
    nk = ATTN_K_BLOCKS * ATTN_Q_TILE
    width = nk + ATTN_Q_TILE
    dist = jnp.arange(width) - nk + LEFT_CHUNKS * CHUNK
    f = rel_bias.astype(F32)[:, jnp.clip(dist, -MAX_REL_DIST, MAX_REL_DIST) + MAX_REL_DIST]
    return pl.pallas_call(
        _attn_bias_kernel,
        grid=(heads,),
        in_specs=[pl.BlockSpec((1, 1, width), lambda h: (h, 0, 0))],
        out_specs=pl.BlockSpec((ATTN_K_BLOCKS, 1, nk, ATTN_Q_TILE), lambda h: (0, h, 0, 0)),
        out_shape=jax.ShapeDtypeStruct((ATTN_K_BLOCKS, heads, nk, ATTN_Q_TILE), F32),
        compiler_params=pltpu.CompilerParams(dimension_semantics=("arbitrary",)),
        name="attn_bias",
    )(f.reshape(heads, 1, width))


def _attention(q, k, v, rel_bias, *, batch, seq):
    t, inner = q.shape
    heads = rel_bias.shape[0]
    tq = ATTN_Q_TILE
    assert seq % tq == 0 and ATTN_K_BLOCKS == 3 and inner == heads * ATTN_HEAD_DIM
    n_q = seq // tq
    bias = _attn_bias_table(rel_bias)
    nk = ATTN_K_BLOCKS * tq

    def kmap(back):
        return lambda i, b: (b * n_q + jnp.maximum(i - back, 0), 0)

    blk = pl.BlockSpec((tq, inner), lambda i, b: (b * n_q + i, 0))
    kv_specs = [pl.BlockSpec((tq, inner), kmap(back)) for back in (2, 1, 0)]
    bias_spec = pl.BlockSpec((1, heads, nk, tq),
                             lambda i, b: (jnp.minimum(i, ATTN_K_BLOCKS - 1), 0, 0, 0))
    nbytes = 2 * 8 * tq * inner * 2 + 2 * heads * tq * nk * 4 + tq * nk * (4 + 2) + 4 * tq * nk * 4
    return pl.pallas_call(
        functools.partial(_attn_kernel, heads=heads),
        grid=(n_q, batch),
        in_specs=[blk] + kv_specs + kv_specs + [bias_spec],
        out_specs=blk,
        out_shape=jax.ShapeDtypeStruct((t, inner), BF16),
        scratch_shapes=[pltpu.VMEM((ATTN_SLOTS, nk, tq), F32), pltpu.VMEM((ATTN_SLOTS, nk, tq), BF16)],
        compiler_params=pltpu.CompilerParams(
            dimension_semantics=("arbitrary", "arbitrary"), vmem_limit_bytes=_vmem_limit(nbytes)),
        name="band_attn",
    )(q, k, k, k, v, v, v, bias)


def _merge_kernel(x_ref, ys_ref, ya_ref, g_ref, ws_ref, wa_ref, wo_ref, o_ref):
    d = x_ref.shape[1]
    bs = jnp.dot(ys_ref[...], ws_ref[...], preferred_element_type=F32)
    ba = jnp.dot(ya_ref[...], wa_ref[...], preferred_element_type=F32)
    g = g_ref[...].astype(F32)
    merged = jax.nn.sigmoid(g[:, :d]) * bs + jax.nn.sigmoid(g[:, d:]) * ba
    o_ref[...] = x_ref[...] + jnp.dot(merged.astype(BF16), wo_ref[...], preferred_element_type=F32)


def _merge(x2d, y_ssm, y_attn, g, w_bs, w_ba, w_out):
    t, d = x2d.shape
    tm = ROW_TILE
    row = lambda w: pl.BlockSpec((tm, w), lambda i: (i, 0))
    nbytes = (w_bs.size + w_ba.size + w_out.size) * 2 + 2 * tm * (2 * d * 4 + 4 * d * 2) + 6 * tm * d * 4
    return pl.pallas_call(
        _merge_kernel,
        grid=(t // tm,),
        in_specs=[row(d), row(y_ssm.shape[1]), row(y_attn.shape[1]), row(2 * d),
                  _const_spec(w_bs.shape), _const_spec(w_ba.shape), _const_spec(w_out.shape)],
        out_specs=row(d),
        out_shape=jax.ShapeDtypeStruct((t, d), F32),
        compiler_params=pltpu.CompilerParams(
            dimension_semantics=("arbitrary",), vmem_limit_bytes=_vmem_limit(nbytes)),
        name="merge",
    )(x2d, y_ssm, y_attn, g, w_bs.astype(BF16), w_ba.astype(BF16), w_out.astype(BF16))


def kernel(x, ffn1_norm_w, ffn1_w_gu, ffn1_w_down, mix_norm_w, w_in, conv_w, conv_b, dt_bias, A_log,
           D_skip, ssm_norm_w, rel_bias, w_branch_ssm, w_branch_attn, w_out, ffn2_norm_w, ffn2_w_gu,
           ffn2_w_down, final_norm_w):
    batch, seq, d = x.shape
    depth = ffn1_w_gu.shape[0]
    ssm_heads = A_log.shape[1]
    ssm_inner = ssm_heads * SSM_HEAD_DIM
    conv_dim = conv_w.shape[2]
    attn_inner = rel_bias.shape[1] * ATTN_HEAD_DIM
    xf = x.reshape(batch * seq, d)
    for l in range(depth):
        last = l == depth - 1
        xf = _ffn(xf, ffn1_norm_w[l], ffn1_w_gu[l], ffn1_w_down[l], final_norm_w, final_norm=False)
        z, xbc, dt_raw, q, k, v, g = _in_proj(
            xf, mix_norm_w[l], w_in[l], ssm_inner=ssm_inner, conv_dim=conv_dim,
            ssm_heads=ssm_heads, attn_inner=attn_inner)
        y_ssm = _ssd(xbc, z, dt_raw, conv_w[l], conv_b[l], dt_bias[l], A_log[l], D_skip[l],
                     ssm_norm_w[l], batch=batch, seq=seq)
        y_attn = _attention(q, k, v, rel_bias[l], batch=batch, seq=seq)
        xf = _merge(xf, y_ssm, y_attn, g, w_branch_ssm[l], w_branch_attn[l], w_out[l])
        xf = _ffn(xf, ffn2_norm_w[l], ffn2_w_gu[l], ffn2_w_down[l], final_norm_w, final_norm=last)
    if depth == 0:
        raise ValueError("depth must be >= 1")
    return xf.reshape(batch, seq, d)
```

```python
import functools

import jax
import jax.numpy as jnp
from jax import lax
from jax.experimental import pallas as pl
from jax.experimental.pallas import tpu as pltpu

F32 = jnp.float32
BF16 = jnp.bfloat16

CHUNK = 64
SSM_HEAD_DIM = 64
SSM_GROUPS = 2
SSM_STATE = 128
CONV_WIDTH = 4
ATTN_HEAD_DIM = 64
LEFT_CHUNKS = 8
MAX_REL_DIST = 128
FFN_RES_SCALE = 0.5
RMS_EPS = 1e-6
LOG2_E = 1.4426950408889634

LANES = 128
SUBLANES = 8
BF16_ROWS = 16
MXU_DIM = 256
VMEM_BYTES = 64 * 1024 * 1024

ROW_TILE = 512
FFN_ROW_TILE = 1024
FFN_COL_TILE = 512
SSD_ROW_TILE = 1024
CONV_ROW_BLOCK = 128
ATTN_Q_CHUNKS = 4
ATTN_Q_TILE = ATTN_Q_CHUNKS * CHUNK
ATTN_TILES_PER_STEP = 2
ATTN_SLOTS = 2
ATTN_K_BLOCKS = (LEFT_CHUNKS + ATTN_Q_CHUNKS) * CHUNK // ATTN_Q_TILE


def _vmem_limit(nbytes):
    return int(min(nbytes * 1.25 + (8 << 20), VMEM_BYTES - (6 << 20)))


def _const_spec(shape):
    zeros = (0,) * len(shape)
    return pl.BlockSpec(shape, lambda *_: zeros, pipeline_mode=pl.Buffered(1))


def _rmsnorm(x, w):
    ms = jnp.mean(x * x, axis=-1, keepdims=True)
    return x * lax.rsqrt(ms + RMS_EPS) * w


def _split3(x):
    hi = x.astype(BF16)
    r1 = x - hi.astype(F32)
    mid = r1.astype(BF16)
    lo = (r1 - mid.astype(F32)).astype(BF16)
    return hi, mid, lo


def _dot3(parts, w, *, parts_on_left):
    acc = None
    for p in parts:
        t = (jnp.dot(p, w, preferred_element_type=F32) if parts_on_left
             else jnp.dot(w, p, preferred_element_type=F32))
        acc = t if acc is None else acc + t
    return acc


def _ffn_kernel(x_ref, nw_ref, wgu_ref, wd_ref, fnw_ref, o_ref, h_ref, a_ref, *, d_ff, final_norm):
    half = x_ref.shape[0] // 2
    halves = (slice(0, half), slice(half, 2 * half))
    col_tiles = [(c, min(FFN_COL_TILE, d_ff - c)) for c in range(0, d_ff, FFN_COL_TILE)]

    def prologue(rows):
        h_ref[rows, :] = _rmsnorm(x_ref[rows, :], nw_ref[...]).astype(BF16)

    def gate_up(rows, c, w):
        h = h_ref[rows, :]
        g = jnp.dot(h, wgu_ref[:, c:c + w], preferred_element_type=F32)
        u = jnp.dot(h, wgu_ref[:, d_ff + c:d_ff + c + w], preferred_element_type=F32)
        a_ref[rows, c:c + w] = (g * jax.nn.sigmoid(g) * u).astype(BF16)

    def down(rows):
        return jnp.dot(a_ref[rows, :], wd_ref[...], preferred_element_type=F32)

    def epilogue(rows, acc):
        y = x_ref[rows, :] + FFN_RES_SCALE * acc
        if final_norm:
            y = _rmsnorm(y, fnw_ref[...])
        o_ref[rows, :] = y

    prologue(halves[0])
    for i, (c, w) in enumerate(col_tiles):
        gate_up(halves[0], c, w)
        if i == 0:
            prologue(halves[1])
    acc0 = down(halves[0])
    for i, (c, w) in enumerate(col_tiles):
        gate_up(halves[1], c, w)
        if i == 0:
            epilogue(halves[0], acc0)
    epilogue(halves[1], down(halves[1]))


def _ffn(x2d, norm_w, w_gu, w_down, final_norm_w, *, final_norm):
    t, d = x2d.shape
    d_ff = w_down.shape[0]
    assert t % FFN_ROW_TILE == 0 and d_ff % LANES == 0
    tm = FFN_ROW_TILE
    nbytes = (3 * d * d_ff * 2 + 4 * tm * d * 4 + tm * d * 2 + tm * d_ff * 2
              + 2 * tm * FFN_COL_TILE * 4 + 2 * tm * d * 4)
    return pl.pallas_call(
        functools.partial(_ffn_kernel, d_ff=d_ff, final_norm=final_norm),
        grid=(t // tm,),
        in_specs=[
            pl.BlockSpec((tm, d), lambda i: (i, 0)),
            _const_spec((1, d)),
            _const_spec((d, 2 * d_ff)),
            _const_spec((d_ff, d)),
            _const_spec((1, d)),
        ],
        out_specs=pl.BlockSpec((tm, d), lambda i: (i, 0)),
        out_shape=jax.ShapeDtypeStruct((t, d), F32),
        scratch_shapes=[pltpu.VMEM((tm, d), BF16), pltpu.VMEM((tm, d_ff), BF16)],
        compiler_params=pltpu.CompilerParams(
            dimension_semantics=("arbitrary",), vmem_limit_bytes=_vmem_limit(nbytes)),
        name="ffn_final" if final_norm else "ffn",
    )(x2d, norm_w.reshape(1, d), w_gu.astype(BF16), w_down.astype(BF16), final_norm_w.reshape(1, d))


def _inproj_kernel(x_ref, nw_ref, w_ref, z_ref, xbc_ref, q_ref, k_ref, v_ref, g_ref, dt_ref, h_ref,
                   *, q_scale):
    h_ref[...] = _rmsnorm(x_ref[...], nw_ref[...]).astype(BF16)
    col = 0
    for o_ref in (z_ref, xbc_ref, q_ref, k_ref, v_ref, g_ref, dt_ref):
        n = o_ref.shape[1]
        step = min(n, 2 * MXU_DIM)
        for c in range(0, n, step):
            r = jnp.dot(h_ref[...], w_ref[:, col + c:col + c + step], preferred_element_type=F32)
            if o_ref is q_ref:
                r = r * q_scale
            o_ref[:, c:c + step] = r.astype(o_ref.dtype)
        col += n


def _in_proj(x2d, norm_w, w_in, *, ssm_inner, conv_dim, ssm_heads, attn_inner):
    t, d = x2d.shape
    tm = ROW_TILE
    dt_lo = ssm_inner + conv_dim
    dt_hi = dt_lo + ssm_heads
    assert dt_lo % LANES == 0
    w = jnp.concatenate([w_in[:, :dt_lo], w_in[:, dt_hi:], w_in[:, dt_lo:dt_hi]], axis=1).astype(BF16)
    widths = [ssm_inner, conv_dim, attn_inner, attn_inner, attn_inner, w_in.shape[1] - dt_hi - 3 * attn_inner,
              ssm_heads]
    out_dtypes = [BF16, BF16, BF16, BF16, BF16, BF16, F32]
    nbytes = (sum(widths) * d * 2 + 2 * tm * d * 4 + tm * d * 2
              + 2 * sum(tm * wd * jnp.dtype(dt).itemsize for wd, dt in zip(widths, out_dtypes))
              + 2 * tm * 2 * MXU_DIM * 4)
    z, xbc, q, k, v, g, dt_raw = pl.pallas_call(
        functools.partial(_inproj_kernel, q_scale=ATTN_HEAD_DIM ** -0.5 * LOG2_E),
        grid=(t // tm,),
        in_specs=[pl.BlockSpec((tm, d), lambda i: (i, 0)), _const_spec((1, d)),
                  _const_spec((d, sum(widths)))],
        out_specs=[pl.BlockSpec((tm, wd), lambda i: (i, 0)) for wd in widths],
        out_shape=[jax.ShapeDtypeStruct((t, wd), dt) for wd, dt in zip(widths, out_dtypes)],
        scratch_shapes=[pltpu.VMEM((tm, d), BF16)],
        compiler_params=pltpu.CompilerParams(
            dimension_semantics=("arbitrary",), vmem_limit_bytes=_vmem_limit(nbytes)),
        name="in_proj",
    )(x2d, norm_w.reshape(1, d), w)
    return z, xbc, dt_raw, q, k, v, g


def _ssd_kernel(xbc_ref, z_ref, dtr_ref, convw_ref, convb_ref, dtb_ref, alog_ref, dskip_ref,
                normw_ref, expand_ref, o_ref,
                xpad_ref, xs_ref, bc_ref, dt_ref, state_ref,
                *, n_chunks, ssm_inner, heads):
    s_idx = pl.program_id(1)
    tm = xs_ref.shape[0]
    gn = SSM_GROUPS * SSM_STATE
    gw = ssm_inner // SSM_GROUPS

    halo = xpad_ref.shape[0] - tm

    @pl.when(s_idx == 0)
    def _():
        state_ref[...] = jnp.zeros_like(state_ref)
        xpad_ref[0:halo, :] = jnp.zeros((halo, xpad_ref.shape[1]), BF16)

    @pl.when(s_idx > 0)
    def _():
        xpad_ref[0:halo, :] = xpad_ref[tm:tm + halo, :]

    xpad_ref[halo:, :] = xbc_ref[...]

    blk = CONV_ROW_BLOCK
    srow = lax.broadcasted_iota(jnp.int32, ((CONV_WIDTH - 1) * blk, blk + halo), 0)
    scol = lax.broadcasted_iota(jnp.int32, ((CONV_WIDTH - 1) * blk, blk + halo), 1)
    shift_stack = (scol == (srow % blk) + halo - (srow // blk + 1)).astype(BF16)
    def shifted_rows(r0):
        return jnp.dot(shift_stack, xpad_ref[r0:r0 + blk + halo, :], preferred_element_type=F32)

    shifted_next = shifted_rows(0)
    for r0 in range(0, tm, blk):
        shifted = shifted_next
        if r0 + blk < tm:
            shifted_next = shifted_rows(r0 + blk)
        xblk = xpad_ref[r0 + halo:r0 + halo + blk, :]
        conv = xblk.astype(F32) * convw_ref[CONV_WIDTH - 1:CONV_WIDTH, :] + convb_ref[...]
        for k in range(1, CONV_WIDTH):
            conv = conv + shifted[(k - 1) * blk:k * blk, :] * convw_ref[CONV_WIDTH - 1 - k:CONV_WIDTH - k, :]
        conv = conv * jax.nn.sigmoid(conv)
        xs_ref[r0:r0 + blk, :] = conv[:, :ssm_inner]
        bc_ref[r0:r0 + blk, :] = conv[:, ssm_inner:]
    dt_ref[...] = jax.nn.softplus(dtr_ref[...] + dtb_ref[...])

    a_head = -jnp.exp(alog_ref[...]) * LOG2_E
    expand = expand_ref[...]
    row = lax.broadcasted_iota(jnp.int32, (CHUNK, CHUNK), 0)
    col = lax.broadcasted_iota(jnp.int32, (CHUNK, CHUNK), 1)
    tril = (col <= row).astype(BF16)
    prow = lax.broadcasted_iota(jnp.int32, (CHUNK, LANES), 0)
    plane = lax.broadcasted_iota(jnp.int32, (CHUNK, LANES), 1)
    causal_pair = (plane % SSM_HEAD_DIM) <= prow
    first_head = plane < SSM_HEAD_DIM
    drow = lax.broadcasted_iota(jnp.int32, (CHUNK, ssm_inner), 0)
    dlane = lax.broadcasted_iota(jnp.int32, (CHUNK, ssm_inner), 1)
    diag = (dlane % SSM_HEAD_DIM) == drow

    def decay_stage(c):
        rows = pl.ds(c * CHUNK, CHUNK)
        dt = dt_ref[rows, :]
        bc = bc_ref[rows, :].astype(BF16)
        dt_e = _dot3(_split3(dt), expand, parts_on_left=True)
        a_cum_h = _dot3(_split3(dt * a_head), tril, parts_on_left=False)
        a_cum = _dot3(_split3(a_cum_h), expand, parts_on_left=True)
        cbs = []
        for g in range(SSM_GROUPS):
            b_g = bc[:, g * SSM_STATE:(g + 1) * SSM_STATE]
            c_g = bc[:, gn + g * SSM_STATE:gn + (g + 1) * SSM_STATE]
            cbs.append(lax.dot_general(c_g, b_g, (((1,), (1,)), ((), ())),
                                       preferred_element_type=F32))
        return dt_e, a_cum, cbs

    def operand_stage(c, dt_e, a_cum, cbs):
        rows = pl.ds(c * CHUNK, CHUNK)
        xs = xs_ref[rows, :]
        a_last = a_cum[CHUNK - 1:CHUNK, :]
        xdt = xs * dt_e
        xdec = (xdt * jnp.exp2(a_last - a_cum)).astype(BF16)
        xdt_b = xdt.astype(BF16)
        a_t = jnp.sum(jnp.where(diag, a_cum, 0.0), axis=0, keepdims=True)
        ms, rhss = [], []
        for g in range(SSM_GROUPS):
            cb2 = jnp.concatenate([cbs[g], cbs[g]], axis=1)
            for j in range(gw // LANES):
                c0 = g * gw + j * LANES
                seg = a_cum[:, c0:c0 + LANES] - a_t[:, c0:c0 + LANES]
                ldec = jnp.exp2(jnp.where(causal_pair, seg, -jnp.inf))
                ms.append((cb2 * ldec).astype(BF16))
                xp = xdt_b[:, c0:c0 + LANES]
                zero = jnp.zeros_like(xp)
                rhss.append(jnp.concatenate([jnp.where(first_head, xp, zero),
                                             jnp.where(first_head, zero, xp)], axis=0))
        return xdec, ms, rhss

    def matmul_stage(c, a_cum, xdec, ms, rhss):
        rows = pl.ds(c * CHUNK, CHUNK)
        bc = bc_ref[rows, :]
        a_last = a_cum[CHUNK - 1:CHUNK, :]
        y_offs, y_diags = [], []
        for g in range(SSM_GROUPS):
            g0 = g * gw
            b_g = bc[:, g * SSM_STATE:(g + 1) * SSM_STATE]
            c_g = bc[:, gn + g * SSM_STATE:gn + (g + 1) * SSM_STATE].astype(BF16)
            prev = state_ref[:, g0:g0 + gw]
            y_offs.append(jnp.dot(c_g, prev.astype(BF16), preferred_element_type=F32))
            new = jnp.dot(b_g.T.astype(BF16), xdec[:, g0:g0 + gw], preferred_element_type=F32)
            state_ref[:, g0:g0 + gw] = prev * jnp.exp2(a_last[:, g0:g0 + gw]) + new
        for m, rhs in zip(ms, rhss):
            y_diags.append(jnp.dot(m, rhs, preferred_element_type=F32))
        return jnp.concatenate(y_offs, axis=1), jnp.concatenate(y_diags, axis=1)

    def output_stage(c, a_cum, y_off, y_diag):
        rows = pl.ds(c * CHUNK, CHUNK)
        y = y_diag + y_off * jnp.exp2(a_cum) + dskip_ref[...] * xs_ref[rows, :]
        zc = z_ref[rows, :].astype(F32)
        y = y * (zc * jax.nn.sigmoid(zc))
        outs = []
        for g in range(SSM_GROUPS):
            yg = y[:, g * gw:(g + 1) * gw]
            outs.append(yg * lax.rsqrt(jnp.mean(yg * yg, axis=-1, keepdims=True) + RMS_EPS))
        o_ref[rows, :] = (jnp.concatenate(outs, axis=1) * normw_ref[...]).astype(o_ref.dtype)

    decays = {c: decay_stage(c) for c in range(min(2, n_chunks))}
    operands = {0: operand_stage(0, *decays[0])}
    for c in range(n_chunks):
        a_cum = decays[c][1]
        y_off, y_diag = matmul_stage(c, a_cum, *operands.pop(c))
        if c + 1 < n_chunks:
            operands[c + 1] = operand_stage(c + 1, *decays[c + 1])
        if c + 2 < n_chunks:
            decays[c + 2] = decay_stage(c + 2)
        output_stage(c, a_cum, y_off, y_diag)
        del decays[c]


def _ssd(xbc, z, dt_raw, conv_w, conv_b, dt_bias, a_log, d_skip, norm_w, *, batch, seq):
    t, conv_dim = xbc.shape
    inner = z.shape[1]
    heads = dt_raw.shape[1]
    tm = SSD_ROW_TILE
    assert seq % tm == 0 and inner == heads * SSM_HEAD_DIM
    n_s = seq // tm
    rep = lambda p: jnp.repeat(p.astype(F32), SSM_HEAD_DIM).reshape(1, inner)
    expand = jnp.repeat(jnp.eye(heads, dtype=BF16), SSM_HEAD_DIM, axis=1)
    row_map = lambda b, s: (b * n_s + s, 0)
    nbytes = (2 * tm * (conv_dim + 2 * inner) * 2 + 2 * tm * LANES * 4
              + (tm + SUBLANES) * conv_dim * 4 + tm * conv_dim * 4 + tm * LANES * 4
              + SSM_STATE * inner * 4 + 3 * tm * conv_dim * 4 + 24 * CHUNK * inner * 4)
    return pl.pallas_call(
        functools.partial(_ssd_kernel, n_chunks=tm // CHUNK, ssm_inner=inner, heads=heads),
        grid=(batch, n_s),
        in_specs=[
            pl.BlockSpec((tm, conv_dim), row_map),
            pl.BlockSpec((tm, inner), row_map),
            pl.BlockSpec((tm, heads), row_map),
            _const_spec((CONV_WIDTH, conv_dim)),
            _const_spec((1, conv_dim)),
            _const_spec((1, heads)),
            _const_spec((1, heads)),
            _const_spec((1, inner)),
            _const_spec((1, inner)),
            _const_spec((heads, inner)),
        ],
        out_specs=pl.BlockSpec((tm, inner), row_map),
        out_shape=jax.ShapeDtypeStruct((t, inner), BF16),
        scratch_shapes=[
            pltpu.VMEM((tm + BF16_ROWS, conv_dim), BF16),
            pltpu.VMEM((tm, inner), F32),
            pltpu.VMEM((tm, conv_dim - inner), F32),
            pltpu.VMEM((tm, heads), F32),
            pltpu.VMEM((SSM_STATE, inner), F32),
        ],
        compiler_params=pltpu.CompilerParams(
            dimension_semantics=("arbitrary", "arbitrary"), vmem_limit_bytes=_vmem_limit(nbytes)),
        name="ssd",
    )(xbc, z, dt_raw, conv_w.astype(F32), conv_b.reshape(1, conv_dim).astype(F32),
      dt_bias.reshape(1, heads).astype(F32), a_log.reshape(1, heads).astype(F32), rep(d_skip),
      norm_w.reshape(1, inner).astype(F32), expand)


def _attn_key_chunks(qv):
    q_per_group = LANES // CHUNK
    return range(qv * q_per_group, qv * q_per_group + LEFT_CHUNKS + q_per_group)


def _attn_kernel(q_ref, kprev_ref, kcur_ref, vprev_ref, vcur_ref, bias0_ref, bias1_ref, o_ref,
                 s_ref, p_ref, *, heads):
    tq = s_ref.shape[2]
    n_slots = s_ref.shape[0]
    n_kc = s_ref.shape[1] // CHUNK
    n_qv = tq // LANES
    pair = LANES // ATTN_HEAD_DIM
    step_rows = q_ref.shape[0]
    bias_refs = (bias0_ref, bias1_ref)
    lane_head = lax.broadcasted_iota(jnp.int32, (tq, LANES), 1) // ATTN_HEAD_DIM

    def window(prev_ref, cur_ref, tile, lo):
        first = tile * tq
        last = first + n_kc * CHUNK - step_rows
        return jnp.concatenate([prev_ref[first:step_rows, lo:lo + LANES],
                                cur_ref[0:last, lo:lo + LANES]], axis=0)
    for slot in range(n_slots):
        for qv in range(n_qv):
            for kc in range(n_kc):
                if kc not in _attn_key_chunks(qv):
                    p_ref[slot, kc * CHUNK:(kc + 1) * CHUNK, qv * LANES:(qv + 1) * LANES] = jnp.zeros(
                        (CHUNK, LANES), BF16)

    items = [(tile, h) for tile in range(step_rows // tq) for h in range(heads)]

    def scores(u):
        tile, h = items[u]
        lo = (h // pair) * LANES
        qp = q_ref[tile * tq:(tile + 1) * tq, lo:lo + LANES]
        kp = window(kprev_ref, kcur_ref, tile, lo)
        qm = jnp.where(lane_head == h % pair, qp, jnp.zeros_like(qp))
        s_ref[u % n_slots] = lax.dot_general(kp, qm, (((1,), (1,)), ((), ())),
                                             preferred_element_type=F32)

    def softmax(u, qv):
        tile, h = items[u]
        bias_ref = bias_refs[tile]
        slot = u % n_slots
        cols = slice(qv * LANES, (qv + 1) * LANES)
        mx = None
        for kc in _attn_key_chunks(qv):
            rows = slice(kc * CHUNK, (kc + 1) * CHUNK)
            t = s_ref[slot, rows, cols] + bias_ref[0, h, rows, cols]
            s_ref[slot, rows, cols] = t
            mx = t if mx is None else jnp.maximum(mx, t)
        m = jnp.max(mx, axis=0, keepdims=True)
        acc = None
        for kc in _attn_key_chunks(qv):
            rows = slice(kc * CHUNK, (kc + 1) * CHUNK)
            p = jnp.exp2(s_ref[slot, rows, cols] - m)
            acc = p if acc is None else acc + p
            p_ref[slot, rows, cols] = p.astype(BF16)
        return 1.0 / jnp.sum(acc, axis=0, keepdims=True)

    def weighted_values(u, inv):
        tile, h = items[u]
        lo = (h // pair) * LANES
        d0 = (h % pair) * ATTN_HEAD_DIM
        vpt = window(vprev_ref, vcur_ref, tile, lo).T
        o_t = jnp.dot(vpt[d0:d0 + ATTN_HEAD_DIM, :], p_ref[u % n_slots],
                      preferred_element_type=F32)
        return o_t * inv

    outs = []

    def finish(u, inv):
        tile, h = items[u]
        outs.append(weighted_values(u, inv))
        if len(outs) == pair:
            lo = (h // pair) * LANES
            o_ref[tile * tq:(tile + 1) * tq, lo:lo + LANES] = jnp.concatenate(
                outs, axis=0).T.astype(o_ref.dtype)
            outs.clear()

    scores(0)
    inv = None
    for u in range(len(items)):
        if u + 1 < len(items):
            scores(u + 1)
        if u > 0:
            finish(u - 1, inv)
        inv = jnp.concatenate([softmax(u, qv) for qv in range(n_qv)], axis=1)
    finish(len(items) - 1, inv)


def _attn_bias_kernel(f_ref, o_ref):
    n_var, _, nk, tq = o_ref.shape
    width = f_ref.shape[2]
    x = jnp.broadcast_to(f_ref[0], (nk, width))
    rolled = pltpu.roll(x, 0, 1, stride=1, stride_axis=0)
    t = rolled[:, nk:] * LOG2_E
    key = lax.broadcasted_iota(jnp.int32, (nk, tq), 0)
    kc = key // CHUNK
    qc = lax.broadcasted_iota(jnp.int32, (nk, tq), 1) // CHUNK
    band = (kc >= qc) & (kc <= qc + LEFT_CHUNKS)
    for v in range(n_var):
        ok = band & (key >= (n_var - 1 - v) * tq)
        o_ref[v, 0] = jnp.where(ok, t, -jnp.inf)


def _attn_bias_table(rel_bias):
    heads = rel_bias.shape[0]
    nk = ATTN_K_BLOCKS * ATTN_Q_TILE
    width = nk + ATTN_Q_TILE
    dist = jnp.arange(width) - nk + LEFT_CHUNKS * CHUNK
    f = rel_bias.astype(F32)[:, jnp.clip(dist, -MAX_REL_DIST, MAX_REL_DIST) + MAX_REL_DIST]
    return pl.pallas_call(
        _attn_bias_kernel,
        grid=(heads,),
        in_specs=[pl.BlockSpec((1, 1, width), lambda h: (h, 0, 0))],
        out_specs=pl.BlockSpec((ATTN_K_BLOCKS, 1, nk, ATTN_Q_TILE), lambda h: (0, h, 0, 0)),
        out_shape=jax.ShapeDtypeStruct((ATTN_K_BLOCKS, heads, nk, ATTN_Q_TILE), F32),
        compiler_params=pltpu.CompilerParams(dimension_semantics=("arbitrary",)),
        name="attn_bias",
    )(f.reshape(heads, 1, width))


def _attention(q, k, v, rel_bias, *, batch, seq):
    t, inner = q.shape
    heads = rel_bias.shape[0]
    tq = ATTN_Q_TILE
    rows = ATTN_TILES_PER_STEP * tq
    assert ATTN_TILES_PER_STEP == 2 and ATTN_K_BLOCKS == ATTN_TILES_PER_STEP + 1
    assert seq % rows == 0 and inner == heads * ATTN_HEAD_DIM
    n_s = seq // rows
    bias = _attn_bias_table(rel_bias)
    nk = ATTN_K_BLOCKS * tq

    cur = pl.BlockSpec((rows, inner), lambda j, b: (b * n_s + j, 0))
    prev = pl.BlockSpec((rows, inner), lambda j, b: (b * n_s + jnp.maximum(j - 1, 0), 0))

    def bias_spec(tile):
        return pl.BlockSpec(
            (1, heads, nk, tq),
            lambda j, b: (jnp.minimum(j * ATTN_TILES_PER_STEP + tile, ATTN_K_BLOCKS - 1), 0, 0, 0),
            pipeline_mode=pl.Buffered(1))

    nbytes = (2 * 6 * rows * inner * 2 + ATTN_TILES_PER_STEP * heads * tq * nk * 4
              + ATTN_SLOTS * tq * nk * (4 + 2) + 4 * tq * nk * 4)
    return pl.pallas_call(
        functools.partial(_attn_kernel, heads=heads),
        grid=(n_s, batch),
        in_specs=[cur, prev, cur, prev, cur, bias_spec(0), bias_spec(1)],
        out_specs=cur,
        out_shape=jax.ShapeDtypeStruct((t, inner), BF16),
        scratch_shapes=[pltpu.VMEM((ATTN_SLOTS, nk, tq), F32), pltpu.VMEM((ATTN_SLOTS, nk, tq), BF16)],
        compiler_params=pltpu.CompilerParams(
            dimension_semantics=("arbitrary", "arbitrary"), vmem_limit_bytes=_vmem_limit(nbytes)),
        name="band_attn",
    )(q, k, k, v, v, bias, bias)


def _merge_kernel(x_ref, ys_ref, ya_ref, g_ref, ws_ref, wa_ref, wo_ref, o_ref):
    d = x_ref.shape[1]
    bs = jnp.dot(ys_ref[...], ws_ref[...], preferred_element_type=F32)
    ba = jnp.dot(ya_ref[...], wa_ref[...], preferred_element_type=F32)
    g = g_ref[...].astype(F32)
    merged = jax.nn.sigmoid(g[:, :d]) * bs + jax.nn.sigmoid(g[:, d:]) * ba
    o_ref[...] = x_ref[...] + jnp.dot(merged.astype(BF16), wo_ref[...], preferred_element_type=F32)


def _merge(x2d, y_ssm, y_attn, g, w_bs, w_ba, w_out):
    t, d = x2d.shape
    tm = ROW_TILE
    row = lambda w: pl.BlockSpec((tm, w), lambda i: (i, 0))
    nbytes = (w_bs.size + w_ba.size + w_out.size) * 2 + 2 * tm * (2 * d * 4 + 4 * d * 2) + 6 * tm * d * 4
    return pl.pallas_call(
        _merge_kernel,
        grid=(t // tm,),
        in_specs=[row(d), row(y_ssm.shape[1]), row(y_attn.shape[1]), row(2 * d),
                  _const_spec(w_bs.shape), _const_spec(w_ba.shape), _const_spec(w_out.shape)],
        out_specs=row(d),
        out_shape=jax.ShapeDtypeStruct((t, d), F32),
        compiler_params=pltpu.CompilerParams(
            dimension_semantics=("arbitrary",), vmem_limit_bytes=_vmem_limit(nbytes)),
        name="merge",
    )(x2d, y_ssm, y_attn, g, w_bs.astype(BF16), w_ba.astype(BF16), w_out.astype(BF16))


def kernel(x, ffn1_norm_w, ffn1_w_gu, ffn1_w_down, mix_norm_w, w_in, conv_w, conv_b, dt_bias, A_log,
           D_skip, ssm_norm_w, rel_bias, w_branch_ssm, w_branch_attn, w_out, ffn2_norm_w, ffn2_w_gu,
           ffn2_w_down, final_norm_w):
    batch, seq, d = x.shape
    depth = ffn1_w_gu.shape[0]
    ssm_heads = A_log.shape[1]
    ssm_inner = ssm_heads * SSM_HEAD_DIM
    conv_dim = conv_w.shape[2]
    attn_inner = rel_bias.shape[1] * ATTN_HEAD_DIM
    xf = x.reshape(batch * seq, d)
    for l in range(depth):
        last = l == depth - 1
        xf = _ffn(xf, ffn1_norm_w[l], ffn1_w_gu[l], ffn1_w_down[l], final_norm_w, final_norm=False)
        z, xbc, dt_raw, q, k, v, g = _in_proj(
            xf, mix_norm_w[l], w_in[l], ssm_inner=ssm_inner, conv_dim=conv_dim,
            ssm_heads=ssm_heads, attn_inner=attn_inner)
        y_ssm = _ssd(xbc, z, dt_raw, conv_w[l], conv_b[l], dt_bias[l], A_log[l], D_skip[l],
                     ssm_norm_w[l], batch=batch, seq=seq)
        y_attn = _attention(q, k, v, rel_bias[l], batch=batch, seq=seq)
        xf = _merge(xf, y_ssm, y_attn, g, w_branch_ssm[l], w_branch_attn[l], w_out[l])
        xf = _ffn(xf, ffn2_norm_w[l], ffn2_w_gu[l], ffn2_w_down[l], final_norm_w, final_norm=last)
    if depth == 0:
        raise ValueError("depth must be >= 1")
    return xf.reshape(batch, seq, d)
```

```python
import functools

import jax
import jax.numpy as jnp
from jax import lax
from jax.experimental import pallas as pl
from jax.experimental.pallas import tpu as pltpu

F32 = jnp.float32
BF16 = jnp.bfloat16

CHUNK = 64
SSM_HEAD_DIM = 64
SSM_GROUPS = 2
SSM_STATE = 128
CONV_WIDTH = 4
ATTN_HEAD_DIM = 64
LEFT_CHUNKS = 8
MAX_REL_DIST = 128
FFN_RES_SCALE = 0.5
RMS_EPS = 1e-6
LOG2_E = 1.4426950408889634

LANES = 128
SUBLANES = 8
BF16_ROWS = 16
MXU_DIM = 256
VMEM_BYTES = 64 * 1024 * 1024

ROW_TILE = 512
FFN_ROW_TILE = 1024
FFN_COL_TILE = 512
SSD_ROW_TILE = 1024
CONV_ROW_BLOCK = 128
ATTN_Q_CHUNKS = 4
ATTN_Q_TILE = ATTN_Q_CHUNKS * CHUNK
ATTN_TILES_PER_STEP = 2
ATTN_SLOTS = 2
ATTN_K_BLOCKS = (LEFT_CHUNKS + ATTN_Q_CHUNKS) * CHUNK // ATTN_Q_TILE


def _vmem_limit(nbytes):
    return int(min(nbytes * 1.25 + (8 << 20), VMEM_BYTES - (6 << 20)))


def _const_spec(shape):
    zeros = (0,) * len(shape)
    return pl.BlockSpec(shape, lambda *_: zeros, pipeline_mode=pl.Buffered(1))


def _rmsnorm(x, w):
    ms = jnp.mean(x * x, axis=-1, keepdims=True)
    return x * lax.rsqrt(ms + RMS_EPS) * w


SPLIT_PIECES = 3


def _split3(x):
    hi = x.astype(BF16)
    r1 = x - hi.astype(F32)
    mid = r1.astype(BF16)
    lo = (r1 - mid.astype(F32)).astype(BF16)
    return hi, mid, lo


def _ffn_kernel(x_ref, nw_ref, wgu_ref, wd_ref, fnw_ref, o_ref, h_ref, a_ref, *, d_ff, final_norm):
    half = x_ref.shape[0] // 2
    halves = (slice(0, half), slice(half, 2 * half))
    col_tiles = [(c, min(FFN_COL_TILE, d_ff - c)) for c in range(0, d_ff, FFN_COL_TILE)]

    def prologue(rows):
        h_ref[rows, :] = _rmsnorm(x_ref[rows, :], nw_ref[...]).astype(BF16)

    def gate_up(rows, c, w):
        h = h_ref[rows, :]
        g = jnp.dot(h, wgu_ref[:, c:c + w], preferred_element_type=F32)
        u = jnp.dot(h, wgu_ref[:, d_ff + c:d_ff + c + w], preferred_element_type=F32)
        a_ref[rows, c:c + w] = (g * jax.nn.sigmoid(g) * u).astype(BF16)

    def down(rows):
        return jnp.dot(a_ref[rows, :], wd_ref[...], preferred_element_type=F32)

    def epilogue(rows, acc):
        y = x_ref[rows, :] + FFN_RES_SCALE * acc
        if final_norm:
            y = _rmsnorm(y, fnw_ref[...])
        o_ref[rows, :] = y

    prologue(halves[0])
    for i, (c, w) in enumerate(col_tiles):
        gate_up(halves[0], c, w)
        if i == 0:
            prologue(halves[1])
    acc0 = down(halves[0])
    for i, (c, w) in enumerate(col_tiles):
        gate_up(halves[1], c, w)
        if i == 0:
            epilogue(halves[0], acc0)
    epilogue(halves[1], down(halves[1]))


def _ffn(x2d, norm_w, w_gu, w_down, final_norm_w, *, final_norm):
    t, d = x2d.shape
    d_ff = w_down.shape[0]
    assert t % FFN_ROW_TILE == 0 and d_ff % LANES == 0
    tm = FFN_ROW_TILE
    nbytes = (3 * d * d_ff * 2 + 4 * tm * d * 4 + tm * d * 2 + tm * d_ff * 2
              + 2 * tm * FFN_COL_TILE * 4 + 2 * tm * d * 4)
    return pl.pallas_call(
        functools.partial(_ffn_kernel, d_ff=d_ff, final_norm=final_norm),
        grid=(t // tm,),
        in_specs=[
            pl.BlockSpec((tm, d), lambda i: (i, 0)),
            _const_spec((1, d)),
            _const_spec((d, 2 * d_ff)),
            _const_spec((d_ff, d)),
            _const_spec((1, d)),
        ],
        out_specs=pl.BlockSpec((tm, d), lambda i: (i, 0)),
        out_shape=jax.ShapeDtypeStruct((t, d), F32),
        scratch_shapes=[pltpu.VMEM((tm, d), BF16), pltpu.VMEM((tm, d_ff), BF16)],
        compiler_params=pltpu.CompilerParams(
            dimension_semantics=("arbitrary",), vmem_limit_bytes=_vmem_limit(nbytes)),
        name="ffn_final" if final_norm else "ffn",
    )(x2d, norm_w.reshape(1, d), w_gu.astype(BF16), w_down.astype(BF16), final_norm_w.reshape(1, d))


def _inproj_kernel(x_ref, nw_ref, w_ref, z_ref, xbc_ref, q_ref, k_ref, v_ref, g_ref, dt_ref, h_ref,
                   *, q_scale):
    h_ref[...] = _rmsnorm(x_ref[...], nw_ref[...]).astype(BF16)
    col = 0
    for o_ref in (z_ref, xbc_ref, q_ref, k_ref, v_ref, g_ref, dt_ref):
        n = o_ref.shape[1]
        step = min(n, 2 * MXU_DIM)
        for c in range(0, n, step):
            r = jnp.dot(h_ref[...], w_ref[:, col + c:col + c + step], preferred_element_type=F32)
            if o_ref is q_ref:
                r = r * q_scale
            o_ref[:, c:c + step] = r.astype(o_ref.dtype)
        col += n


def _in_proj(x2d, norm_w, w_in, *, ssm_inner, conv_dim, ssm_heads, attn_inner):
    t, d = x2d.shape
    tm = ROW_TILE
    dt_lo = ssm_inner + conv_dim
    dt_hi = dt_lo + ssm_heads
    assert dt_lo % LANES == 0
    w_dt = w_in[:, dt_lo:dt_hi]
    w = jnp.concatenate([w_in[:, :dt_lo], w_in[:, dt_hi:]] + [w_dt] * SPLIT_PIECES, axis=1).astype(BF16)
    widths = [ssm_inner, conv_dim, attn_inner, attn_inner, attn_inner, w_in.shape[1] - dt_hi - 3 * attn_inner,
              SPLIT_PIECES * ssm_heads]
    out_dtypes = [BF16, BF16, BF16, BF16, BF16, BF16, F32]
    nbytes = (sum(widths) * d * 2 + 2 * tm * d * 4 + tm * d * 2
              + 2 * sum(tm * wd * jnp.dtype(dt).itemsize for wd, dt in zip(widths, out_dtypes))
              + 2 * tm * 2 * MXU_DIM * 4)
    z, xbc, q, k, v, g, dt_raw = pl.pallas_call(
        functools.partial(_inproj_kernel, q_scale=ATTN_HEAD_DIM ** -0.5 * LOG2_E),
        grid=(t // tm,),
        in_specs=[pl.BlockSpec((tm, d), lambda i: (i, 0)), _const_spec((1, d)),
                  _const_spec((d, sum(widths)))],
        out_specs=[pl.BlockSpec((tm, wd), lambda i: (i, 0)) for wd in widths],
        out_shape=[jax.ShapeDtypeStruct((t, wd), dt) for wd, dt in zip(widths, out_dtypes)],
        scratch_shapes=[pltpu.VMEM((tm, d), BF16)],
        compiler_params=pltpu.CompilerParams(
            dimension_semantics=("arbitrary",), vmem_limit_bytes=_vmem_limit(nbytes)),
        name="in_proj",
    )(x2d, norm_w.reshape(1, d), w)
    return z, xbc, dt_raw, q, k, v, g


def _ssd_kernel(xbc_ref, z_ref, dtr_ref, convw_ref, convb_ref, dtb_ref, alog_ref, dskip_ref,
                normw_ref, expand_ref, o_ref,
                xpad_ref, xs_ref, bc_ref, dt_ref, state_ref,
                *, n_chunks, ssm_inner, heads):
    s_idx = pl.program_id(1)
    tm = xs_ref.shape[0]
    gn = SSM_GROUPS * SSM_STATE
    gw = ssm_inner // SSM_GROUPS

    halo = xpad_ref.shape[0] - tm

    @pl.when(s_idx == 0)
    def _():
        state_ref[...] = jnp.zeros_like(state_ref)
        xpad_ref[0:halo, :] = jnp.zeros((halo, xpad_ref.shape[1]), BF16)

    @pl.when(s_idx > 0)
    def _():
        xpad_ref[0:halo, :] = xpad_ref[tm:tm + halo, :]

    xpad_ref[halo:, :] = xbc_ref[...]

    blk = CONV_ROW_BLOCK
    srow = lax.broadcasted_iota(jnp.int32, ((CONV_WIDTH - 1) * blk, blk + halo), 0)
    scol = lax.broadcasted_iota(jnp.int32, ((CONV_WIDTH - 1) * blk, blk + halo), 1)
    shift_stack = (scol == (srow % blk) + halo - (srow // blk + 1)).astype(BF16)
    def shifted_rows(r0):
        return jnp.dot(shift_stack, xpad_ref[r0:r0 + blk + halo, :], preferred_element_type=F32)

    shifted_next = shifted_rows(0)
    for r0 in range(0, tm, blk):
        shifted = shifted_next
        if r0 + blk < tm:
            shifted_next = shifted_rows(r0 + blk)
        xblk = xpad_ref[r0 + halo:r0 + halo + blk, :]
        conv = xblk.astype(F32) * convw_ref[CONV_WIDTH - 1:CONV_WIDTH, :] + convb_ref[...]
        for k in range(1, CONV_WIDTH):
            conv = conv + shifted[(k - 1) * blk:k * blk, :] * convw_ref[CONV_WIDTH - 1 - k:CONV_WIDTH - k, :]
        conv = conv * jax.nn.sigmoid(conv)
        xs_ref[r0:r0 + blk, :] = conv[:, :ssm_inner]
        bc_ref[r0:r0 + blk, :] = conv[:, ssm_inner:]
    dt_ref[...] = jax.nn.softplus(dtr_ref[...] + dtb_ref[...])

    a_head = -jnp.exp(alog_ref[...]) * LOG2_E
    expand = expand_ref[...]
    lane_group = lax.broadcasted_iota(jnp.int32, (CHUNK, SPLIT_PIECES * heads), 1) // heads

    def pieces(x):
        parts = _split3(x)
        out = parts[SPLIT_PIECES - 1]
        for i in range(SPLIT_PIECES - 2, -1, -1):
            out = jnp.where(lane_group == i, parts[i], out)
        return out

    row = lax.broadcasted_iota(jnp.int32, (CHUNK, SPLIT_PIECES * CHUNK), 0)
    col = lax.broadcasted_iota(jnp.int32, (CHUNK, SPLIT_PIECES * CHUNK), 1)
    tril = ((col % CHUNK) <= row).astype(BF16)
    prow = lax.broadcasted_iota(jnp.int32, (CHUNK, LANES), 0)
    plane = lax.broadcasted_iota(jnp.int32, (CHUNK, LANES), 1)
    causal_pair = (plane % SSM_HEAD_DIM) <= prow
    first_head = plane < SSM_HEAD_DIM
    drow = lax.broadcasted_iota(jnp.int32, (CHUNK, ssm_inner), 0)
    dlane = lax.broadcasted_iota(jnp.int32, (CHUNK, ssm_inner), 1)
    diag = (dlane % SSM_HEAD_DIM) == drow

    def decay_stage(c):
        rows = pl.ds(c * CHUNK, CHUNK)
        dt = dt_ref[rows, :]
        bc = bc_ref[rows, :].astype(BF16)
        dt_e = jnp.dot(pieces(dt), expand, preferred_element_type=F32)
        a_parts = jnp.concatenate(_split3(dt * a_head), axis=0)
        a_cum_h = jnp.dot(tril, a_parts, preferred_element_type=F32)
        a_cum = jnp.dot(pieces(a_cum_h), expand, preferred_element_type=F32)
        cbs = []
        for g in range(SSM_GROUPS):
            b_g = bc[:, g * SSM_STATE:(g + 1) * SSM_STATE]
            c_g = bc[:, gn + g * SSM_STATE:gn + (g + 1) * SSM_STATE]
            cbs.append(lax.dot_general(c_g, b_g, (((1,), (1,)), ((), ())),
                                       preferred_element_type=F32))
        return dt_e, a_cum, cbs

    def operand_stage(c, dt_e, a_cum, cbs):
        rows = pl.ds(c * CHUNK, CHUNK)
        xs = xs_ref[rows, :]
        a_last = a_cum[CHUNK - 1:CHUNK, :]
        xdt = xs * dt_e
        xdec = (xdt * jnp.exp2(a_last - a_cum)).astype(BF16)
        xdt_b = xdt.astype(BF16)
        a_t = jnp.sum(jnp.where(diag, a_cum, 0.0), axis=0, keepdims=True)
        ms, rhss = [], []
        for g in range(SSM_GROUPS):
            cb2 = jnp.concatenate([cbs[g], cbs[g]], axis=1)
            for j in range(gw // LANES):
                c0 = g * gw + j * LANES
                seg = a_cum[:, c0:c0 + LANES] - a_t[:, c0:c0 + LANES]
                ldec = jnp.exp2(jnp.where(causal_pair, seg, -jnp.inf))
                ms.append((cb2 * ldec).astype(BF16))
                xp = xdt_b[:, c0:c0 + LANES]
                zero = jnp.zeros_like(xp)
                rhss.append(jnp.concatenate([jnp.where(first_head, xp, zero),
                                             jnp.where(first_head, zero, xp)], axis=0))
        return xdec, ms, rhss

    def matmul_stage(c, a_cum, xdec, ms, rhss):
        rows = pl.ds(c * CHUNK, CHUNK)
        bc = bc_ref[rows, :]
        a_last = a_cum[CHUNK - 1:CHUNK, :]
        y_offs, y_diags = [], []
        for g in range(SSM_GROUPS):
            g0 = g * gw
            b_g = bc[:, g * SSM_STATE:(g + 1) * SSM_STATE]
            c_g = bc[:, gn + g * SSM_STATE:gn + (g + 1) * SSM_STATE].astype(BF16)
            prev = state_ref[:, g0:g0 + gw]
            y_offs.append(jnp.dot(c_g, prev.astype(BF16), preferred_element_type=F32))
            new = jnp.dot(b_g.T.astype(BF16), xdec[:, g0:g0 + gw], preferred_element_type=F32)
            state_ref[:, g0:g0 + gw] = prev * jnp.exp2(a_last[:, g0:g0 + gw]) + new
        for m, rhs in zip(ms, rhss):
            y_diags.append(jnp.dot(m, rhs, preferred_element_type=F32))
        return jnp.concatenate(y_offs, axis=1), jnp.concatenate(y_diags, axis=1)

    def output_stage(c, a_cum, y_off, y_diag):
        rows = pl.ds(c * CHUNK, CHUNK)
        y = y_diag + y_off * jnp.exp2(a_cum) + dskip_ref[...] * xs_ref[rows, :]
        zc = z_ref[rows, :].astype(F32)
        y = y * (zc * jax.nn.sigmoid(zc))
        outs = []
        for g in range(SSM_GROUPS):
            yg = y[:, g * gw:(g + 1) * gw]
            outs.append(yg * lax.rsqrt(jnp.mean(yg * yg, axis=-1, keepdims=True) + RMS_EPS))
        o_ref[rows, :] = (jnp.concatenate(outs, axis=1) * normw_ref[...]).astype(o_ref.dtype)

    decays = {c: decay_stage(c) for c in range(min(2, n_chunks))}
    operands = {0: operand_stage(0, *decays[0])}
    for c in range(n_chunks):
        a_cum = decays[c][1]
        y_off, y_diag = matmul_stage(c, a_cum, *operands.pop(c))
        if c + 1 < n_chunks:
            operands[c + 1] = operand_stage(c + 1, *decays[c + 1])
        if c + 2 < n_chunks:
            decays[c + 2] = decay_stage(c + 2)
        output_stage(c, a_cum, y_off, y_diag)
        del decays[c]


def _ssd(xbc, z, dt_raw, conv_w, conv_b, dt_bias, a_log, d_skip, norm_w, *, batch, seq):
    t, conv_dim = xbc.shape
    inner = z.shape[1]
    heads = a_log.shape[0]
    hrep = dt_raw.shape[1]
    tm = SSD_ROW_TILE
    assert seq % tm == 0 and inner == heads * SSM_HEAD_DIM and hrep == SPLIT_PIECES * heads
    n_s = seq // tm
    rep = lambda p: jnp.repeat(p.astype(F32), SSM_HEAD_DIM).reshape(1, inner)
    rep_h = lambda p: jnp.tile(p.astype(F32), SPLIT_PIECES).reshape(1, hrep)
    expand = jnp.tile(jnp.repeat(jnp.eye(heads, dtype=BF16), SSM_HEAD_DIM, axis=1),
                      (SPLIT_PIECES, 1))
    row_map = lambda b, s: (b * n_s + s, 0)
    nbytes = (2 * tm * (conv_dim + 2 * inner) * 2 + 2 * tm * LANES * 4
              + (tm + SUBLANES) * conv_dim * 4 + tm * conv_dim * 4 + tm * LANES * 4
              + SSM_STATE * inner * 4 + 3 * tm * conv_dim * 4 + 24 * CHUNK * inner * 4)
    return pl.pallas_call(
        functools.partial(_ssd_kernel, n_chunks=tm // CHUNK, ssm_inner=inner, heads=heads),
        grid=(batch, n_s),
        in_specs=[
            pl.BlockSpec((tm, conv_dim), row_map),
            pl.BlockSpec((tm, inner), row_map),
            pl.BlockSpec((tm, hrep), row_map),
            _const_spec((CONV_WIDTH, conv_dim)),
            _const_spec((1, conv_dim)),
            _const_spec((1, hrep)),
            _const_spec((1, hrep)),
            _const_spec((1, inner)),
            _const_spec((1, inner)),
            _const_spec((hrep, inner)),
        ],
        out_specs=pl.BlockSpec((tm, inner), row_map),
        out_shape=jax.ShapeDtypeStruct((t, inner), BF16),
        scratch_shapes=[
            pltpu.VMEM((tm + BF16_ROWS, conv_dim), BF16),
            pltpu.VMEM((tm, inner), F32),
            pltpu.VMEM((tm, conv_dim - inner), F32),
            pltpu.VMEM((tm, hrep), F32),
            pltpu.VMEM((SSM_STATE, inner), F32),
        ],
        compiler_params=pltpu.CompilerParams(
            dimension_semantics=("arbitrary", "arbitrary"), vmem_limit_bytes=_vmem_limit(nbytes)),
        name="ssd",
    )(xbc, z, dt_raw, conv_w.astype(F32), conv_b.reshape(1, conv_dim).astype(F32),
      rep_h(dt_bias), rep_h(a_log), rep(d_skip),
      norm_w.reshape(1, inner).astype(F32), expand)


def _attn_key_chunks(qv):
    q_per_group = LANES // CHUNK
    return range(qv * q_per_group, qv * q_per_group + LEFT_CHUNKS + q_per_group)


def _attn_kernel(q_ref, kprev_ref, kcur_ref, vprev_ref, vcur_ref, bias0_ref, bias1_ref, o_ref,
                 s_ref, p_ref, *, heads):
    tq = s_ref.shape[2]
    n_slots = s_ref.shape[0]
    n_kc = s_ref.shape[1] // CHUNK
    n_qv = tq // LANES
    pair = LANES // ATTN_HEAD_DIM
    step_rows = q_ref.shape[0]
    bias_refs = (bias0_ref, bias1_ref)
    lane_head = lax.broadcasted_iota(jnp.int32, (tq, LANES), 1) // ATTN_HEAD_DIM

    def window(prev_ref, cur_ref, tile, lo):
        first = tile * tq
        last = first + n_kc * CHUNK - step_rows
        return jnp.concatenate([prev_ref[first:step_rows, lo:lo + LANES],
                                cur_ref[0:last, lo:lo + LANES]], axis=0)
    for slot in range(n_slots):
        for qv in range(n_qv):
            for kc in range(n_kc):
                if kc not in _attn_key_chunks(qv):
                    p_ref[slot, kc * CHUNK:(kc + 1) * CHUNK, qv * LANES:(qv + 1) * LANES] = jnp.zeros(
                        (CHUNK, LANES), BF16)

    items = [(tile, h) for tile in range(step_rows // tq) for h in range(heads)]

    def scores(u):
        tile, h = items[u]
        lo = (h // pair) * LANES
        qp = q_ref[tile * tq:(tile + 1) * tq, lo:lo + LANES]
        kp = window(kprev_ref, kcur_ref, tile, lo)
        qm = jnp.where(lane_head == h % pair, qp, jnp.zeros_like(qp))
        s_ref[u % n_slots] = lax.dot_general(kp, qm, (((1,), (1,)), ((), ())),
                                             preferred_element_type=F32)

    def softmax(u, qv):
        tile, h = items[u]
        bias_ref = bias_refs[tile]
        slot = u % n_slots
        cols = slice(qv * LANES, (qv + 1) * LANES)
        mx = None
        for kc in _attn_key_chunks(qv):
            rows = slice(kc * CHUNK, (kc + 1) * CHUNK)
            t = s_ref[slot, rows, cols] + bias_ref[0, h, rows, cols]
            s_ref[slot, rows, cols] = t
            mx = t if mx is None else jnp.maximum(mx, t)
        m = jnp.max(mx, axis=0, keepdims=True)
        acc = None
        for kc in _attn_key_chunks(qv):
            rows = slice(kc * CHUNK, (kc + 1) * CHUNK)
            p = jnp.exp2(s_ref[slot, rows, cols] - m)
            acc = p if acc is None else acc + p
            p_ref[slot, rows, cols] = p.astype(BF16)
        return 1.0 / jnp.sum(acc, axis=0, keepdims=True)

    def weighted_values(u, inv):
        tile, h = items[u]
        lo = (h // pair) * LANES
        d0 = (h % pair) * ATTN_HEAD_DIM
        vpt = window(vprev_ref, vcur_ref, tile, lo).T
        o_t = jnp.dot(vpt[d0:d0 + ATTN_HEAD_DIM, :], p_ref[u % n_slots],
                      preferred_element_type=F32)
        return o_t * inv

    outs = []

    def finish(u, inv):
        tile, h = items[u]
        outs.append(weighted_values(u, inv))
        if len(outs) == pair:
            lo = (h // pair) * LANES
            o_ref[tile * tq:(tile + 1) * tq, lo:lo + LANES] = jnp.concatenate(
                outs, axis=0).T.astype(o_ref.dtype)
            outs.clear()

    scores(0)
    inv = None
    for u in range(len(items)):
        if u + 1 < len(items):
            scores(u + 1)
        if u > 0:
            finish(u - 1, inv)
        inv = jnp.concatenate([softmax(u, qv) for qv in range(n_qv)], axis=1)
    finish(len(items) - 1, inv)


def _attn_bias_kernel(f_ref, o_ref):
    n_var, _, nk, tq = o_ref.shape
    width = f_ref.shape[2]
    x = jnp.broadcast_to(f_ref[0], (nk, width))
    rolled = pltpu.roll(x, 0, 1, stride=1, stride_axis=0)
    t = rolled[:, nk:] * LOG2_E
    key = lax.broadcasted_iota(jnp.int32, (nk, tq), 0)
    kc = key // CHUNK
    qc = lax.broadcasted_iota(jnp.int32, (nk, tq), 1) // CHUNK
    band = (kc >= qc) & (kc <= qc + LEFT_CHUNKS)
    for v in range(n_var):
        ok = band & (key >= (n_var - 1 - v) * tq)
        o_ref[v, 0] = jnp.where(ok, t, -jnp.inf)


def _attn_bias_table(rel_bias):
    heads = rel_bias.shape[0]
    nk = ATTN_K_BLOCKS * ATTN_Q_TILE
    width = nk + ATTN_Q_TILE
    dist = jnp.arange(width) - nk + LEFT_CHUNKS * CHUNK
    f = rel_bias.astype(F32)[:, jnp.clip(dist, -MAX_REL_DIST, MAX_REL_DIST) + MAX_REL_DIST]
    return pl.pallas_call(
        _attn_bias_kernel,
        grid=(heads,),
        in_specs=[pl.BlockSpec((1, 1, width), lambda h: (h, 0, 0))],
        out_specs=pl.BlockSpec((ATTN_K_BLOCKS, 1, nk, ATTN_Q_TILE), lambda h: (0, h, 0, 0)),
        out_shape=jax.ShapeDtypeStruct((ATTN_K_BLOCKS, heads, nk, ATTN_Q_TILE), F32),
        compiler_params=pltpu.CompilerParams(dimension_semantics=("arbitrary",)),
        name="attn_bias",
    )(f.reshape(heads, 1, width))


def _attention(q, k, v, rel_bias, *, batch, seq):
    t, inner = q.shape
    heads = rel_bias.shape[0]
    tq = ATTN_Q_TILE
    rows = ATTN_TILES_PER_STEP * tq
    assert ATTN_TILES_PER_STEP == 2 and ATTN_K_BLOCKS == ATTN_TILES_PER_STEP + 1
    assert seq % rows == 0 and inner == heads * ATTN_HEAD_DIM
    n_s = seq // rows
    bias = _attn_bias_table(rel_bias)
    nk = ATTN_K_BLOCKS * tq

    cur = pl.BlockSpec((rows, inner), lambda j, b: (b * n_s + j, 0))
    prev = pl.BlockSpec((rows, inner), lambda j, b: (b * n_s + jnp.maximum(j - 1, 0), 0))

    def bias_spec(tile):
        return pl.BlockSpec(
            (1, heads, nk, tq),
            lambda j, b: (jnp.minimum(j * ATTN_TILES_PER_STEP + tile, ATTN_K_BLOCKS - 1), 0, 0, 0),
            pipeline_mode=pl.Buffered(1))

    nbytes = (2 * 6 * rows * inner * 2 + ATTN_TILES_PER_STEP * heads * tq * nk * 4
              + ATTN_SLOTS * tq * nk * (4 + 2) + 4 * tq * nk * 4)
    return pl.pallas_call(
        functools.partial(_attn_kernel, heads=heads),
        grid=(n_s, batch),
        in_specs=[cur, prev, cur, prev, cur, bias_spec(0), bias_spec(1)],
        out_specs=cur,
        out_shape=jax.ShapeDtypeStruct((t, inner), BF16),
        scratch_shapes=[pltpu.VMEM((ATTN_SLOTS, nk, tq), F32), pltpu.VMEM((ATTN_SLOTS, nk, tq), BF16)],
        compiler_params=pltpu.CompilerParams(
            dimension_semantics=("arbitrary", "arbitrary"), vmem_limit_bytes=_vmem_limit(nbytes)),
        name="band_attn",
    )(q, k, k, v, v, bias, bias)


def _merge_kernel(x_ref, ys_ref, ya_ref, g_ref, ws_ref, wa_ref, wo_ref, o_ref):
    d = x_ref.shape[1]
    half = x_ref.shape[0] // 2
    halves = (slice(0, half), slice(half, 2 * half))

    def branches(rows):
        return (jnp.dot(ys_ref[rows, :], ws_ref[...], preferred_element_type=F32),
                jnp.dot(ya_ref[rows, :], wa_ref[...], preferred_element_type=F32))

    def gate(rows, bs, ba):
        g = g_ref[rows, :].astype(F32)
        return (jax.nn.sigmoid(g[:, :d]) * bs + jax.nn.sigmoid(g[:, d:]) * ba).astype(BF16)

    def project(rows, merged):
        o_ref[rows, :] = x_ref[rows, :] + jnp.dot(merged, wo_ref[...], preferred_element_type=F32)

    b0 = branches(halves[0])
    b1 = branches(halves[1])
    m0 = gate(halves[0], *b0)
    project(halves[0], m0)
    m1 = gate(halves[1], *b1)
    project(halves[1], m1)


def _merge(x2d, y_ssm, y_attn, g, w_bs, w_ba, w_out):
    t, d = x2d.shape
    tm = FFN_ROW_TILE
    assert t % tm == 0
    row = lambda w: pl.BlockSpec((tm, w), lambda i: (i, 0))
    nbytes = (w_bs.size + w_ba.size + w_out.size) * 2 + 2 * tm * (2 * d * 4 + 4 * d * 2) + 6 * tm * d * 4
    return pl.pallas_call(
        _merge_kernel,
        grid=(t // tm,),
        in_specs=[row(d), row(y_ssm.shape[1]), row(y_attn.shape[1]), row(2 * d),
                  _const_spec(w_bs.shape), _const_spec(w_ba.shape), _const_spec(w_out.shape)],
        out_specs=row(d),
        out_shape=jax.ShapeDtypeStruct((t, d), F32),
        compiler_params=pltpu.CompilerParams(
            dimension_semantics=("arbitrary",), vmem_limit_bytes=_vmem_limit(nbytes)),
        name="merge",
    )(x2d, y_ssm, y_attn, g, w_bs.astype(BF16), w_ba.astype(BF16), w_out.astype(BF16))


def kernel(x, ffn1_norm_w, ffn1_w_gu, ffn1_w_down, mix_norm_w, w_in, conv_w, conv_b, dt_bias, A_log,
           D_skip, ssm_norm_w, rel_bias, w_branch_ssm, w_branch_attn, w_out, ffn2_norm_w, ffn2_w_gu,
           ffn2_w_down, final_norm_w):
    batch, seq, d = x.shape
    depth = ffn1_w_gu.shape[0]
    ssm_heads = A_log.shape[1]
    ssm_inner = ssm_heads * SSM_HEAD_DIM
    conv_dim = conv_w.shape[2]
    attn_inner = rel_bias.shape[1] * ATTN_HEAD_DIM
    xf = x.reshape(batch * seq, d)
    for l in range(depth):
        last = l == depth - 1
        xf = _ffn(xf, ffn1_norm_w[l], ffn1_w_gu[l], ffn1_w_down[l], final_norm_w, final_norm=False)
        z, xbc, dt_raw, q, k, v, g = _in_proj(
            xf, mix_norm_w[l], w_in[l], ssm_inner=ssm_inner, conv_dim=conv_dim,
            ssm_heads=ssm_heads, attn_inner=attn_inner)
        y_ssm = _ssd(xbc, z, dt_raw, conv_w[l], conv_b[l], dt_bias[l], A_log[l], D_skip[l],
                     ssm_norm_w[l], batch=batch, seq=seq)
        y_attn = _attention(q, k, v, rel_bias[l], batch=batch, seq=seq)
        xf = _merge(xf, y_ssm, y_attn, g, w_branch_ssm[l], w_branch_attn[l], w_out[l])
        xf = _ffn(xf, ffn2_norm_w[l], ffn2_w_gu[l], ffn2_w_down[l], final_norm_w, final_norm=last)
    if depth == 0:
        raise ValueError("depth must be >= 1")
    return xf.reshape(batch, seq, d)
```

```python
import functools

import jax
import jax.numpy as jnp
from jax import lax
from jax.experimental import pallas as pl
from jax.experimental.pallas import tpu as pltpu

F32 = jnp.float32
BF16 = jnp.bfloat16

CHUNK = 64
SSM_HEAD_DIM = 64
SSM_GROUPS = 2
SSM_STATE = 128
CONV_WIDTH = 4
ATTN_HEAD_DIM = 64
LEFT_CHUNKS = 8
MAX_REL_DIST = 128
FFN_RES_SCALE = 0.5
RMS_EPS = 1e-6
LOG2_E = 1.4426950408889634

LANES = 128
SUBLANES = 8
BF16_ROWS = 16
MXU_DIM = 256
VMEM_BYTES = 64 * 1024 * 1024

ROW_TILE = 512
FFN_ROW_TILE = 1024
FFN_COL_TILE = 512
SSD_ROW_TILE = 1024
CONV_ROW_BLOCK = 128
ATTN_Q_CHUNKS = 4
ATTN_Q_TILE = ATTN_Q_CHUNKS * CHUNK
ATTN_TILES_PER_STEP = 2
ATTN_SLOTS = 2
ATTN_K_BLOCKS = (LEFT_CHUNKS + ATTN_Q_CHUNKS) * CHUNK // ATTN_Q_TILE


def _vmem_limit(nbytes):
    return int(min(nbytes * 1.25 + (8 << 20), VMEM_BYTES - (6 << 20)))


def _const_spec(shape):
    zeros = (0,) * len(shape)
    return pl.BlockSpec(shape, lambda *_: zeros, pipeline_mode=pl.Buffered(1))


def _rmsnorm(x, w):
    ms = jnp.mean(x * x, axis=-1, keepdims=True)
    return x * lax.rsqrt(ms + RMS_EPS) * w


def _silu(x):
    h = 0.5 * x
    return h + h * jnp.tanh(h)


SPLIT_PIECES = 3


def _split3(x):
    hi = x.astype(BF16)
    r1 = x - hi.astype(F32)
    mid = r1.astype(BF16)
    lo = (r1 - mid.astype(F32)).astype(BF16)
    return hi, mid, lo


def _ffn_kernel(x_ref, nw_ref, wgu_ref, wd_ref, fnw_ref, o_ref, h_ref, a_ref, *, d_ff, final_norm):
    half = x_ref.shape[0] // 2
    halves = (slice(0, half), slice(half, 2 * half))
    col_tiles = [(c, min(FFN_COL_TILE, d_ff - c)) for c in range(0, d_ff, FFN_COL_TILE)]

    def prologue(rows):
        h_ref[rows, :] = _rmsnorm(x_ref[rows, :], nw_ref[...]).astype(BF16)

    def gate_up(rows, c, w):
        h = h_ref[rows, :]
        g = jnp.dot(h, wgu_ref[:, c:c + w], preferred_element_type=F32)
        u = jnp.dot(h, wgu_ref[:, d_ff + c:d_ff + c + w], preferred_element_type=F32)
        a_ref[rows, c:c + w] = (_silu(g) * u).astype(BF16)

    def down(rows):
        return jnp.dot(a_ref[rows, :], wd_ref[...], preferred_element_type=F32)

    def epilogue(rows, acc):
        y = x_ref[rows, :] + FFN_RES_SCALE * acc
        if final_norm:
            y = _rmsnorm(y, fnw_ref[...])
        o_ref[rows, :] = y

    prologue(halves[0])
    for i, (c, w) in enumerate(col_tiles):
        gate_up(halves[0], c, w)
        if i == 0:
            prologue(halves[1])
    acc0 = down(halves[0])
    for i, (c, w) in enumerate(col_tiles):
        gate_up(halves[1], c, w)
        if i == 0:
            epilogue(halves[0], acc0)
    epilogue(halves[1], down(halves[1]))


def _ffn(x2d, norm_w, w_gu, w_down, final_norm_w, *, final_norm):
    t, d = x2d.shape
    d_ff = w_down.shape[0]
    assert t % FFN_ROW_TILE == 0 and d_ff % LANES == 0
    tm = FFN_ROW_TILE
    nbytes = (3 * d * d_ff * 2 + 4 * tm * d * 4 + tm * d * 2 + tm * d_ff * 2
              + 2 * tm * FFN_COL_TILE * 4 + 2 * tm * d * 4)
    return pl.pallas_call(
        functools.partial(_ffn_kernel, d_ff=d_ff, final_norm=final_norm),
        grid=(t // tm,),
        in_specs=[
            pl.BlockSpec((tm, d), lambda i: (i, 0)),
            _const_spec((1, d)),
            _const_spec((d, 2 * d_ff)),
            _const_spec((d_ff, d)),
            _const_spec((1, d)),
        ],
        out_specs=pl.BlockSpec((tm, d), lambda i: (i, 0)),
        out_shape=jax.ShapeDtypeStruct((t, d), F32),
        scratch_shapes=[pltpu.VMEM((tm, d), BF16), pltpu.VMEM((tm, d_ff), BF16)],
        compiler_params=pltpu.CompilerParams(
            dimension_semantics=("arbitrary",), vmem_limit_bytes=_vmem_limit(nbytes)),
        name="ffn_final" if final_norm else "ffn",
    )(x2d, norm_w.reshape(1, d), w_gu.astype(BF16), w_down.astype(BF16), final_norm_w.reshape(1, d))


def _inproj_kernel(x_ref, nw_ref, w_ref, z_ref, xbc_ref, q_ref, k_ref, v_ref, g_ref, dt_ref, h_ref,
                   *, q_scale):
    h_ref[...] = _rmsnorm(x_ref[...], nw_ref[...]).astype(BF16)
    col = 0
    for o_ref in (z_ref, xbc_ref, q_ref, k_ref, v_ref, g_ref, dt_ref):
        n = o_ref.shape[1]
        step = min(n, 2 * MXU_DIM)
        for c in range(0, n, step):
            r = jnp.dot(h_ref[...], w_ref[:, col + c:col + c + step], preferred_element_type=F32)
            if o_ref is q_ref:
                r = r * q_scale
            o_ref[:, c:c + step] = r.astype(o_ref.dtype)
        col += n


def _in_proj(x2d, norm_w, w_in, *, ssm_inner, conv_dim, ssm_heads, attn_inner):
    t, d = x2d.shape
    tm = ROW_TILE
    dt_lo = ssm_inner + conv_dim
    dt_hi = dt_lo + ssm_heads
    assert dt_lo % LANES == 0
    w_dt = w_in[:, dt_lo:dt_hi]
    w = jnp.concatenate([w_in[:, :dt_lo], w_in[:, dt_hi:]] + [w_dt] * SPLIT_PIECES, axis=1).astype(BF16)
    widths = [ssm_inner, conv_dim, attn_inner, attn_inner, attn_inner, w_in.shape[1] - dt_hi - 3 * attn_inner,
              SPLIT_PIECES * ssm_heads]
    out_dtypes = [BF16, BF16, BF16, BF16, BF16, BF16, F32]
    nbytes = (sum(widths) * d * 2 + 2 * tm * d * 4 + tm * d * 2
              + 2 * sum(tm * wd * jnp.dtype(dt).itemsize for wd, dt in zip(widths, out_dtypes))
              + 2 * tm * 2 * MXU_DIM * 4)
    z, xbc, q, k, v, g, dt_raw = pl.pallas_call(
        functools.partial(_inproj_kernel, q_scale=ATTN_HEAD_DIM ** -0.5 * LOG2_E),
        grid=(t // tm,),
        in_specs=[pl.BlockSpec((tm, d), lambda i: (i, 0)), _const_spec((1, d)),
                  _const_spec((d, sum(widths)))],
        out_specs=[pl.BlockSpec((tm, wd), lambda i: (i, 0)) for wd in widths],
        out_shape=[jax.ShapeDtypeStruct((t, wd), dt) for wd, dt in zip(widths, out_dtypes)],
        scratch_shapes=[pltpu.VMEM((tm, d), BF16)],
        compiler_params=pltpu.CompilerParams(
            dimension_semantics=("arbitrary",), vmem_limit_bytes=_vmem_limit(nbytes)),
        name="in_proj",
    )(x2d, norm_w.reshape(1, d), w)
    return z, xbc, dt_raw, q, k, v, g


def _ssd_kernel(xbc_ref, z_ref, dtr_ref, convw_ref, convb_ref, dtb_ref, alog_ref, dskip_ref,
                normw_ref, expand_ref, o_ref,
                xpad_ref, xs_ref, bc_ref, dt_ref, state_ref,
                *, n_chunks, ssm_inner, heads):
    s_idx = pl.program_id(1)
    tm = xs_ref.shape[0]
    gn = SSM_GROUPS * SSM_STATE
    gw = ssm_inner // SSM_GROUPS

    halo = xpad_ref.shape[0] - tm

    @pl.when(s_idx == 0)
    def _():
        state_ref[...] = jnp.zeros_like(state_ref)
        xpad_ref[0:halo, :] = jnp.zeros((halo, xpad_ref.shape[1]), BF16)

    @pl.when(s_idx > 0)
    def _():
        xpad_ref[0:halo, :] = xpad_ref[tm:tm + halo, :]

    xpad_ref[halo:, :] = xbc_ref[...]

    blk = CONV_ROW_BLOCK
    srow = lax.broadcasted_iota(jnp.int32, ((CONV_WIDTH - 1) * blk, blk + halo), 0)
    scol = lax.broadcasted_iota(jnp.int32, ((CONV_WIDTH - 1) * blk, blk + halo), 1)
    shift_stack = (scol == (srow % blk) + halo - (srow // blk + 1)).astype(BF16)
    def shifted_rows(r0):
        return jnp.dot(shift_stack, xpad_ref[r0:r0 + blk + halo, :], preferred_element_type=F32)

    shifted_next = shifted_rows(0)
    for r0 in range(0, tm, blk):
        shifted = shifted_next
        if r0 + blk < tm:
            shifted_next = shifted_rows(r0 + blk)
        xblk = xpad_ref[r0 + halo:r0 + halo + blk, :]
        conv = xblk.astype(F32) * convw_ref[CONV_WIDTH - 1:CONV_WIDTH, :] + convb_ref[...]
        for k in range(1, CONV_WIDTH):
            conv = conv + shifted[(k - 1) * blk:k * blk, :] * convw_ref[CONV_WIDTH - 1 - k:CONV_WIDTH - k, :]
        conv = _silu(conv)
        xs_ref[r0:r0 + blk, :] = conv[:, :ssm_inner]
        bc_ref[r0:r0 + blk, :] = conv[:, ssm_inner:]
    dt_ref[...] = jax.nn.softplus(dtr_ref[...] + dtb_ref[...])

    a_head = -jnp.exp(alog_ref[...]) * LOG2_E
    expand = expand_ref[...]
    lane_group = lax.broadcasted_iota(jnp.int32, (CHUNK, SPLIT_PIECES * heads), 1) // heads

    def pieces(x):
        parts = _split3(x)
        out = parts[SPLIT_PIECES - 1]
        for i in range(SPLIT_PIECES - 2, -1, -1):
            out = jnp.where(lane_group == i, parts[i], out)
        return out

    row = lax.broadcasted_iota(jnp.int32, (CHUNK, SPLIT_PIECES * CHUNK), 0)
    col = lax.broadcasted_iota(jnp.int32, (CHUNK, SPLIT_PIECES * CHUNK), 1)
    tril = ((col % CHUNK) <= row).astype(BF16)
    prow = lax.broadcasted_iota(jnp.int32, (CHUNK, LANES), 0)
    plane = lax.broadcasted_iota(jnp.int32, (CHUNK, LANES), 1)
    causal_pair = (plane % SSM_HEAD_DIM) <= prow
    first_head = plane < SSM_HEAD_DIM
    drow = lax.broadcasted_iota(jnp.int32, (CHUNK, ssm_inner), 0)
    dlane = lax.broadcasted_iota(jnp.int32, (CHUNK, ssm_inner), 1)
    diag = (dlane % SSM_HEAD_DIM) == drow

    def decay_stage(c):
        rows = pl.ds(c * CHUNK, CHUNK)
        dt = dt_ref[rows, :]
        bc = bc_ref[rows, :].astype(BF16)
        dt_e = jnp.dot(pieces(dt), expand, preferred_element_type=F32)
        a_parts = jnp.concatenate(_split3(dt * a_head), axis=0)
        a_cum_h = jnp.dot(tril, a_parts, preferred_element_type=F32)
        a_cum = jnp.dot(pieces(a_cum_h), expand, preferred_element_type=F32)
        cbs = []
        for g in range(SSM_GROUPS):
            b_g = bc[:, g * SSM_STATE:(g + 1) * SSM_STATE]
            c_g = bc[:, gn + g * SSM_STATE:gn + (g + 1) * SSM_STATE]
            cbs.append(lax.dot_general(c_g, b_g, (((1,), (1,)), ((), ())),
                                       preferred_element_type=F32))
        return dt_e, a_cum, cbs

    def operand_stage(c, dt_e, a_cum, cbs):
        rows = pl.ds(c * CHUNK, CHUNK)
        xs = xs_ref[rows, :]
        a_last = a_cum[CHUNK - 1:CHUNK, :]
        xdt = xs * dt_e
        xdec = (xdt * jnp.exp2(a_last - a_cum)).astype(BF16)
        xdt_b = xdt.astype(BF16)
        a_t = jnp.sum(jnp.where(diag, a_cum, 0.0), axis=0, keepdims=True)
        ms, rhss = [], []
        for g in range(SSM_GROUPS):
            cb2 = jnp.concatenate([cbs[g], cbs[g]], axis=1)
            for j in range(gw // LANES):
                c0 = g * gw + j * LANES
                seg = a_cum[:, c0:c0 + LANES] - a_t[:, c0:c0 + LANES]
                ldec = jnp.exp2(jnp.where(causal_pair, seg, -jnp.inf))
                ms.append((cb2 * ldec).astype(BF16))
                xp = xdt_b[:, c0:c0 + LANES]
                zero = jnp.zeros_like(xp)
                rhss.append(jnp.concatenate([jnp.where(first_head, xp, zero),
                                             jnp.where(first_head, zero, xp)], axis=0))
        return xdec, ms, rhss

    def matmul_stage(c, a_cum, xdec, ms, rhss):
        rows = pl.ds(c * CHUNK, CHUNK)
        bc = bc_ref[rows, :]
        a_last = a_cum[CHUNK - 1:CHUNK, :]
        y_offs, y_diags = [], []
        for g in range(SSM_GROUPS):
            g0 = g * gw
            b_g = bc[:, g * SSM_STATE:(g + 1) * SSM_STATE]
            c_g = bc[:, gn + g * SSM_STATE:gn + (g + 1) * SSM_STATE].astype(BF16)
            prev = state_ref[:, g0:g0 + gw]
            y_offs.append(jnp.dot(c_g, prev.astype(BF16), preferred_element_type=F32))
            new = jnp.dot(b_g.T.astype(BF16), xdec[:, g0:g0 + gw], preferred_element_type=F32)
            state_ref[:, g0:g0 + gw] = prev * jnp.exp2(a_last[:, g0:g0 + gw]) + new
        for m, rhs in zip(ms, rhss):
            y_diags.append(jnp.dot(m, rhs, preferred_element_type=F32))
        return jnp.concatenate(y_offs, axis=1), jnp.concatenate(y_diags, axis=1)

    def output_stage(c, a_cum, y_off, y_diag):
        rows = pl.ds(c * CHUNK, CHUNK)
        y = y_diag + y_off * jnp.exp2(a_cum) + dskip_ref[...] * xs_ref[rows, :]
        zc = z_ref[rows, :].astype(F32)
        y = y * _silu(zc)
        outs = []
        for g in range(SSM_GROUPS):
            yg = y[:, g * gw:(g + 1) * gw]
            outs.append(yg * lax.rsqrt(jnp.mean(yg * yg, axis=-1, keepdims=True) + RMS_EPS))
        o_ref[rows, :] = (jnp.concatenate(outs, axis=1) * normw_ref[...]).astype(o_ref.dtype)

    decays = {c: decay_stage(c) for c in range(min(2, n_chunks))}
    operands = {0: operand_stage(0, *decays[0])}
    for c in range(n_chunks):
        a_cum = decays[c][1]
        y_off, y_diag = matmul_stage(c, a_cum, *operands.pop(c))
        if c + 1 < n_chunks:
            operands[c + 1] = operand_stage(c + 1, *decays[c + 1])
        if c + 2 < n_chunks:
            decays[c + 2] = decay_stage(c + 2)
        output_stage(c, a_cum, y_off, y_diag)
        del decays[c]


def _ssd(xbc, z, dt_raw, conv_w, conv_b, dt_bias, a_log, d_skip, norm_w, *, batch, seq):
    t, conv_dim = xbc.shape
    inner = z.shape[1]
    heads = a_log.shape[0]
    hrep = dt_raw.shape[1]
    tm = SSD_ROW_TILE
    assert seq % tm == 0 and inner == heads * SSM_HEAD_DIM and hrep == SPLIT_PIECES * heads
    n_s = seq // tm
    rep = lambda p: jnp.repeat(p.astype(F32), SSM_HEAD_DIM).reshape(1, inner)
    rep_h = lambda p: jnp.tile(p.astype(F32), SPLIT_PIECES).reshape(1, hrep)
    expand = jnp.tile(jnp.repeat(jnp.eye(heads, dtype=BF16), SSM_HEAD_DIM, axis=1),
                      (SPLIT_PIECES, 1))
    row_map = lambda b, s: (b * n_s + s, 0)
    nbytes = (2 * tm * (conv_dim + 2 * inner) * 2 + 2 * tm * LANES * 4
              + (tm + SUBLANES) * conv_dim * 4 + tm * conv_dim * 4 + tm * LANES * 4
              + SSM_STATE * inner * 4 + 3 * tm * conv_dim * 4 + 24 * CHUNK * inner * 4)
    return pl.pallas_call(
        functools.partial(_ssd_kernel, n_chunks=tm // CHUNK, ssm_inner=inner, heads=heads),
        grid=(batch, n_s),
        in_specs=[
            pl.BlockSpec((tm, conv_dim), row_map),
            pl.BlockSpec((tm, inner), row_map),
            pl.BlockSpec((tm, hrep), row_map),
            _const_spec((CONV_WIDTH, conv_dim)),
            _const_spec((1, conv_dim)),
            _const_spec((1, hrep)),
            _const_spec((1, hrep)),
            _const_spec((1, inner)),
            _const_spec((1, inner)),
            _const_spec((hrep, inner)),
        ],
        out_specs=pl.BlockSpec((tm, inner), row_map),
        out_shape=jax.ShapeDtypeStruct((t, inner), BF16),
        scratch_shapes=[
            pltpu.VMEM((tm + BF16_ROWS, conv_dim), BF16),
            pltpu.VMEM((tm, inner), F32),
            pltpu.VMEM((tm, conv_dim - inner), F32),
            pltpu.VMEM((tm, hrep), F32),
            pltpu.VMEM((SSM_STATE, inner), F32),
        ],
        compiler_params=pltpu.CompilerParams(
            dimension_semantics=("arbitrary", "arbitrary"), vmem_limit_bytes=_vmem_limit(nbytes)),
        name="ssd",
    )(xbc, z, dt_raw, conv_w.astype(F32), conv_b.reshape(1, conv_dim).astype(F32),
      rep_h(dt_bias), rep_h(a_log), rep(d_skip),
      norm_w.reshape(1, inner).astype(F32), expand)


def _attn_key_chunks(qv):
    q_per_group = LANES // CHUNK
    return range(qv * q_per_group, qv * q_per_group + LEFT_CHUNKS + q_per_group)


def _attn_kernel(q_ref, kprev_ref, kcur_ref, vprev_ref, vcur_ref, bias0_ref, bias1_ref, o_ref,
                 s_ref, p_ref, *, heads):
    tq = s_ref.shape[2]
    n_slots = s_ref.shape[0]
    n_kc = s_ref.shape[1] // CHUNK
    n_qv = tq // LANES
    pair = LANES // ATTN_HEAD_DIM
    step_rows = q_ref.shape[0]
    bias_refs = (bias0_ref, bias1_ref)
    lane_head = lax.broadcasted_iota(jnp.int32, (tq, LANES), 1) // ATTN_HEAD_DIM

    def window(prev_ref, cur_ref, tile, lo):
        first = tile * tq
        last = first + n_kc * CHUNK - step_rows
        return jnp.concatenate([prev_ref[first:step_rows, lo:lo + LANES],
                                cur_ref[0:last, lo:lo + LANES]], axis=0)
    for slot in range(n_slots):
        for qv in range(n_qv):
            for kc in range(n_kc):
                if kc not in _attn_key_chunks(qv):
                    p_ref[slot, kc * CHUNK:(kc + 1) * CHUNK, qv * LANES:(qv + 1) * LANES] = jnp.zeros(
                        (CHUNK, LANES), BF16)

    items = [(tile, h) for tile in range(step_rows // tq) for h in range(heads)]

    def scores(u):
        tile, h = items[u]
        lo = (h // pair) * LANES
        qp = q_ref[tile * tq:(tile + 1) * tq, lo:lo + LANES]
        kp = window(kprev_ref, kcur_ref, tile, lo)
        qm = jnp.where(lane_head == h % pair, qp, jnp.zeros_like(qp))
        s_ref[u % n_slots] = lax.dot_general(kp, qm, (((1,), (1,)), ((), ())),
                                             preferred_element_type=F32)

    def softmax(u, qv):
        tile, h = items[u]
        bias_ref = bias_refs[tile]
        slot = u % n_slots
        cols = slice(qv * LANES, (qv + 1) * LANES)
        mx = None
        for kc in _attn_key_chunks(qv):
            rows = slice(kc * CHUNK, (kc + 1) * CHUNK)
            t = s_ref[slot, rows, cols] + bias_ref[0, h, rows, cols]
            mx = t if mx is None else jnp.maximum(mx, t)
        m = jnp.max(mx, axis=0, keepdims=True)
        acc = None
        for kc in _attn_key_chunks(qv):
            rows = slice(kc * CHUNK, (kc + 1) * CHUNK)
            p = jnp.exp2(s_ref[slot, rows, cols] + bias_ref[0, h, rows, cols] - m)
            acc = p if acc is None else acc + p
            p_ref[slot, rows, cols] = p.astype(BF16)
        return 1.0 / jnp.sum(acc, axis=0, keepdims=True)

    def weighted_values(u, inv):
        tile, h = items[u]
        lo = (h // pair) * LANES
        d0 = (h % pair) * ATTN_HEAD_DIM
        vpt = window(vprev_ref, vcur_ref, tile, lo).T
        o_t = jnp.dot(vpt[d0:d0 + ATTN_HEAD_DIM, :], p_ref[u % n_slots],
                      preferred_element_type=F32)
        return o_t * inv

    outs = []

    def finish(u, inv):
        tile, h = items[u]
        outs.append(weighted_values(u, inv))
        if len(outs) == pair:
            lo = (h // pair) * LANES
            o_ref[tile * tq:(tile + 1) * tq, lo:lo + LANES] = jnp.concatenate(
                outs, axis=0).T.astype(o_ref.dtype)
            outs.clear()

    scores(0)
    inv = None
    for u in range(len(items)):
        if u + 1 < len(items):
            scores(u + 1)
        if u > 0:
            finish(u - 1, inv)
        inv = jnp.concatenate([softmax(u, qv) for qv in range(n_qv)], axis=1)
    finish(len(items) - 1, inv)


def _attn_bias_kernel(f_ref, o_ref):
    n_var, _, nk, tq = o_ref.shape
    width = f_ref.shape[2]
    x = jnp.broadcast_to(f_ref[0], (nk, width))
    rolled = pltpu.roll(x, 0, 1, stride=1, stride_axis=0)
    t = rolled[:, nk:] * LOG2_E
    key = lax.broadcasted_iota(jnp.int32, (nk, tq), 0)
    kc = key // CHUNK
    qc = lax.broadcasted_iota(jnp.int32, (nk, tq), 1) // CHUNK
    band = (kc >= qc) & (kc <= qc + LEFT_CHUNKS)
    for v in range(n_var):
        ok = band & (key >= (n_var - 1 - v) * tq)
        o_ref[v, 0] = jnp.where(ok, t, -jnp.inf)


def _attn_bias_table(rel_bias):
    heads = rel_bias.shape[0]
    nk = ATTN_K_BLOCKS * ATTN_Q_TILE
    width = nk + ATTN_Q_TILE
    dist = jnp.arange(width) - nk + LEFT_CHUNKS * CHUNK
    f = rel_bias.astype(F32)[:, jnp.clip(dist, -MAX_REL_DIST, MAX_REL_DIST) + MAX_REL_DIST]
    return pl.pallas_call(
        _attn_bias_kernel,
        grid=(heads,),
        in_specs=[pl.BlockSpec((1, 1, width), lambda h: (h, 0, 0))],
        out_specs=pl.BlockSpec((ATTN_K_BLOCKS, 1, nk, ATTN_Q_TILE), lambda h: (0, h, 0, 0)),
        out_shape=jax.ShapeDtypeStruct((ATTN_K_BLOCKS, heads, nk, ATTN_Q_TILE), F32),
        compiler_params=pltpu.CompilerParams(dimension_semantics=("arbitrary",)),
        name="attn_bias",
    )(f.reshape(heads, 1, width))


def _attention(q, k, v, rel_bias, *, batch, seq):
    t, inner = q.shape
    heads = rel_bias.shape[0]
    tq = ATTN_Q_TILE
    rows = ATTN_TILES_PER_STEP * tq
    assert ATTN_TILES_PER_STEP == 2 and ATTN_K_BLOCKS == ATTN_TILES_PER_STEP + 1
    assert seq % rows == 0 and inner == heads * ATTN_HEAD_DIM
    n_s = seq // rows
    bias = _attn_bias_table(rel_bias)
    nk = ATTN_K_BLOCKS * tq

    cur = pl.BlockSpec((rows, inner), lambda j, b: (b * n_s + j, 0))
    prev = pl.BlockSpec((rows, inner), lambda j, b: (b * n_s + jnp.maximum(j - 1, 0), 0))

    def bias_spec(tile):
        return pl.BlockSpec(
            (1, heads, nk, tq),
            lambda j, b: (jnp.minimum(j * ATTN_TILES_PER_STEP + tile, ATTN_K_BLOCKS - 1), 0, 0, 0),
            pipeline_mode=pl.Buffered(1))

    nbytes = (2 * 6 * rows * inner * 2 + ATTN_TILES_PER_STEP * heads * tq * nk * 4
              + ATTN_SLOTS * tq * nk * (4 + 2) + 4 * tq * nk * 4)
    return pl.pallas_call(
        functools.partial(_attn_kernel, heads=heads),
        grid=(n_s, batch),
        in_specs=[cur, prev, cur, prev, cur, bias_spec(0), bias_spec(1)],
        out_specs=cur,
        out_shape=jax.ShapeDtypeStruct((t, inner), BF16),
        scratch_shapes=[pltpu.VMEM((ATTN_SLOTS, nk, tq), F32), pltpu.VMEM((ATTN_SLOTS, nk, tq), BF16)],
        compiler_params=pltpu.CompilerParams(
            dimension_semantics=("arbitrary", "arbitrary"), vmem_limit_bytes=_vmem_limit(nbytes)),
        name="band_attn",
    )(q, k, k, v, v, bias, bias)


def _merge_kernel(x_ref, ys_ref, ya_ref, g_ref, ws_ref, wa_ref, wo_ref, o_ref):
    d = x_ref.shape[1]
    half = x_ref.shape[0] // 2
    halves = (slice(0, half), slice(half, 2 * half))

    def branches(rows):
        return (jnp.dot(ys_ref[rows, :], ws_ref[...], preferred_element_type=F32),
                jnp.dot(ya_ref[rows, :], wa_ref[...], preferred_element_type=F32))

    def gate(rows, bs, ba):
        g = g_ref[rows, :].astype(F32)
        return (jax.nn.sigmoid(g[:, :d]) * bs + jax.nn.sigmoid(g[:, d:]) * ba).astype(BF16)

    def project(rows, merged):
        o_ref[rows, :] = x_ref[rows, :] + jnp.dot(merged, wo_ref[...], preferred_element_type=F32)

    b0 = branches(halves[0])
    b1 = branches(halves[1])
    m0 = gate(halves[0], *b0)
    project(halves[0], m0)
    m1 = gate(halves[1], *b1)
    project(halves[1], m1)


def _merge(x2d, y_ssm, y_attn, g, w_bs, w_ba, w_out):
    t, d = x2d.shape
    tm = FFN_ROW_TILE
    assert t % tm == 0
    row = lambda w: pl.BlockSpec((tm, w), lambda i: (i, 0))
    nbytes = (w_bs.size + w_ba.size + w_out.size) * 2 + 2 * tm * (2 * d * 4 + 4 * d * 2) + 6 * tm * d * 4
    return pl.pallas_call(
        _merge_kernel,
        grid=(t // tm,),
        in_specs=[row(d), row(y_ssm.shape[1]), row(y_attn.shape[1]), row(2 * d),
                  _const_spec(w_bs.shape), _const_spec(w_ba.shape), _const_spec(w_out.shape)],
        out_specs=row(d),
        out_shape=jax.ShapeDtypeStruct((t, d), F32),
        compiler_params=pltpu.CompilerParams(
            dimension_semantics=("arbitrary",), vmem_limit_bytes=_vmem_limit(nbytes)),
        name="merge",
    )(x2d, y_ssm, y_attn, g, w_bs.astype(BF16), w_ba.astype(BF16), w_out.astype(BF16))


def kernel(x, ffn1_norm_w, ffn1_w_gu, ffn1_w_down, mix_norm_w, w_in, conv_w, conv_b, dt_bias, A_log,
           D_skip, ssm_norm_w, rel_bias, w_branch_ssm, w_branch_attn, w_out, ffn2_norm_w, ffn2_w_gu,
           ffn2_w_down, final_norm_w):
    batch, seq, d = x.shape
    depth = ffn1_w_gu.shape[0]
    ssm_heads = A_log.shape[1]
    ssm_inner = ssm_heads * SSM_HEAD_DIM
    conv_dim = conv_w.shape[2]
    attn_inner = rel_bias.shape[1] * ATTN_HEAD_DIM
    xf = x.reshape(batch * seq, d)
    for l in range(depth):
        last = l == depth - 1
        xf = _ffn(xf, ffn1_norm_w[l], ffn1_w_gu[l], ffn1_w_down[l], final_norm_w, final_norm=False)
        z, xbc, dt_raw, q, k, v, g = _in_proj(
            xf, mix_norm_w[l], w_in[l], ssm_inner=ssm_inner, conv_dim=conv_dim,
            ssm_heads=ssm_heads, attn_inner=attn_inner)
        y_ssm = _ssd(xbc, z, dt_raw, conv_w[l], conv_b[l], dt_bias[l], A_log[l], D_skip[l],
                     ssm_norm_w[l], batch=batch, seq=seq)
        y_attn = _attention(q, k, v, rel_bias[l], batch=batch, seq=seq)
        xf = _merge(xf, y_ssm, y_attn, g, w_branch_ssm[l], w_branch_attn[l], w_out[l])
        xf = _ffn(xf, ffn2_norm_w[l], ffn2_w_gu[l], ffn2_w_down[l], final_norm_w, final_norm=last)
    if depth == 0:
        raise ValueError("depth must be >= 1")
    return xf.reshape(batch, seq, d)
```

```python
import functools

import jax
import jax.numpy as jnp
from jax import lax
from jax.experimental import pallas as pl
from jax.experimental.pallas import tpu as pltpu

F32 = jnp.float32
BF16 = jnp.bfloat16

CHUNK = 64
SSM_HEAD_DIM = 64
SSM_GROUPS = 2
SSM_STATE = 128
CONV_WIDTH = 4
ATTN_HEAD_DIM = 64
LEFT_CHUNKS = 8
MAX_REL_DIST = 128
FFN_RES_SCALE = 0.5
RMS_EPS = 1e-6
LOG2_E = 1.4426950408889634

LANES = 128
SUBLANES = 8
BF16_ROWS = 16
MXU_DIM = 256
VMEM_BYTES = 64 * 1024 * 1024

ROW_TILE = 512
FFN_ROW_TILE = 1024
FFN_COL_TILE = 512
SSD_ROW_TILE = 1024
CONV_ROW_BLOCK = 128
ATTN_Q_CHUNKS = 4
ATTN_Q_TILE = ATTN_Q_CHUNKS * CHUNK
ATTN_TILES_PER_STEP = 2
ATTN_SLOTS = 2
ATTN_K_BLOCKS = (LEFT_CHUNKS + ATTN_Q_CHUNKS) * CHUNK // ATTN_Q_TILE


def _vmem_limit(nbytes):
    return int(min(nbytes * 1.25 + (8 << 20), VMEM_BYTES - (6 << 20)))


def _const_spec(shape):
    zeros = (0,) * len(shape)
    return pl.BlockSpec(shape, lambda *_: zeros, pipeline_mode=pl.Buffered(1))


def _rmsnorm(x, w):
    ms = jnp.mean(x * x, axis=-1, keepdims=True)
    return x * lax.rsqrt(ms + RMS_EPS) * w


def _silu(x):
    h = 0.5 * x
    return h + h * jnp.tanh(h)


SPLIT_PIECES = 3


def _split3(x):
    hi = x.astype(BF16)
    r1 = x - hi.astype(F32)
    mid = r1.astype(BF16)
    lo = (r1 - mid.astype(F32)).astype(BF16)
    return hi, mid, lo


def _ffn_kernel(x_ref, nw_ref, wgu_ref, wd_ref, fnw_ref, o_ref, h_ref, a_ref, *, d_ff, final_norm):
    half = x_ref.shape[0] // 2
    halves = (slice(0, half), slice(half, 2 * half))
    col_tiles = [(c, min(FFN_COL_TILE, d_ff - c)) for c in range(0, d_ff, FFN_COL_TILE)]

    def prologue(rows):
        h_ref[rows, :] = _rmsnorm(x_ref[rows, :], nw_ref[...]).astype(BF16)

    def gate_up(rows, c, w):
        h = h_ref[rows, :]
        g = jnp.dot(h, wgu_ref[:, c:c + w], preferred_element_type=F32)
        u = jnp.dot(h, wgu_ref[:, d_ff + c:d_ff + c + w], preferred_element_type=F32)
        a_ref[rows, c:c + w] = (_silu(g) * u).astype(BF16)

    def down(rows):
        return jnp.dot(a_ref[rows, :], wd_ref[...], preferred_element_type=F32)

    def epilogue(rows, acc):
        y = x_ref[rows, :] + FFN_RES_SCALE * acc
        if final_norm:
            y = _rmsnorm(y, fnw_ref[...])
        o_ref[rows, :] = y

    prologue(halves[0])
    for i, (c, w) in enumerate(col_tiles):
        gate_up(halves[0], c, w)
        if i == 0:
            prologue(halves[1])
    acc0 = down(halves[0])
    for i, (c, w) in enumerate(col_tiles):
        gate_up(halves[1], c, w)
        if i == 0:
            epilogue(halves[0], acc0)
    epilogue(halves[1], down(halves[1]))


def _ffn(x2d, norm_w, w_gu, w_down, final_norm_w, *, final_norm):
    t, d = x2d.shape
    d_ff = w_down.shape[0]
    assert t % FFN_ROW_TILE == 0 and d_ff % LANES == 0
    tm = FFN_ROW_TILE
    nbytes = (3 * d * d_ff * 2 + 4 * tm * d * 4 + tm * d * 2 + tm * d_ff * 2
              + 2 * tm * FFN_COL_TILE * 4 + 2 * tm * d * 4)
    return pl.pallas_call(
        functools.partial(_ffn_kernel, d_ff=d_ff, final_norm=final_norm),
        grid=(t // tm,),
        in_specs=[
            pl.BlockSpec((tm, d), lambda i: (i, 0)),
            _const_spec((1, d)),
            _const_spec((d, 2 * d_ff)),
            _const_spec((d_ff, d)),
            _const_spec((1, d)),
        ],
        out_specs=pl.BlockSpec((tm, d), lambda i: (i, 0)),
        out_shape=jax.ShapeDtypeStruct((t, d), F32),
        scratch_shapes=[pltpu.VMEM((tm, d), BF16), pltpu.VMEM((tm, d_ff), BF16)],
        compiler_params=pltpu.CompilerParams(
            dimension_semantics=("arbitrary",), vmem_limit_bytes=_vmem_limit(nbytes)),
        name="ffn_final" if final_norm else "ffn",
    )(x2d, norm_w.reshape(1, d), w_gu.astype(BF16), w_down.astype(BF16), final_norm_w.reshape(1, d))


def _inproj_kernel(x_ref, nw_ref, w_ref, z_ref, xbc_ref, q_ref, k_ref, v_ref, g_ref, dt_ref, h_ref,
                   *, q_scale):
    h_ref[...] = _rmsnorm(x_ref[...], nw_ref[...]).astype(BF16)
    col = 0
    for o_ref in (z_ref, xbc_ref, q_ref, k_ref, v_ref, g_ref, dt_ref):
        n = o_ref.shape[1]
        step = min(n, 2 * MXU_DIM)
        for c in range(0, n, step):
            r = jnp.dot(h_ref[...], w_ref[:, col + c:col + c + step], preferred_element_type=F32)
            if o_ref is q_ref:
                r = r * q_scale
            o_ref[:, c:c + step] = r.astype(o_ref.dtype)
        col += n


def _in_proj(x2d, norm_w, w_in, *, ssm_inner, conv_dim, ssm_heads, attn_inner):
    t, d = x2d.shape
    tm = ROW_TILE
    dt_lo = ssm_inner + conv_dim
    dt_hi = dt_lo + ssm_heads
    assert dt_lo % LANES == 0
    w_dt = w_in[:, dt_lo:dt_hi]
    w = jnp.concatenate([w_in[:, :dt_lo], w_in[:, dt_hi:]] + [w_dt] * SPLIT_PIECES, axis=1).astype(BF16)
    widths = [ssm_inner, conv_dim, attn_inner, attn_inner, attn_inner, w_in.shape[1] - dt_hi - 3 * attn_inner,
              SPLIT_PIECES * ssm_heads]
    out_dtypes = [BF16, BF16, BF16, BF16, BF16, BF16, F32]
    nbytes = (sum(widths) * d * 2 + 2 * tm * d * 4 + tm * d * 2
              + 2 * sum(tm * wd * jnp.dtype(dt).itemsize for wd, dt in zip(widths, out_dtypes))
              + 2 * tm * 2 * MXU_DIM * 4)
    z, xbc, q, k, v, g, dt_raw = pl.pallas_call(
        functools.partial(_inproj_kernel, q_scale=ATTN_HEAD_DIM ** -0.5 * LOG2_E),
        grid=(t // tm,),
        in_specs=[pl.BlockSpec((tm, d), lambda i: (i, 0)), _const_spec((1, d)),
                  _const_spec((d, sum(widths)))],
        out_specs=[pl.BlockSpec((tm, wd), lambda i: (i, 0)) for wd in widths],
        out_shape=[jax.ShapeDtypeStruct((t, wd), dt) for wd, dt in zip(widths, out_dtypes)],
        scratch_shapes=[pltpu.VMEM((tm, d), BF16)],
        compiler_params=pltpu.CompilerParams(
            dimension_semantics=("arbitrary",), vmem_limit_bytes=_vmem_limit(nbytes)),
        name="in_proj",
    )(x2d, norm_w.reshape(1, d), w)
    return z, xbc, dt_raw, q, k, v, g


def _ssd_kernel(xbc_ref, z_ref, dtr_ref, convw_ref, convb_ref, dtb_ref, alog_ref, dskip_ref,
                normw_ref, expand_ref, o_ref,
                xpad_ref, xs_ref, bc_ref, dt_ref, state_ref,
                *, n_chunks, ssm_inner, heads):
    s_idx = pl.program_id(1)
    tm = xs_ref.shape[0]
    gn = SSM_GROUPS * SSM_STATE
    gw = ssm_inner // SSM_GROUPS

    halo = xpad_ref.shape[0] - tm

    @pl.when(s_idx == 0)
    def _():
        state_ref[...] = jnp.zeros_like(state_ref)
        xpad_ref[0:halo, :] = jnp.zeros((halo, xpad_ref.shape[1]), BF16)

    @pl.when(s_idx > 0)
    def _():
        xpad_ref[0:halo, :] = xpad_ref[tm:tm + halo, :]

    xpad_ref[halo:, :] = xbc_ref[...]

    blk = CONV_ROW_BLOCK
    srow = lax.broadcasted_iota(jnp.int32, ((CONV_WIDTH - 1) * blk, blk + halo), 0)
    scol = lax.broadcasted_iota(jnp.int32, ((CONV_WIDTH - 1) * blk, blk + halo), 1)
    shift_stack = (scol == (srow % blk) + halo - (srow // blk + 1)).astype(BF16)
    def shifted_rows(r0):
        return jnp.dot(shift_stack, xpad_ref[r0:r0 + blk + halo, :], preferred_element_type=F32)

    shifted_next = shifted_rows(0)
    for r0 in range(0, tm, blk):
        shifted = shifted_next
        if r0 + blk < tm:
            shifted_next = shifted_rows(r0 + blk)
        xblk = xpad_ref[r0 + halo:r0 + halo + blk, :]
        conv = xblk.astype(F32) * convw_ref[CONV_WIDTH - 1:CONV_WIDTH, :] + convb_ref[...]
        for k in range(1, CONV_WIDTH):
            conv = conv + shifted[(k - 1) * blk:k * blk, :] * convw_ref[CONV_WIDTH - 1 - k:CONV_WIDTH - k, :]
        conv = _silu(conv)
        xs_ref[r0:r0 + blk, :] = conv[:, :ssm_inner]
        bc_ref[r0:r0 + blk, :] = conv[:, ssm_inner:]
    dt_ref[...] = jax.nn.softplus(dtr_ref[...] + dtb_ref[...])

    a_head = -jnp.exp(alog_ref[...]) * LOG2_E
    expand = expand_ref[...]
    lane_group = lax.broadcasted_iota(jnp.int32, (CHUNK, SPLIT_PIECES * heads), 1) // heads

    def pieces(x):
        parts = _split3(x)
        out = parts[SPLIT_PIECES - 1]
        for i in range(SPLIT_PIECES - 2, -1, -1):
            out = jnp.where(lane_group == i, parts[i], out)
        return out

    row = lax.broadcasted_iota(jnp.int32, (CHUNK, SPLIT_PIECES * CHUNK), 0)
    col = lax.broadcasted_iota(jnp.int32, (CHUNK, SPLIT_PIECES * CHUNK), 1)
    tril = ((col % CHUNK) <= row).astype(BF16)
    prow = lax.broadcasted_iota(jnp.int32, (CHUNK, LANES), 0)
    plane = lax.broadcasted_iota(jnp.int32, (CHUNK, LANES), 1)
    causal_pair = (plane % SSM_HEAD_DIM) <= prow
    first_head = plane < SSM_HEAD_DIM
    diag_pair = (plane % SSM_HEAD_DIM) == prow

    def decay_stage(c):
        rows = pl.ds(c * CHUNK, CHUNK)
        dt = dt_ref[rows, :]
        bc = bc_ref[rows, :].astype(BF16)
        dt_e = jnp.dot(pieces(dt), expand, preferred_element_type=F32)
        a_parts = jnp.concatenate(_split3(dt * a_head), axis=0)
        a_cum_h = jnp.dot(tril, a_parts, preferred_element_type=F32)
        a_cum = jnp.dot(pieces(a_cum_h), expand, preferred_element_type=F32)
        cbs = []
        for g in range(SSM_GROUPS):
            b_g = bc[:, g * SSM_STATE:(g + 1) * SSM_STATE]
            c_g = bc[:, gn + g * SSM_STATE:gn + (g + 1) * SSM_STATE]
            cbs.append(lax.dot_general(c_g, b_g, (((1,), (1,)), ((), ())),
                                       preferred_element_type=F32))
        return dt_e, a_cum, cbs

    def operand_stage(c, dt_e, a_cum, cbs):
        rows = pl.ds(c * CHUNK, CHUNK)
        xdecs, ms, rhss = [], [], []
        for g in range(SSM_GROUPS):
            cb2 = jnp.concatenate([cbs[g], cbs[g]], axis=1)
            for j in range(gw // LANES):
                cols = slice(g * gw + j * LANES, g * gw + (j + 1) * LANES)
                a_c = a_cum[:, cols]
                xdt = xs_ref[rows, cols] * dt_e[:, cols]
                xdecs.append((xdt * jnp.exp2(a_c[CHUNK - 1:CHUNK, :] - a_c)).astype(BF16))
                a_t = jnp.sum(jnp.where(diag_pair, a_c, 0.0), axis=0, keepdims=True)
                ldec = jnp.exp2(jnp.where(causal_pair, a_c - a_t, -jnp.inf))
                ms.append((cb2 * ldec).astype(BF16))
                xp = xdt.astype(BF16)
                zero = jnp.zeros_like(xp)
                rhss.append(jnp.concatenate([jnp.where(first_head, xp, zero),
                                             jnp.where(first_head, zero, xp)], axis=0))
        return jnp.concatenate(xdecs, axis=1), ms, rhss

    def matmul_stage(c, a_cum, xdec, ms, rhss):
        rows = pl.ds(c * CHUNK, CHUNK)
        bc = bc_ref[rows, :]
        a_last = a_cum[CHUNK - 1:CHUNK, :]
        y_offs, y_diags = [], []
        for g in range(SSM_GROUPS):
            g0 = g * gw
            b_g = bc[:, g * SSM_STATE:(g + 1) * SSM_STATE]
            c_g = bc[:, gn + g * SSM_STATE:gn + (g + 1) * SSM_STATE].astype(BF16)
            prev = state_ref[:, g0:g0 + gw]
            y_offs.append(jnp.dot(c_g, prev.astype(BF16), preferred_element_type=F32))
            new = jnp.dot(b_g.T.astype(BF16), xdec[:, g0:g0 + gw], preferred_element_type=F32)
            state_ref[:, g0:g0 + gw] = prev * jnp.exp2(a_last[:, g0:g0 + gw]) + new
        for m, rhs in zip(ms, rhss):
            y_diags.append(jnp.dot(m, rhs, preferred_element_type=F32))
        return jnp.concatenate(y_offs, axis=1), y_diags

    def output_stage(c, a_cum, y_off, y_diags):
        rows = pl.ds(c * CHUNK, CHUNK)
        n_lg = gw // LANES
        for g in range(SSM_GROUPS):
            ys, ssq = [], None
            for j in range(n_lg):
                cols = slice(g * gw + j * LANES, g * gw + (j + 1) * LANES)
                y = (y_diags[g * n_lg + j] + y_off[:, cols] * jnp.exp2(a_cum[:, cols])
                     + dskip_ref[:, cols] * xs_ref[rows, cols])
                y = y * _silu(z_ref[rows, cols].astype(F32))
                ys.append(y)
                part = jnp.sum(y * y, axis=-1, keepdims=True)
                ssq = part if ssq is None else ssq + part
            scale = lax.rsqrt(ssq * (1.0 / gw) + RMS_EPS)
            for j in range(n_lg):
                cols = slice(g * gw + j * LANES, g * gw + (j + 1) * LANES)
                o_ref[rows, cols] = (ys[j] * scale * normw_ref[:, cols]).astype(o_ref.dtype)

    decays = {c: decay_stage(c) for c in range(min(2, n_chunks))}
    operands = {0: operand_stage(0, *decays[0])}
    for c in range(n_chunks):
        a_cum = decays[c][1]
        y_off, y_diag = matmul_stage(c, a_cum, *operands.pop(c))
        if c + 1 < n_chunks:
            operands[c + 1] = operand_stage(c + 1, *decays[c + 1])
        if c + 2 < n_chunks:
            decays[c + 2] = decay_stage(c + 2)
        output_stage(c, a_cum, y_off, y_diag)
        del decays[c]


def _ssd(xbc, z, dt_raw, conv_w, conv_b, dt_bias, a_log, d_skip, norm_w, *, batch, seq):
    t, conv_dim = xbc.shape
    inner = z.shape[1]
    heads = a_log.shape[0]
    hrep = dt_raw.shape[1]
    tm = SSD_ROW_TILE
    assert seq % tm == 0 and inner == heads * SSM_HEAD_DIM and hrep == SPLIT_PIECES * heads
    n_s = seq // tm
    rep = lambda p: jnp.repeat(p.astype(F32), SSM_HEAD_DIM).reshape(1, inner)
    rep_h = lambda p: jnp.tile(p.astype(F32), SPLIT_PIECES).reshape(1, hrep)
    expand = jnp.tile(jnp.repeat(jnp.eye(heads, dtype=BF16), SSM_HEAD_DIM, axis=1),
                      (SPLIT_PIECES, 1))
    row_map = lambda b, s: (b * n_s + s, 0)
    nbytes = (2 * tm * (conv_dim + 2 * inner) * 2 + 2 * tm * LANES * 4
              + (tm + SUBLANES) * conv_dim * 4 + tm * conv_dim * 4 + tm * LANES * 4
              + SSM_STATE * inner * 4 + 3 * tm * conv_dim * 4 + 24 * CHUNK * inner * 4)
    return pl.pallas_call(
        functools.partial(_ssd_kernel, n_chunks=tm // CHUNK, ssm_inner=inner, heads=heads),
        grid=(batch, n_s),
        in_specs=[
            pl.BlockSpec((tm, conv_dim), row_map),
            pl.BlockSpec((tm, inner), row_map),
            pl.BlockSpec((tm, hrep), row_map),
            _const_spec((CONV_WIDTH, conv_dim)),
            _const_spec((1, conv_dim)),
            _const_spec((1, hrep)),
            _const_spec((1, hrep)),
            _const_spec((1, inner)),
            _const_spec((1, inner)),
            _const_spec((hrep, inner)),
        ],
        out_specs=pl.BlockSpec((tm, inner), row_map),
        out_shape=jax.ShapeDtypeStruct((t, inner), BF16),
        scratch_shapes=[
            pltpu.VMEM((tm + BF16_ROWS, conv_dim), BF16),
            pltpu.VMEM((tm, inner), F32),
            pltpu.VMEM((tm, conv_dim - inner), F32),
            pltpu.VMEM((tm, hrep), F32),
            pltpu.VMEM((SSM_STATE, inner), F32),
        ],
        compiler_params=pltpu.CompilerParams(
            dimension_semantics=("arbitrary", "arbitrary"), vmem_limit_bytes=_vmem_limit(nbytes)),
        name="ssd",
    )(xbc, z, dt_raw, conv_w.astype(F32), conv_b.reshape(1, conv_dim).astype(F32),
      rep_h(dt_bias), rep_h(a_log), rep(d_skip),
      norm_w.reshape(1, inner).astype(F32), expand)


def _attn_key_chunks(qv):
    q_per_group = LANES // CHUNK
    return range(qv * q_per_group, qv * q_per_group + LEFT_CHUNKS + q_per_group)


def _attn_kernel(q_ref, kprev_ref, kcur_ref, vprev_ref, vcur_ref, bias0_ref, bias1_ref, o_ref,
                 s_ref, p_ref, *, heads):
    tq = s_ref.shape[2]
    n_slots = s_ref.shape[0]
    n_kc = s_ref.shape[1] // CHUNK
    n_qv = tq // LANES
    pair = LANES // ATTN_HEAD_DIM
    step_rows = q_ref.shape[0]
    bias_refs = (bias0_ref, bias1_ref)
    lane_head = lax.broadcasted_iota(jnp.int32, (tq, LANES), 1) // ATTN_HEAD_DIM

    def window(prev_ref, cur_ref, tile, lo):
        first = tile * tq
        last = first + n_kc * CHUNK - step_rows
        return jnp.concatenate([prev_ref[first:step_rows, lo:lo + LANES],
                                cur_ref[0:last, lo:lo + LANES]], axis=0)
    for slot in range(n_slots):
        for qv in range(n_qv):
            for kc in range(n_kc):
                if kc not in _attn_key_chunks(qv):
                    p_ref[slot, kc * CHUNK:(kc + 1) * CHUNK, qv * LANES:(qv + 1) * LANES] = jnp.zeros(
                        (CHUNK, LANES), BF16)

    items = [(tile, h) for tile in range(step_rows // tq) for h in range(heads)]

    def scores(u):
        tile, h = items[u]
        lo = (h // pair) * LANES
        qp = q_ref[tile * tq:(tile + 1) * tq, lo:lo + LANES]
        kp = window(kprev_ref, kcur_ref, tile, lo)
        qm = jnp.where(lane_head == h % pair, qp, jnp.zeros_like(qp))
        s_ref[u % n_slots] = lax.dot_general(kp, qm, (((1,), (1,)), ((), ())),
                                             preferred_element_type=F32)

    def softmax(u, qv):
        tile, h = items[u]
        bias_ref = bias_refs[tile]
        slot = u % n_slots
        cols = slice(qv * LANES, (qv + 1) * LANES)
        mx = None
        for kc in _attn_key_chunks(qv):
            rows = slice(kc * CHUNK, (kc + 1) * CHUNK)
            t = s_ref[slot, rows, cols] + bias_ref[0, h, rows, cols]
            mx = t if mx is None else jnp.maximum(mx, t)
        m = jnp.max(mx, axis=0, keepdims=True)
        acc = None
        for kc in _attn_key_chunks(qv):
            rows = slice(kc * CHUNK, (kc + 1) * CHUNK)
            p = jnp.exp2(s_ref[slot, rows, cols] + bias_ref[0, h, rows, cols] - m)
            acc = p if acc is None else acc + p
            p_ref[slot, rows, cols] = p.astype(BF16)
        return 1.0 / jnp.sum(acc, axis=0, keepdims=True)

    def weighted_values(u, inv):
        tile, h = items[u]
        lo = (h // pair) * LANES
        d0 = (h % pair) * ATTN_HEAD_DIM
        vpt = window(vprev_ref, vcur_ref, tile, lo).T
        o_t = jnp.dot(vpt[d0:d0 + ATTN_HEAD_DIM, :], p_ref[u % n_slots],
                      preferred_element_type=F32)
        return o_t * inv

    outs = []

    def finish(u, inv):
        tile, h = items[u]
        outs.append(weighted_values(u, inv))
        if len(outs) == pair:
            lo = (h // pair) * LANES
            o_ref[tile * tq:(tile + 1) * tq, lo:lo + LANES] = jnp.concatenate(
                outs, axis=0).T.astype(o_ref.dtype)
            outs.clear()

    scores(0)
    inv = None
    for u in range(len(items)):
        if u + 1 < len(items):
            scores(u + 1)
        if u > 0:
            finish(u - 1, inv)
        inv = jnp.concatenate([softmax(u, qv) for qv in range(n_qv)], axis=1)
    finish(len(items) - 1, inv)


def _attn_bias_kernel(f_ref, o_ref):
    n_var, _, nk, tq = o_ref.shape
    width = f_ref.shape[2]
    x = jnp.broadcast_to(f_ref[0], (nk, width))
    rolled = pltpu.roll(x, 0, 1, stride=1, stride_axis=0)
    t = rolled[:, nk:] * LOG2_E
    key = lax.broadcasted_iota(jnp.int32, (nk, tq), 0)
    kc = key // CHUNK
    qc = lax.broadcasted_iota(jnp.int32, (nk, tq), 1) // CHUNK
    band = (kc >= qc) & (kc <= qc + LEFT_CHUNKS)
    for v in range(n_var):
        ok = band & (key >= (n_var - 1 - v) * tq)
        o_ref[v, 0] = jnp.where(ok, t, -jnp.inf)


def _attn_bias_table(rel_bias):
    heads = rel_bias.shape[0]
    nk = ATTN_K_BLOCKS * ATTN_Q_TILE
    width = nk + ATTN_Q_TILE
    dist = jnp.arange(width) - nk + LEFT_CHUNKS * CHUNK
    f = rel_bias.astype(F32)[:, jnp.clip(dist, -MAX_REL_DIST, MAX_REL_DIST) + MAX_REL_DIST]
    return pl.pallas_call(
        _attn_bias_kernel,
        grid=(heads,),
        in_specs=[pl.BlockSpec((1, 1, width), lambda h: (h, 0, 0))],
        out_specs=pl.BlockSpec((ATTN_K_BLOCKS, 1, nk, ATTN_Q_TILE), lambda h: (0, h, 0, 0)),
        out_shape=jax.ShapeDtypeStruct((ATTN_K_BLOCKS, heads, nk, ATTN_Q_TILE), F32),
        compiler_params=pltpu.CompilerParams(dimension_semantics=("arbitrary",)),
        name="attn_bias",
    )(f.reshape(heads, 1, width))


def _attention(q, k, v, rel_bias, *, batch, seq):
    t, inner = q.shape
    heads = rel_bias.shape[0]
    tq = ATTN_Q_TILE
    rows = ATTN_TILES_PER_STEP * tq
    assert ATTN_TILES_PER_STEP == 2 and ATTN_K_BLOCKS == ATTN_TILES_PER_STEP + 1
    assert seq % rows == 0 and inner == heads * ATTN_HEAD_DIM
    n_s = seq // rows
    bias = _attn_bias_table(rel_bias)
    nk = ATTN_K_BLOCKS * tq

    cur = pl.BlockSpec((rows, inner), lambda j, b: (b * n_s + j, 0))
    prev = pl.BlockSpec((rows, inner), lambda j, b: (b * n_s + jnp.maximum(j - 1, 0), 0))

    def bias_spec(tile):
        return pl.BlockSpec(
            (1, heads, nk, tq),
            lambda j, b: (jnp.minimum(j * ATTN_TILES_PER_STEP + tile, ATTN_K_BLOCKS - 1), 0, 0, 0),
            pipeline_mode=pl.Buffered(1))

    nbytes = (2 * 6 * rows * inner * 2 + ATTN_TILES_PER_STEP * heads * tq * nk * 4
              + ATTN_SLOTS * tq * nk * (4 + 2) + 4 * tq * nk * 4)
    return pl.pallas_call(
        functools.partial(_attn_kernel, heads=heads),
        grid=(n_s, batch),
        in_specs=[cur, prev, cur, prev, cur, bias_spec(0), bias_spec(1)],
        out_specs=cur,
        out_shape=jax.ShapeDtypeStruct((t, inner), BF16),
        scratch_shapes=[pltpu.VMEM((ATTN_SLOTS, nk, tq), F32), pltpu.VMEM((ATTN_SLOTS, nk, tq), BF16)],
        compiler_params=pltpu.CompilerParams(
            dimension_semantics=("arbitrary", "arbitrary"), vmem_limit_bytes=_vmem_limit(nbytes)),
        name="band_attn",
    )(q, k, k, v, v, bias, bias)


def _merge_kernel(x_ref, ys_ref, ya_ref, g_ref, ws_ref, wa_ref, wo_ref, o_ref):
    d = x_ref.shape[1]
    half = x_ref.shape[0] // 2
    halves = (slice(0, half), slice(half, 2 * half))

    def branches(rows):
        return (jnp.dot(ys_ref[rows, :], ws_ref[...], preferred_element_type=F32),
                jnp.dot(ya_ref[rows, :], wa_ref[...], preferred_element_type=F32))

    def gate(rows, bs, ba):
        g = g_ref[rows, :].astype(F32)
        return (jax.nn.sigmoid(g[:, :d]) * bs + jax.nn.sigmoid(g[:, d:]) * ba).astype(BF16)

    def project(rows, merged):
        o_ref[rows, :] = x_ref[rows, :] + jnp.dot(merged, wo_ref[...], preferred_element_type=F32)

    b0 = branches(halves[0])
    b1 = branches(halves[1])
    m0 = gate(halves[0], *b0)
    project(halves[0], m0)
    m1 = gate(halves[1], *b1)
    project(halves[1], m1)


def _merge(x2d, y_ssm, y_attn, g, w_bs, w_ba, w_out):
    t, d = x2d.shape
    tm = FFN_ROW_TILE
    assert t % tm == 0
    row = lambda w: pl.BlockSpec((tm, w), lambda i: (i, 0))
    nbytes = (w_bs.size + w_ba.size + w_out.size) * 2 + 2 * tm * (2 * d * 4 + 4 * d * 2) + 6 * tm * d * 4
    return pl.pallas_call(
        _merge_kernel,
        grid=(t // tm,),
        in_specs=[row(d), row(y_ssm.shape[1]), row(y_attn.shape[1]), row(2 * d),
                  _const_spec(w_bs.shape), _const_spec(w_ba.shape), _const_spec(w_out.shape)],
        out_specs=row(d),
        out_shape=jax.ShapeDtypeStruct((t, d), F32),
        compiler_params=pltpu.CompilerParams(
            dimension_semantics=("arbitrary",), vmem_limit_bytes=_vmem_limit(nbytes)),
        name="merge",
    )(x2d, y_ssm, y_attn, g, w_bs.astype(BF16), w_ba.astype(BF16), w_out.astype(BF16))


def kernel(x, ffn1_norm_w, ffn1_w_gu, ffn1_w_down, mix_norm_w, w_in, conv_w, conv_b, dt_bias, A_log,
           D_skip, ssm_norm_w, rel_bias, w_branch_ssm, w_branch_attn, w_out, ffn2_norm_w, ffn2_w_gu,
           ffn2_w_down, final_norm_w):
    batch, seq, d = x.shape
    depth = ffn1_w_gu.shape[0]
    ssm_heads = A_log.shape[1]
    ssm_inner = ssm_heads * SSM_HEAD_DIM
    conv_dim = conv_w.shape[2]
    attn_inner = rel_bias.shape[1] * ATTN_HEAD_DIM
    xf = x.reshape(batch * seq, d)
    for l in range(depth):
        last = l == depth - 1
        xf = _ffn(xf, ffn1_norm_w[l], ffn1_w_gu[l], ffn1_w_down[l], final_norm_w, final_norm=False)
        z, xbc, dt_raw, q, k, v, g = _in_proj(
            xf, mix_norm_w[l], w_in[l], ssm_inner=ssm_inner, conv_dim=conv_dim,
            ssm_heads=ssm_heads, attn_inner=attn_inner)
        y_ssm = _ssd(xbc, z, dt_raw, conv_w[l], conv_b[l], dt_bias[l], A_log[l], D_skip[l],
                     ssm_norm_w[l], batch=batch, seq=seq)
        y_attn = _attention(q, k, v, rel_bias[l], batch=batch, seq=seq)
        xf = _merge(xf, y_ssm, y_attn, g, w_branch_ssm[l], w_branch_attn[l], w_out[l])
        xf = _ffn(xf, ffn2_norm_w[l], ffn2_w_gu[l], ffn2_w_down[l], final_norm_w, final_norm=last)
    if depth == 0:
        raise ValueError("depth must be >= 1")
    return xf.reshape(batch, seq, d)
```

```python
import functools

import jax
import jax.numpy as jnp
from jax import lax
from jax.experimental import pallas as pl
from jax.experimental.pallas import tpu as pltpu

F32 = jnp.float32
BF16 = jnp.bfloat16

CHUNK = 64
SSM_HEAD_DIM = 64
SSM_GROUPS = 2
SSM_STATE = 128
CONV_WIDTH = 4
ATTN_HEAD_DIM = 64
LEFT_CHUNKS = 8
MAX_REL_DIST = 128
FFN_RES_SCALE = 0.5
RMS_EPS = 1e-6
LOG2_E = 1.4426950408889634

LANES = 128
SUBLANES = 8
BF16_ROWS = 16
MXU_DIM = 256
VMEM_BYTES = 64 * 1024 * 1024

ROW_TILE = 512
FFN_ROW_TILE = 1024
FFN_COL_TILE = 512
SSD_ROW_TILE = 1024
CONV_ROW_BLOCK = 128
ATTN_Q_CHUNKS = 4
ATTN_Q_TILE = ATTN_Q_CHUNKS * CHUNK
ATTN_TILES_PER_STEP = 2
ATTN_SLOTS = 2
ATTN_K_BLOCKS = (LEFT_CHUNKS + ATTN_Q_CHUNKS) * CHUNK // ATTN_Q_TILE


def _vmem_limit(nbytes):
    return int(min(nbytes * 1.25 + (8 << 20), VMEM_BYTES - (6 << 20)))


def _const_spec(shape):
    zeros = (0,) * len(shape)
    return pl.BlockSpec(shape, lambda *_: zeros, pipeline_mode=pl.Buffered(1))


def _rmsnorm(x, w):
    ms = jnp.mean(x * x, axis=-1, keepdims=True)
    return x * lax.rsqrt(ms + RMS_EPS) * w


def _sigmoid(x):
    return 0.5 + 0.5 * jnp.tanh(0.5 * x)


def _silu(x):
    h = 0.5 * x
    return h + h * jnp.tanh(h)


SPLIT_PIECES = 3


def _split3(x):
    hi = x.astype(BF16)
    r1 = x - hi.astype(F32)
    mid = r1.astype(BF16)
    lo = (r1 - mid.astype(F32)).astype(BF16)
    return hi, mid, lo


def _ffn_kernel(x_ref, nw_ref, wgu_ref, wd_ref, fnw_ref, o_ref, h_ref, a_ref, *, d_ff, final_norm):
    half = x_ref.shape[0] // 2
    halves = (slice(0, half), slice(half, 2 * half))
    col_tiles = [(c, min(FFN_COL_TILE, d_ff - c)) for c in range(0, d_ff, FFN_COL_TILE)]

    def prologue(rows):
        h_ref[rows, :] = _rmsnorm(x_ref[rows, :], nw_ref[...]).astype(BF16)

    def gate_up(rows, c, w):
        h = h_ref[rows, :]
        g = jnp.dot(h, wgu_ref[:, c:c + w], preferred_element_type=F32)
        u = jnp.dot(h, wgu_ref[:, d_ff + c:d_ff + c + w], preferred_element_type=F32)
        a_ref[rows, c:c + w] = (_silu(g) * u).astype(BF16)

    def down(rows):
        return jnp.dot(a_ref[rows, :], wd_ref[...], preferred_element_type=F32)

    def epilogue(rows, acc):
        y = x_ref[rows, :] + FFN_RES_SCALE * acc
        if final_norm:
            y = _rmsnorm(y, fnw_ref[...])
        o_ref[rows, :] = y

    prologue(halves[0])
    for i, (c, w) in enumerate(col_tiles):
        gate_up(halves[0], c, w)
        if i == 0:
            prologue(halves[1])
    acc0 = down(halves[0])
    for i, (c, w) in enumerate(col_tiles):
        gate_up(halves[1], c, w)
        if i == 0:
            epilogue(halves[0], acc0)
    epilogue(halves[1], down(halves[1]))


def _ffn(x2d, norm_w, w_gu, w_down, final_norm_w, *, final_norm):
    t, d = x2d.shape
    d_ff = w_down.shape[0]
    assert t % FFN_ROW_TILE == 0 and d_ff % LANES == 0
    tm = FFN_ROW_TILE
    nbytes = (3 * d * d_ff * 2 + 4 * tm * d * 4 + tm * d * 2 + tm * d_ff * 2
              + 2 * tm * FFN_COL_TILE * 4 + 2 * tm * d * 4)
    return pl.pallas_call(
        functools.partial(_ffn_kernel, d_ff=d_ff, final_norm=final_norm),
        grid=(t // tm,),
        in_specs=[
            pl.BlockSpec((tm, d), lambda i: (i, 0)),
            _const_spec((1, d)),
            _const_spec((d, 2 * d_ff)),
            _const_spec((d_ff, d)),
            _const_spec((1, d)),
        ],
        out_specs=pl.BlockSpec((tm, d), lambda i: (i, 0)),
        out_shape=jax.ShapeDtypeStruct((t, d), F32),
        scratch_shapes=[pltpu.VMEM((tm, d), BF16), pltpu.VMEM((tm, d_ff), BF16)],
        compiler_params=pltpu.CompilerParams(
            dimension_semantics=("arbitrary",), vmem_limit_bytes=_vmem_limit(nbytes)),
        name="ffn_final" if final_norm else "ffn",
    )(x2d, norm_w.reshape(1, d), w_gu.astype(BF16), w_down.astype(BF16), final_norm_w.reshape(1, d))


def _inproj_kernel(x_ref, nw_ref, w_ref, z_ref, xbc_ref, q_ref, k_ref, v_ref, g_ref, dt_ref, h_ref,
                   *, q_scale):
    h_ref[...] = _rmsnorm(x_ref[...], nw_ref[...]).astype(BF16)
    col = 0
    for o_ref in (z_ref, xbc_ref, q_ref, k_ref, v_ref, g_ref, dt_ref):
        n = o_ref.shape[1]
        step = min(n, 2 * MXU_DIM)
        for c in range(0, n, step):
            r = jnp.dot(h_ref[...], w_ref[:, col + c:col + c + step], preferred_element_type=F32)
            if o_ref is q_ref:
                r = r * q_scale
            if o_ref is z_ref:
                r = _silu(r)
            o_ref[:, c:c + step] = r.astype(o_ref.dtype)
        col += n


def _in_proj(x2d, norm_w, w_in, *, ssm_inner, conv_dim, ssm_heads, attn_inner):
    t, d = x2d.shape
    tm = ROW_TILE
    dt_lo = ssm_inner + conv_dim
    dt_hi = dt_lo + ssm_heads
    assert dt_lo % LANES == 0
    w_dt = w_in[:, dt_lo:dt_hi]
    w = jnp.concatenate([w_in[:, :dt_lo], w_in[:, dt_hi:]] + [w_dt] * SPLIT_PIECES, axis=1).astype(BF16)
    widths = [ssm_inner, conv_dim, attn_inner, attn_inner, attn_inner, w_in.shape[1] - dt_hi - 3 * attn_inner,
              SPLIT_PIECES * ssm_heads]
    out_dtypes = [BF16, BF16, BF16, BF16, BF16, BF16, F32]
    nbytes = (sum(widths) * d * 2 + 2 * tm * d * 4 + tm * d * 2
              + 2 * sum(tm * wd * jnp.dtype(dt).itemsize for wd, dt in zip(widths, out_dtypes))
              + 2 * tm * 2 * MXU_DIM * 4)
    z, xbc, q, k, v, g, dt_raw = pl.pallas_call(
        functools.partial(_inproj_kernel, q_scale=ATTN_HEAD_DIM ** -0.5 * LOG2_E),
        grid=(t // tm,),
        in_specs=[pl.BlockSpec((tm, d), lambda i: (i, 0)), _const_spec((1, d)),
                  _const_spec((d, sum(widths)))],
        out_specs=[pl.BlockSpec((tm, wd), lambda i: (i, 0)) for wd in widths],
        out_shape=[jax.ShapeDtypeStruct((t, wd), dt) for wd, dt in zip(widths, out_dtypes)],
        scratch_shapes=[pltpu.VMEM((tm, d), BF16)],
        compiler_params=pltpu.CompilerParams(
            dimension_semantics=("arbitrary",), vmem_limit_bytes=_vmem_limit(nbytes)),
        name="in_proj",
    )(x2d, norm_w.reshape(1, d), w)
    return z, xbc, dt_raw, q, k, v, g


def _ssd_kernel(xbc_ref, zg_ref, dtr_ref, convw_ref, convb_ref, dtb_ref, alog_ref, dskip_ref,
                expand_ref, o_ref,
                xpad_ref, xs_ref, bc_ref, dt_ref, state_ref,
                *, n_chunks, ssm_inner, heads):
    s_idx = pl.program_id(1)
    tm = xs_ref.shape[0]
    gn = SSM_GROUPS * SSM_STATE
    gw = ssm_inner // SSM_GROUPS

    halo = xpad_ref.shape[0] - tm

    @pl.when(s_idx == 0)
    def _():
        state_ref[...] = jnp.zeros_like(state_ref)
        xpad_ref[0:halo, :] = jnp.zeros((halo, xpad_ref.shape[1]), BF16)

    @pl.when(s_idx > 0)
    def _():
        xpad_ref[0:halo, :] = xpad_ref[tm:tm + halo, :]

    xpad_ref[halo:, :] = xbc_ref[...]

    blk = CONV_ROW_BLOCK
    srow = lax.broadcasted_iota(jnp.int32, ((CONV_WIDTH - 1) * blk, blk + halo), 0)
    scol = lax.broadcasted_iota(jnp.int32, ((CONV_WIDTH - 1) * blk, blk + halo), 1)
    shift_stack = (scol == (srow % blk) + halo - (srow // blk + 1)).astype(BF16)
    def shifted_rows(r0):
        return jnp.dot(shift_stack, xpad_ref[r0:r0 + blk + halo, :], preferred_element_type=F32)

    shifted_next = shifted_rows(0)
    for r0 in range(0, tm, blk):
        shifted = shifted_next
        if r0 + blk < tm:
            shifted_next = shifted_rows(r0 + blk)
        xblk = xpad_ref[r0 + halo:r0 + halo + blk, :]
        conv = xblk.astype(F32) * convw_ref[CONV_WIDTH - 1:CONV_WIDTH, :] + convb_ref[...]
        for k in range(1, CONV_WIDTH):
            conv = conv + shifted[(k - 1) * blk:k * blk, :] * convw_ref[CONV_WIDTH - 1 - k:CONV_WIDTH - k, :]
        conv = _silu(conv)
        xs_ref[r0:r0 + blk, :] = conv[:, :ssm_inner]
        bc_ref[r0:r0 + blk, :] = conv[:, ssm_inner:]
    dt_ref[...] = jax.nn.softplus(dtr_ref[...] + dtb_ref[...])

    a_head = -jnp.exp(alog_ref[...]) * LOG2_E
    expand = expand_ref[...]
    lane_group = lax.broadcasted_iota(jnp.int32, (CHUNK, SPLIT_PIECES * heads), 1) // heads

    def pieces(x):
        parts = _split3(x)
        out = parts[SPLIT_PIECES - 1]
        for i in range(SPLIT_PIECES - 2, -1, -1):
            out = jnp.where(lane_group == i, parts[i], out)
        return out

    row = lax.broadcasted_iota(jnp.int32, (CHUNK, SPLIT_PIECES * CHUNK), 0)
    col = lax.broadcasted_iota(jnp.int32, (CHUNK, SPLIT_PIECES * CHUNK), 1)
    tril = ((col % CHUNK) <= row).astype(BF16)
    prow = lax.broadcasted_iota(jnp.int32, (CHUNK, LANES), 0)
    plane = lax.broadcasted_iota(jnp.int32, (CHUNK, LANES), 1)
    causal_pair = (plane % SSM_HEAD_DIM) <= prow
    first_head = plane < SSM_HEAD_DIM
    drow = lax.broadcasted_iota(jnp.int32, (CHUNK, ssm_inner), 0)
    dlane = lax.broadcasted_iota(jnp.int32, (CHUNK, ssm_inner), 1)
    diag = (dlane % SSM_HEAD_DIM) == drow

    def decay_stage(c):
        rows = pl.ds(c * CHUNK, CHUNK)
        dt = dt_ref[rows, :]
        bc = bc_ref[rows, :].astype(BF16)
        dt_e = jnp.dot(pieces(dt), expand, preferred_element_type=F32)
        a_parts = jnp.concatenate(_split3(dt * a_head), axis=0)
        a_cum_h = jnp.dot(tril, a_parts, preferred_element_type=F32)
        a_cum = jnp.dot(pieces(a_cum_h), expand, preferred_element_type=F32)
        cbs = []
        for g in range(SSM_GROUPS):
            b_g = bc[:, g * SSM_STATE:(g + 1) * SSM_STATE]
            c_g = bc[:, gn + g * SSM_STATE:gn + (g + 1) * SSM_STATE]
            cbs.append(lax.dot_general(c_g, b_g, (((1,), (1,)), ((), ())),
                                       preferred_element_type=F32))
        return dt_e, a_cum, cbs

    def operand_stage(c, dt_e, a_cum, cbs):
        rows = pl.ds(c * CHUNK, CHUNK)
        xs = xs_ref[rows, :]
        a_last = a_cum[CHUNK - 1:CHUNK, :]
        xdt = xs * dt_e
        xdec = (xdt * jnp.exp2(a_last - a_cum)).astype(BF16)
        xdt_b = xdt.astype(BF16)
        a_t = jnp.sum(jnp.where(diag, a_cum, 0.0), axis=0, keepdims=True)
        ms, rhss = [], []
        for g in range(SSM_GROUPS):
            cb2 = jnp.concatenate([cbs[g], cbs[g]], axis=1)
            for j in range(gw // LANES):
                c0 = g * gw + j * LANES
                seg = a_cum[:, c0:c0 + LANES] - a_t[:, c0:c0 + LANES]
                ldec = jnp.exp2(jnp.where(causal_pair, seg, -jnp.inf))
                ms.append((cb2 * ldec).astype(BF16))
                xp = xdt_b[:, c0:c0 + LANES]
                zero = jnp.zeros_like(xp)
                rhss.append(jnp.concatenate([jnp.where(first_head, xp, zero),
                                             jnp.where(first_head, zero, xp)], axis=0))
        return xdec, ms, rhss

    def matmul_stage(c, a_cum, xdec, ms, rhss):
        rows = pl.ds(c * CHUNK, CHUNK)
        bc = bc_ref[rows, :]
        a_last = a_cum[CHUNK - 1:CHUNK, :]
        y_offs, y_diags = [], []
        for g in range(SSM_GROUPS):
            g0 = g * gw
            b_g = bc[:, g * SSM_STATE:(g + 1) * SSM_STATE]
            c_g = bc[:, gn + g * SSM_STATE:gn + (g + 1) * SSM_STATE].astype(BF16)
            prev = state_ref[:, g0:g0 + gw]
            y_offs.append(jnp.dot(c_g, prev.astype(BF16), preferred_element_type=F32))
            new = jnp.dot(b_g.T.astype(BF16), xdec[:, g0:g0 + gw], preferred_element_type=F32)
            state_ref[:, g0:g0 + gw] = prev * jnp.exp2(a_last[:, g0:g0 + gw]) + new
        for m, rhs in zip(ms, rhss):
            y_diags.append(jnp.dot(m, rhs, preferred_element_type=F32))
        return jnp.concatenate(y_offs, axis=1), jnp.concatenate(y_diags, axis=1)

    def output_stage(c, a_cum, y_off, y_diag):
        rows = pl.ds(c * CHUNK, CHUNK)
        y = y_diag + y_off * jnp.exp2(a_cum) + dskip_ref[...] * xs_ref[rows, :]
        y = y * zg_ref[rows, :].astype(F32)
        outs = []
        for g in range(SSM_GROUPS):
            yg = y[:, g * gw:(g + 1) * gw]
            outs.append(yg * lax.rsqrt(jnp.mean(yg * yg, axis=-1, keepdims=True) + RMS_EPS))
        o_ref[rows, :] = jnp.concatenate(outs, axis=1).astype(o_ref.dtype)

    decays = {c: decay_stage(c) for c in range(min(2, n_chunks))}
    operands = {0: operand_stage(0, *decays[0])}
    for c in range(n_chunks):
        a_cum = decays[c][1]
        y_off, y_diag = matmul_stage(c, a_cum, *operands.pop(c))
        if c + 1 < n_chunks:
            operands[c + 1] = operand_stage(c + 1, *decays[c + 1])
        if c + 2 < n_chunks:
            decays[c + 2] = decay_stage(c + 2)
        output_stage(c, a_cum, y_off, y_diag)
        del decays[c]


def _ssd(xbc, zg, dt_raw, conv_w, conv_b, dt_bias, a_log, d_skip, *, batch, seq):
    t, conv_dim = xbc.shape
    inner = zg.shape[1]
    heads = a_log.shape[0]
    hrep = dt_raw.shape[1]
    tm = SSD_ROW_TILE
    assert seq % tm == 0 and inner == heads * SSM_HEAD_DIM and hrep == SPLIT_PIECES * heads
    n_s = seq // tm
    rep = lambda p: jnp.repeat(p.astype(F32), SSM_HEAD_DIM).reshape(1, inner)
    rep_h = lambda p: jnp.tile(p.astype(F32), SPLIT_PIECES).reshape(1, hrep)
    expand = jnp.tile(jnp.repeat(jnp.eye(heads, dtype=BF16), SSM_HEAD_DIM, axis=1),
                      (SPLIT_PIECES, 1))
    row_map = lambda b, s: (b * n_s + s, 0)
    nbytes = (2 * tm * (conv_dim + 2 * inner) * 2 + 2 * tm * LANES * 4
              + (tm + SUBLANES) * conv_dim * 4 + tm * conv_dim * 4 + tm * LANES * 4
              + SSM_STATE * inner * 4 + 3 * tm * conv_dim * 4 + 24 * CHUNK * inner * 4)
    return pl.pallas_call(
        functools.partial(_ssd_kernel, n_chunks=tm // CHUNK, ssm_inner=inner, heads=heads),
        grid=(batch, n_s),
        in_specs=[
            pl.BlockSpec((tm, conv_dim), row_map),
            pl.BlockSpec((tm, inner), row_map),
            pl.BlockSpec((tm, hrep), row_map),
            _const_spec((CONV_WIDTH, conv_dim)),
            _const_spec((1, conv_dim)),
            _const_spec((1, hrep)),
            _const_spec((1, hrep)),
            _const_spec((1, inner)),
            _const_spec((hrep, inner)),
        ],
        out_specs=pl.BlockSpec((tm, inner), row_map),
        out_shape=jax.ShapeDtypeStruct((t, inner), BF16),
        scratch_shapes=[
            pltpu.VMEM((tm + BF16_ROWS, conv_dim), BF16),
            pltpu.VMEM((tm, inner), F32),
            pltpu.VMEM((tm, conv_dim - inner), F32),
            pltpu.VMEM((tm, hrep), F32),
            pltpu.VMEM((SSM_STATE, inner), F32),
        ],
        compiler_params=pltpu.CompilerParams(
            dimension_semantics=("arbitrary", "arbitrary"), vmem_limit_bytes=_vmem_limit(nbytes)),
        name="ssd",
    )(xbc, zg, dt_raw, conv_w.astype(F32), conv_b.reshape(1, conv_dim).astype(F32),
      rep_h(dt_bias), rep_h(a_log), rep(d_skip), expand)


def _attn_key_chunks(qv):
    q_per_group = LANES // CHUNK
    return range(qv * q_per_group, qv * q_per_group + LEFT_CHUNKS + q_per_group)


def _attn_kernel(q_ref, kprev_ref, kcur_ref, vprev_ref, vcur_ref, bias0_ref, bias1_ref, o_ref,
                 s_ref, p_ref, *, heads):
    tq = s_ref.shape[2]
    n_slots = s_ref.shape[0]
    n_kc = s_ref.shape[1] // CHUNK
    n_qv = tq // LANES
    pair = LANES // ATTN_HEAD_DIM
    step_rows = q_ref.shape[0]
    bias_refs = (bias0_ref, bias1_ref)
    lane_head = lax.broadcasted_iota(jnp.int32, (tq, LANES), 1) // ATTN_HEAD_DIM

    def window(prev_ref, cur_ref, tile, lo):
        first = tile * tq
        last = first + n_kc * CHUNK - step_rows
        return jnp.concatenate([prev_ref[first:step_rows, lo:lo + LANES],
                                cur_ref[0:last, lo:lo + LANES]], axis=0)
    for slot in range(n_slots):
        for qv in range(n_qv):
            for kc in range(n_kc):
                if kc not in _attn_key_chunks(qv):
                    p_ref[slot, kc * CHUNK:(kc + 1) * CHUNK, qv * LANES:(qv + 1) * LANES] = jnp.zeros(
                        (CHUNK, LANES), BF16)

    items = [(tile, h) for tile in range(step_rows // tq) for h in range(heads)]

    def scores(u):
        tile, h = items[u]
        lo = (h // pair) * LANES
        qp = q_ref[tile * tq:(tile + 1) * tq, lo:lo + LANES]
        kp = window(kprev_ref, kcur_ref, tile, lo)
        qm = jnp.where(lane_head == h % pair, qp, jnp.zeros_like(qp))
        s_ref[u % n_slots] = lax.dot_general(kp, qm, (((1,), (1,)), ((), ())),
                                             preferred_element_type=F32)

    def softmax(u, qv):
        tile, h = items[u]
        bias_ref = bias_refs[tile]
        slot = u % n_slots
        cols = slice(qv * LANES, (qv + 1) * LANES)
        mx = None
        for kc in _attn_key_chunks(qv):
            rows = slice(kc * CHUNK, (kc + 1) * CHUNK)
            t = s_ref[slot, rows, cols] + bias_ref[0, h, rows, cols]
            mx = t if mx is None else jnp.maximum(mx, t)
        m = jnp.max(mx, axis=0, keepdims=True)
        acc = None
        for kc in _attn_key_chunks(qv):
            rows = slice(kc * CHUNK, (kc + 1) * CHUNK)
            p = jnp.exp2(s_ref[slot, rows, cols] + bias_ref[0, h, rows, cols] - m)
            acc = p if acc is None else acc + p
            p_ref[slot, rows, cols] = p.astype(BF16)
        return 1.0 / jnp.sum(acc, axis=0, keepdims=True)

    def weighted_values(u, inv):
        tile, h = items[u]
        lo = (h // pair) * LANES
        d0 = (h % pair) * ATTN_HEAD_DIM
        vpt = window(vprev_ref, vcur_ref, tile, lo).T
        o_t = jnp.dot(vpt[d0:d0 + ATTN_HEAD_DIM, :], p_ref[u % n_slots],
                      preferred_element_type=F32)
        return o_t * inv

    outs = []

    def finish(u, inv):
        tile, h = items[u]
        outs.append(weighted_values(u, inv))
        if len(outs) == pair:
            lo = (h // pair) * LANES
            o_ref[tile * tq:(tile + 1) * tq, lo:lo + LANES] = jnp.concatenate(
                outs, axis=0).T.astype(o_ref.dtype)
            outs.clear()

    scores(0)
    inv = None
    for u in range(len(items)):
        if u + 1 < len(items):
            scores(u + 1)
        if u > 0:
            finish(u - 1, inv)
        inv = jnp.concatenate([softmax(u, qv) for qv in range(n_qv)], axis=1)
    finish(len(items) - 1, inv)


def _attn_bias_kernel(f_ref, o_ref):
    n_var, _, nk, tq = o_ref.shape
    width = f_ref.shape[2]
    x = jnp.broadcast_to(f_ref[0], (nk, width))
    rolled = pltpu.roll(x, 0, 1, stride=1, stride_axis=0)
    t = rolled[:, nk:] * LOG2_E
    key = lax.broadcasted_iota(jnp.int32, (nk, tq), 0)
    kc = key // CHUNK
    qc = lax.broadcasted_iota(jnp.int32, (nk, tq), 1) // CHUNK
    band = (kc >= qc) & (kc <= qc + LEFT_CHUNKS)
    for v in range(n_var):
        ok = band & (key >= (n_var - 1 - v) * tq)
        o_ref[v, 0] = jnp.where(ok, t, -jnp.inf)


def _attn_bias_table(rel_bias):
    heads = rel_bias.shape[0]
    nk = ATTN_K_BLOCKS * ATTN_Q_TILE
    width = nk + ATTN_Q_TILE
    dist = jnp.arange(width) - nk + LEFT_CHUNKS * CHUNK
    f = rel_bias.astype(F32)[:, jnp.clip(dist, -MAX_REL_DIST, MAX_REL_DIST) + MAX_REL_DIST]
    return pl.pallas_call(
        _attn_bias_kernel,
        grid=(heads,),
        in_specs=[pl.BlockSpec((1, 1, width), lambda h: (h, 0, 0))],
        out_specs=pl.BlockSpec((ATTN_K_BLOCKS, 1, nk, ATTN_Q_TILE), lambda h: (0, h, 0, 0)),
        out_shape=jax.ShapeDtypeStruct((ATTN_K_BLOCKS, heads, nk, ATTN_Q_TILE), F32),
        compiler_params=pltpu.CompilerParams(dimension_semantics=("arbitrary",)),
        name="attn_bias",
    )(f.reshape(heads, 1, width))


def _attention(q, k, v, rel_bias, *, batch, seq):
    t, inner = q.shape
    heads = rel_bias.shape[0]
    tq = ATTN_Q_TILE
    rows = ATTN_TILES_PER_STEP * tq
    assert ATTN_TILES_PER_STEP == 2 and ATTN_K_BLOCKS == ATTN_TILES_PER_STEP + 1
    assert seq % rows == 0 and inner == heads * ATTN_HEAD_DIM
    n_s = seq // rows
    bias = _attn_bias_table(rel_bias)
    nk = ATTN_K_BLOCKS * tq

    cur = pl.BlockSpec((rows, inner), lambda j, b: (b * n_s + j, 0))
    prev = pl.BlockSpec((rows, inner), lambda j, b: (b * n_s + jnp.maximum(j - 1, 0), 0))

    def bias_spec(tile):
        return pl.BlockSpec(
            (1, heads, nk, tq),
            lambda j, b: (jnp.minimum(j * ATTN_TILES_PER_STEP + tile, ATTN_K_BLOCKS - 1), 0, 0, 0),
            pipeline_mode=pl.Buffered(1))

    nbytes = (2 * 6 * rows * inner * 2 + ATTN_TILES_PER_STEP * heads * tq * nk * 4
              + ATTN_SLOTS * tq * nk * (4 + 2) + 4 * tq * nk * 4)
    return pl.pallas_call(
        functools.partial(_attn_kernel, heads=heads),
        grid=(n_s, batch),
        in_specs=[cur, prev, cur, prev, cur, bias_spec(0), bias_spec(1)],
        out_specs=cur,
        out_shape=jax.ShapeDtypeStruct((t, inner), BF16),
        scratch_shapes=[pltpu.VMEM((ATTN_SLOTS, nk, tq), F32), pltpu.VMEM((ATTN_SLOTS, nk, tq), BF16)],
        compiler_params=pltpu.CompilerParams(
            dimension_semantics=("arbitrary", "arbitrary"), vmem_limit_bytes=_vmem_limit(nbytes)),
        name="band_attn",
    )(q, k, k, v, v, bias, bias)


def _merge_kernel(x_ref, ys_ref, ya_ref, g_ref, ws_ref, wa_ref, wo_ref, o_ref):
    d = x_ref.shape[1]
    half = x_ref.shape[0] // 2
    halves = (slice(0, half), slice(half, 2 * half))

    def branches(rows):
        return (jnp.dot(ys_ref[rows, :], ws_ref[...], preferred_element_type=F32),
                jnp.dot(ya_ref[rows, :], wa_ref[...], preferred_element_type=F32))

    def gate(rows, bs, ba):
        g = g_ref[rows, :].astype(F32)
        return (_sigmoid(g[:, :d]) * bs + _sigmoid(g[:, d:]) * ba).astype(BF16)

    def project(rows, merged):
        o_ref[rows, :] = x_ref[rows, :] + jnp.dot(merged, wo_ref[...], preferred_element_type=F32)

    b0 = branches(halves[0])
    b1 = branches(halves[1])
    m0 = gate(halves[0], *b0)
    project(halves[0], m0)
    m1 = gate(halves[1], *b1)
    project(halves[1], m1)


def _merge(x2d, y_ssm, y_attn, g, w_bs, w_ba, w_out):
    t, d = x2d.shape
    tm = FFN_ROW_TILE
    assert t % tm == 0
    row = lambda w: pl.BlockSpec((tm, w), lambda i: (i, 0))
    nbytes = (w_bs.size + w_ba.size + w_out.size) * 2 + 2 * tm * (2 * d * 4 + 4 * d * 2) + 6 * tm * d * 4
    return pl.pallas_call(
        _merge_kernel,
        grid=(t // tm,),
        in_specs=[row(d), row(y_ssm.shape[1]), row(y_attn.shape[1]), row(2 * d),
                  _const_spec(w_bs.shape), _const_spec(w_ba.shape), _const_spec(w_out.shape)],
        out_specs=row(d),
        out_shape=jax.ShapeDtypeStruct((t, d), F32),
        compiler_params=pltpu.CompilerParams(
            dimension_semantics=("arbitrary",), vmem_limit_bytes=_vmem_limit(nbytes)),
        name="merge",
    )(x2d, y_ssm, y_attn, g, w_bs.astype(BF16), w_ba.astype(BF16), w_out.astype(BF16))


def kernel(x, ffn1_norm_w, ffn1_w_gu, ffn1_w_down, mix_norm_w, w_in, conv_w, conv_b, dt_bias, A_log,
           D_skip, ssm_norm_w, rel_bias, w_branch_ssm, w_branch_attn, w_out, ffn2_norm_w, ffn2_w_gu,
           ffn2_w_down, final_norm_w):
    batch, seq, d = x.shape
    depth = ffn1_w_gu.shape[0]
    assert depth >= 1
    ssm_heads = A_log.shape[1]
    ssm_inner = ssm_heads * SSM_HEAD_DIM
    conv_dim = conv_w.shape[2]
    attn_inner = rel_bias.shape[1] * ATTN_HEAD_DIM
    xf = x.reshape(batch * seq, d)
    for l in range(depth):
        last = l == depth - 1
        xf = _ffn(xf, ffn1_norm_w[l], ffn1_w_gu[l], ffn1_w_down[l], final_norm_w, final_norm=False)
        zg, xbc, dt_raw, q, k, v, g = _in_proj(
            xf, mix_norm_w[l], w_in[l], ssm_inner=ssm_inner, conv_dim=conv_dim,
            ssm_heads=ssm_heads, attn_inner=attn_inner)
        y_ssm = _ssd(xbc, zg, dt_raw, conv_w[l], conv_b[l], dt_bias[l], A_log[l], D_skip[l],
                     batch=batch, seq=seq)
        y_attn = _attention(q, k, v, rel_bias[l], batch=batch, seq=seq)
        w_bs = ssm_norm_w[l].astype(F32)[:, None] * w_branch_ssm[l]
        xf = _merge(xf, y_ssm, y_attn, g, w_bs, w_branch_attn[l], w_out[l])
        xf = _ffn(xf, ffn2_norm_w[l], ffn2_w_gu[l], ffn2_w_down[l], final_norm_w, final_norm=last)
    return xf.reshape(batch, seq, d)
```

```python
import functools

import jax
import jax.numpy as jnp
from jax import lax
from jax.experimental import pallas as pl
from jax.experimental.pallas import tpu as pltpu

F32 = jnp.float32
BF16 = jnp.bfloat16

CHUNK = 64
SSM_HEAD_DIM = 64
SSM_GROUPS = 2
SSM_STATE = 128
CONV_WIDTH = 4
ATTN_HEAD_DIM = 64
LEFT_CHUNKS = 8
MAX_REL_DIST = 128
FFN_RES_SCALE = 0.5
RMS_EPS = 1e-6
LOG2_E = 1.4426950408889634

LANES = 128
SUBLANES = 8
BF16_ROWS = 16
MXU_DIM = 256
VMEM_BYTES = 64 * 1024 * 1024

ROW_TILE = 512
FFN_ROW_TILE = 1024
FFN_COL_TILE = 512
SSD_ROW_TILE = 1024
CONV_ROW_BLOCK = 128
ATTN_Q_CHUNKS = 4
ATTN_Q_TILE = ATTN_Q_CHUNKS * CHUNK
ATTN_TILES_PER_STEP = 2
ATTN_SLOTS = 2
ATTN_K_BLOCKS = (LEFT_CHUNKS + ATTN_Q_CHUNKS) * CHUNK // ATTN_Q_TILE


def _vmem_limit(nbytes):
    return int(min(nbytes * 1.25 + (8 << 20), VMEM_BYTES - (6 << 20)))


def _const_spec(shape):
    zeros = (0,) * len(shape)
    return pl.BlockSpec(shape, lambda *_: zeros, pipeline_mode=pl.Buffered(1))


def _rmsnorm(x, w):
    ms = jnp.mean(x * x, axis=-1, keepdims=True)
    return x * lax.rsqrt(ms + RMS_EPS) * w


def _sigmoid(x):
    return 0.5 + 0.5 * jnp.tanh(0.5 * x)


def _silu(x):
    h = 0.5 * x
    return h + h * jnp.tanh(h)


SPLIT_PIECES = 3


def _split3(x):
    hi = x.astype(BF16)
    r1 = x - hi.astype(F32)
    mid = r1.astype(BF16)
    lo = (r1 - mid.astype(F32)).astype(BF16)
    return hi, mid, lo


def _ffn_kernel(x_ref, nw_ref, wgu_ref, wd_ref, onw_ref, *refs, d_ff, final_norm):
    if final_norm:
        o_ref, h_ref, a_ref = refs
    else:
        o_ref, hn_ref, h_ref, a_ref = refs
    half = x_ref.shape[0] // 2
    halves = (slice(0, half), slice(half, 2 * half))
    col_tiles = [(c, min(FFN_COL_TILE, d_ff - c)) for c in range(0, d_ff, FFN_COL_TILE)]

    def prologue(rows):
        h_ref[rows, :] = _rmsnorm(x_ref[rows, :], nw_ref[...]).astype(BF16)

    def gate_up(rows, c, w):
        h = h_ref[rows, :]
        g = jnp.dot(h, wgu_ref[:, c:c + w], preferred_element_type=F32)
        u = jnp.dot(h, wgu_ref[:, d_ff + c:d_ff + c + w], preferred_element_type=F32)
        a_ref[rows, c:c + w] = (_silu(g) * u).astype(BF16)

    def down(rows):
        return jnp.dot(a_ref[rows, :], wd_ref[...], preferred_element_type=F32)

    def epilogue(rows, acc):
        y = x_ref[rows, :] + FFN_RES_SCALE * acc
        yn = _rmsnorm(y, onw_ref[...])
        if final_norm:
            o_ref[rows, :] = yn
        else:
            o_ref[rows, :] = y
            hn_ref[rows, :] = yn.astype(hn_ref.dtype)

    prologue(halves[0])
    for i, (c, w) in enumerate(col_tiles):
        gate_up(halves[0], c, w)
        if i == 0:
            prologue(halves[1])
    acc0 = down(halves[0])
    for i, (c, w) in enumerate(col_tiles):
        gate_up(halves[1], c, w)
        if i == 0:
            epilogue(halves[0], acc0)
    epilogue(halves[1], down(halves[1]))


def _ffn(x2d, norm_w, w_gu, w_down, out_norm_w, *, final_norm):
    t, d = x2d.shape
    d_ff = w_down.shape[0]
    assert t % FFN_ROW_TILE == 0 and d_ff % LANES == 0
    tm = FFN_ROW_TILE
    nbytes = (3 * d * d_ff * 2 + 4 * tm * d * 4 + 2 * tm * d * 2 + tm * d * 2 + tm * d_ff * 2
              + 2 * tm * FFN_COL_TILE * 4 + 2 * tm * d * 4)
    row = pl.BlockSpec((tm, d), lambda i: (i, 0))
    y_shape = jax.ShapeDtypeStruct((t, d), F32)
    return pl.pallas_call(
        functools.partial(_ffn_kernel, d_ff=d_ff, final_norm=final_norm),
        grid=(t // tm,),
        in_specs=[
            row,
            _const_spec((1, d)),
            _const_spec((d, 2 * d_ff)),
            _const_spec((d_ff, d)),
            _const_spec((1, d)),
        ],
        out_specs=row if final_norm else [row, row],
        out_shape=y_shape if final_norm else [y_shape, jax.ShapeDtypeStruct((t, d), BF16)],
        scratch_shapes=[pltpu.VMEM((tm, d), BF16), pltpu.VMEM((tm, d_ff), BF16)],
        compiler_params=pltpu.CompilerParams(
            dimension_semantics=("arbitrary",), vmem_limit_bytes=_vmem_limit(nbytes)),
        name="ffn_final" if final_norm else "ffn",
    )(x2d, norm_w.reshape(1, d), w_gu.astype(BF16), w_down.astype(BF16), out_norm_w.reshape(1, d))


def _inproj_kernel(h_ref, wa_ref, wb_ref, wdt_ref, z_ref, xbc_ref, q_ref, k_ref, v_ref, g_ref, dt_ref,
                   *, q_scale):
    for w_ref, o_refs in ((wa_ref, (z_ref, xbc_ref)), (wb_ref, (q_ref, k_ref, v_ref, g_ref)),
                          (wdt_ref, (dt_ref,))):
        col = 0
        for o_ref in o_refs:
            n = o_ref.shape[1]
            step = min(n, 2 * MXU_DIM)
            for c in range(0, n, step):
                r = jnp.dot(h_ref[...], w_ref[:, col + c:col + c + step], preferred_element_type=F32)
                if o_ref is q_ref:
                    r = r * q_scale
                if o_ref is z_ref:
                    r = _silu(r)
                o_ref[:, c:c + step] = r.astype(o_ref.dtype)
            col += n


def _in_proj(h2d, w_in, *, ssm_inner, conv_dim, ssm_heads, attn_inner):
    t, d = h2d.shape
    tm = ROW_TILE
    dt_lo = ssm_inner + conv_dim
    dt_hi = dt_lo + ssm_heads
    assert dt_lo % LANES == 0
    ws = [w_in[:, :dt_lo].astype(BF16), w_in[:, dt_hi:].astype(BF16),
          jnp.tile(w_in[:, dt_lo:dt_hi], (1, SPLIT_PIECES)).astype(BF16)]
    widths = [ssm_inner, conv_dim, attn_inner, attn_inner, attn_inner, w_in.shape[1] - dt_hi - 3 * attn_inner,
              SPLIT_PIECES * ssm_heads]
    out_dtypes = [BF16, BF16, BF16, BF16, BF16, BF16, F32]
    nbytes = (sum(widths) * d * 2 + 2 * tm * d * 2
              + 2 * sum(tm * wd * jnp.dtype(dt).itemsize for wd, dt in zip(widths, out_dtypes))
              + 2 * tm * 2 * MXU_DIM * 4)
    z, xbc, q, k, v, g, dt_raw = pl.pallas_call(
        functools.partial(_inproj_kernel, q_scale=ATTN_HEAD_DIM ** -0.5 * LOG2_E),
        grid=(t // tm,),
        in_specs=[pl.BlockSpec((tm, d), lambda i: (i, 0))] + [_const_spec(w.shape) for w in ws],
        out_specs=[pl.BlockSpec((tm, wd), lambda i: (i, 0)) for wd in widths],
        out_shape=[jax.ShapeDtypeStruct((t, wd), dt) for wd, dt in zip(widths, out_dtypes)],
        compiler_params=pltpu.CompilerParams(
            dimension_semantics=("arbitrary",), vmem_limit_bytes=_vmem_limit(nbytes)),
        name="in_proj",
    )(h2d, *ws)
    return z, xbc, dt_raw, q, k, v, g


def _ssd_kernel(xbc_ref, zg_ref, dtr_ref, convw_ref, convb_ref, dtb_ref, alog_ref, dskip_ref,
                expand_ref, o_ref,
                xpad_ref, xs_ref, bc_ref, dt_ref, state_ref,
                *, n_chunks, ssm_inner, heads):
    s_idx = pl.program_id(1)
    tm = xs_ref.shape[0]
    gn = SSM_GROUPS * SSM_STATE
    gw = ssm_inner // SSM_GROUPS

    halo = xpad_ref.shape[0] - tm

    @pl.when(s_idx == 0)
    def _():
        state_ref[...] = jnp.zeros_like(state_ref)
        xpad_ref[0:halo, :] = jnp.zeros((halo, xpad_ref.shape[1]), BF16)

    @pl.when(s_idx > 0)
    def _():
        xpad_ref[0:halo, :] = xpad_ref[tm:tm + halo, :]

    xpad_ref[halo:, :] = xbc_ref[...]

    blk = CONV_ROW_BLOCK
    srow = lax.broadcasted_iota(jnp.int32, ((CONV_WIDTH - 1) * blk, blk + halo), 0)
    scol = lax.broadcasted_iota(jnp.int32, ((CONV_WIDTH - 1) * blk, blk + halo), 1)
    shift_stack = (scol == (srow % blk) + halo - (srow // blk + 1)).astype(BF16)
    def shifted_rows(r0):
        return jnp.dot(shift_stack, xpad_ref[r0:r0 + blk + halo, :], preferred_element_type=F32)

    shifted_next = shifted_rows(0)
    for r0 in range(0, tm, blk):
        shifted = shifted_next
        if r0 + blk < tm:
            shifted_next = shifted_rows(r0 + blk)
        xblk = xpad_ref[r0 + halo:r0 + halo + blk, :]
        conv = xblk.astype(F32) * convw_ref[CONV_WIDTH - 1:CONV_WIDTH, :] + convb_ref[...]
        for k in range(1, CONV_WIDTH):
            conv = conv + shifted[(k - 1) * blk:k * blk, :] * convw_ref[CONV_WIDTH - 1 - k:CONV_WIDTH - k, :]
        conv = _silu(conv)
        xs_ref[r0:r0 + blk, :] = conv[:, :ssm_inner]
        bc_ref[r0:r0 + blk, :] = conv[:, ssm_inner:]
    dt_ref[...] = jax.nn.softplus(dtr_ref[...] + dtb_ref[...])

    a_head = -jnp.exp(alog_ref[...]) * LOG2_E
    expand = expand_ref[...]
    lane_group = lax.broadcasted_iota(jnp.int32, (CHUNK, SPLIT_PIECES * heads), 1) // heads

    def pieces(x):
        parts = _split3(x)
        out = parts[SPLIT_PIECES - 1]
        for i in range(SPLIT_PIECES - 2, -1, -1):
            out = jnp.where(lane_group == i, parts[i], out)
        return out

    row = lax.broadcasted_iota(jnp.int32, (CHUNK, SPLIT_PIECES * CHUNK), 0)
    col = lax.broadcasted_iota(jnp.int32, (CHUNK, SPLIT_PIECES * CHUNK), 1)
    tril = ((col % CHUNK) <= row).astype(BF16)
    prow = lax.broadcasted_iota(jnp.int32, (CHUNK, LANES), 0)
    plane = lax.broadcasted_iota(jnp.int32, (CHUNK, LANES), 1)
    causal_pair = (plane % SSM_HEAD_DIM) <= prow
    first_head = plane < SSM_HEAD_DIM
    drow = lax.broadcasted_iota(jnp.int32, (CHUNK, ssm_inner), 0)
    dlane = lax.broadcasted_iota(jnp.int32, (CHUNK, ssm_inner), 1)
    diag = (dlane % SSM_HEAD_DIM) == drow

    def decay_stage(c):
        rows = pl.ds(c * CHUNK, CHUNK)
        dt = dt_ref[rows, :]
        bc = bc_ref[rows, :].astype(BF16)
        dt_e = jnp.dot(pieces(dt), expand, preferred_element_type=F32)
        a_parts = jnp.concatenate(_split3(dt * a_head), axis=0)
        a_cum_h = jnp.dot(tril, a_parts, preferred_element_type=F32)
        a_cum = jnp.dot(pieces(a_cum_h), expand, preferred_element_type=F32)
        cbs = []
        for g in range(SSM_GROUPS):
            b_g = bc[:, g * SSM_STATE:(g + 1) * SSM_STATE]
            c_g = bc[:, gn + g * SSM_STATE:gn + (g + 1) * SSM_STATE]
            cbs.append(lax.dot_general(c_g, b_g, (((1,), (1,)), ((), ())),
                                       preferred_element_type=F32))
        return dt_e, a_cum, cbs

    def operand_stage(c, dt_e, a_cum, cbs):
        rows = pl.ds(c * CHUNK, CHUNK)
        xs = xs_ref[rows, :]
        a_last = a_cum[CHUNK - 1:CHUNK, :]
        xdt = xs * dt_e
        xdec = (xdt * jnp.exp2(a_last - a_cum)).astype(BF16)
        xdt_b = xdt.astype(BF16)
        a_t = jnp.sum(jnp.where(diag, a_cum, 0.0), axis=0, keepdims=True)
        ms, rhss = [], []
        for g in range(SSM_GROUPS):
            cb2 = jnp.concatenate([cbs[g], cbs[g]], axis=1)
            for j in range(gw // LANES):
                c0 = g * gw + j * LANES
                seg = a_cum[:, c0:c0 + LANES] - a_t[:, c0:c0 + LANES]
                ldec = jnp.exp2(jnp.where(causal_pair, seg, -jnp.inf))
                ms.append((cb2 * ldec).astype(BF16))
                xp = xdt_b[:, c0:c0 + LANES]
                zero = jnp.zeros_like(xp)
                rhss.append(jnp.concatenate([jnp.where(first_head, xp, zero),
                                             jnp.where(first_head, zero, xp)], axis=0))
        return xdec, ms, rhss

    def matmul_stage(c, a_cum, xdec, ms, rhss):
        rows = pl.ds(c * CHUNK, CHUNK)
        bc = bc_ref[rows, :]
        a_last = a_cum[CHUNK - 1:CHUNK, :]
        y_offs, y_diags = [], []
        for g in range(SSM_GROUPS):
            g0 = g * gw
            b_g = bc[:, g * SSM_STATE:(g + 1) * SSM_STATE]
            c_g = bc[:, gn + g * SSM_STATE:gn + (g + 1) * SSM_STATE].astype(BF16)
            prev = state_ref[:, g0:g0 + gw]
            y_offs.append(jnp.dot(c_g, prev.astype(BF16), preferred_element_type=F32))
            new = jnp.dot(b_g.T.astype(BF16), xdec[:, g0:g0 + gw], preferred_element_type=F32)
            state_ref[:, g0:g0 + gw] = prev * jnp.exp2(a_last[:, g0:g0 + gw]) + new
        for m, rhs in zip(ms, rhss):
            y_diags.append(jnp.dot(m, rhs, preferred_element_type=F32))
        return jnp.concatenate(y_offs, axis=1), jnp.concatenate(y_diags, axis=1)

    def output_stage(c, a_cum, y_off, y_diag):
        rows = pl.ds(c * CHUNK, CHUNK)
        y = y_diag + y_off * jnp.exp2(a_cum) + dskip_ref[...] * xs_ref[rows, :]
        y = y * zg_ref[rows, :].astype(F32)
        outs = []
        for g in range(SSM_GROUPS):
            yg = y[:, g * gw:(g + 1) * gw]
            outs.append(yg * lax.rsqrt(jnp.mean(yg * yg, axis=-1, keepdims=True) + RMS_EPS))
        o_ref[rows, :] = jnp.concatenate(outs, axis=1).astype(o_ref.dtype)

    decays = {c: decay_stage(c) for c in range(min(2, n_chunks))}
    operands = {0: operand_stage(0, *decays[0])}
    for c in range(n_chunks):
        a_cum = decays[c][1]
        y_off, y_diag = matmul_stage(c, a_cum, *operands.pop(c))
        if c + 1 < n_chunks:
            operands[c + 1] = operand_stage(c + 1, *decays[c + 1])
        if c + 2 < n_chunks:
            decays[c + 2] = decay_stage(c + 2)
        output_stage(c, a_cum, y_off, y_diag)
        del decays[c]


def _ssd(xbc, zg, dt_raw, conv_w, conv_b, dt_bias, a_log, d_skip, *, batch, seq):
    t, conv_dim = xbc.shape
    inner = zg.shape[1]
    heads = a_log.shape[0]
    hrep = dt_raw.shape[1]
    tm = SSD_ROW_TILE
    assert seq % tm == 0 and inner == heads * SSM_HEAD_DIM and hrep == SPLIT_PIECES * heads
    n_s = seq // tm
    rep = lambda p: jnp.repeat(p.astype(F32), SSM_HEAD_DIM).reshape(1, inner)
    rep_h = lambda p: jnp.tile(p.astype(F32), SPLIT_PIECES).reshape(1, hrep)
    expand = jnp.tile(jnp.repeat(jnp.eye(heads, dtype=BF16), SSM_HEAD_DIM, axis=1),
                      (SPLIT_PIECES, 1))
    row_map = lambda b, s: (b * n_s + s, 0)
    nbytes = (2 * tm * (conv_dim + 2 * inner) * 2 + 2 * tm * LANES * 4
              + (tm + SUBLANES) * conv_dim * 4 + tm * conv_dim * 4 + tm * LANES * 4
              + SSM_STATE * inner * 4 + 3 * tm * conv_dim * 4 + 24 * CHUNK * inner * 4)
    return pl.pallas_call(
        functools.partial(_ssd_kernel, n_chunks=tm // CHUNK, ssm_inner=inner, heads=heads),
        grid=(batch, n_s),
        in_specs=[
            pl.BlockSpec((tm, conv_dim), row_map),
            pl.BlockSpec((tm, inner), row_map),
            pl.BlockSpec((tm, hrep), row_map),
            _const_spec((CONV_WIDTH, conv_dim)),
            _const_spec((1, conv_dim)),
            _const_spec((1, hrep)),
            _const_spec((1, hrep)),
            _const_spec((1, inner)),
            _const_spec((hrep, inner)),
        ],
        out_specs=pl.BlockSpec((tm, inner), row_map),
        out_shape=jax.ShapeDtypeStruct((t, inner), BF16),
        scratch_shapes=[
            pltpu.VMEM((tm + BF16_ROWS, conv_dim), BF16),
            pltpu.VMEM((tm, inner), F32),
            pltpu.VMEM((tm, conv_dim - inner), F32),
            pltpu.VMEM((tm, hrep), F32),
            pltpu.VMEM((SSM_STATE, inner), F32),
        ],
        compiler_params=pltpu.CompilerParams(
            dimension_semantics=("arbitrary", "arbitrary"), vmem_limit_bytes=_vmem_limit(nbytes)),
        name="ssd",
    )(xbc, zg, dt_raw, conv_w.astype(F32), conv_b.reshape(1, conv_dim).astype(F32),
      rep_h(dt_bias), rep_h(a_log), rep(d_skip), expand)


def _attn_key_chunks(qv):
    q_per_group = LANES // CHUNK
    return range(qv * q_per_group, qv * q_per_group + LEFT_CHUNKS + q_per_group)


def _attn_kernel(q_ref, kprev_ref, kcur_ref, vprev_ref, vcur_ref, bias0_ref, bias1_ref, o_ref,
                 s_ref, p_ref, *, heads):
    tq = s_ref.shape[2]
    n_slots = s_ref.shape[0]
    n_kc = s_ref.shape[1] // CHUNK
    n_qv = tq // LANES
    pair = LANES // ATTN_HEAD_DIM
    step_rows = q_ref.shape[0]
    bias_refs = (bias0_ref, bias1_ref)
    lane_head = lax.broadcasted_iota(jnp.int32, (tq, LANES), 1) // ATTN_HEAD_DIM

    def window(prev_ref, cur_ref, tile, lo):
        first = tile * tq
        last = first + n_kc * CHUNK - step_rows
        return jnp.concatenate([prev_ref[first:step_rows, lo:lo + LANES],
                                cur_ref[0:last, lo:lo + LANES]], axis=0)
    for slot in range(n_slots):
        for qv in range(n_qv):
            for kc in range(n_kc):
                if kc not in _attn_key_chunks(qv):
                    p_ref[slot, kc * CHUNK:(kc + 1) * CHUNK, qv * LANES:(qv + 1) * LANES] = jnp.zeros(
                        (CHUNK, LANES), BF16)

    items = [(tile, h) for tile in range(step_rows // tq) for h in range(heads)]

    def scores(u):
        tile, h = items[u]
        lo = (h // pair) * LANES
        qp = q_ref[tile * tq:(tile + 1) * tq, lo:lo + LANES]
        kp = window(kprev_ref, kcur_ref, tile, lo)
        qm = jnp.where(lane_head == h % pair, qp, jnp.zeros_like(qp))
        s_ref[u % n_slots] = lax.dot_general(kp, qm, (((1,), (1,)), ((), ())),
                                             preferred_element_type=F32)

    def softmax(u, qv):
        tile, h = items[u]
        bias_ref = bias_refs[tile]
        slot = u % n_slots
        cols = slice(qv * LANES, (qv + 1) * LANES)
        mx = None
        for kc in _attn_key_chunks(qv):
            rows = slice(kc * CHUNK, (kc + 1) * CHUNK)
            t = s_ref[slot, rows, cols] + bias_ref[0, h, rows, cols]
            mx = t if mx is None else jnp.maximum(mx, t)
        m = jnp.max(mx, axis=0, keepdims=True)
        acc = None
        for kc in _attn_key_chunks(qv):
            rows = slice(kc * CHUNK, (kc + 1) * CHUNK)
            p = jnp.exp2(s_ref[slot, rows, cols] + bias_ref[0, h, rows, cols] - m)
            acc = p if acc is None else acc + p
            p_ref[slot, rows, cols] = p.astype(BF16)
        return 1.0 / jnp.sum(acc, axis=0, keepdims=True)

    def weighted_values(u, inv):
        tile, h = items[u]
        lo = (h // pair) * LANES
        d0 = (h % pair) * ATTN_HEAD_DIM
        vpt = window(vprev_ref, vcur_ref, tile, lo).T
        o_t = jnp.dot(vpt[d0:d0 + ATTN_HEAD_DIM, :], p_ref[u % n_slots],
                      preferred_element_type=F32)
        return o_t * inv

    outs = []

    def finish(u, inv):
        tile, h = items[u]
        outs.append(weighted_values(u, inv))
        if len(outs) == pair:
            lo = (h // pair) * LANES
            o_ref[tile * tq:(tile + 1) * tq, lo:lo + LANES] = jnp.concatenate(
                outs, axis=0).T.astype(o_ref.dtype)
            outs.clear()

    scores(0)
    inv = None
    for u in range(len(items)):
        if u + 1 < len(items):
            scores(u + 1)
        if u > 0:
            finish(u - 1, inv)
        inv = jnp.concatenate([softmax(u, qv) for qv in range(n_qv)], axis=1)
    finish(len(items) - 1, inv)


def _attn_bias_kernel(f_ref, o_ref):
    n_var, _, nk, tq = o_ref.shape
    width = f_ref.shape[2]
    x = jnp.broadcast_to(f_ref[0], (nk, width))
    rolled = pltpu.roll(x, 0, 1, stride=1, stride_axis=0)
    t = rolled[:, nk:] * LOG2_E
    key = lax.broadcasted_iota(jnp.int32, (nk, tq), 0)
    kc = key // CHUNK
    qc = lax.broadcasted_iota(jnp.int32, (nk, tq), 1) // CHUNK
    band = (kc >= qc) & (kc <= qc + LEFT_CHUNKS)
    for v in range(n_var):
        ok = band & (key >= (n_var - 1 - v) * tq)
        o_ref[v, 0] = jnp.where(ok, t, -jnp.inf)


def _attn_bias_table(rel_bias):
    heads = rel_bias.shape[0]
    nk = ATTN_K_BLOCKS * ATTN_Q_TILE
    width = nk + ATTN_Q_TILE
    dist = jnp.arange(width) - nk + LEFT_CHUNKS * CHUNK
    f = rel_bias.astype(F32)[:, jnp.clip(dist, -MAX_REL_DIST, MAX_REL_DIST) + MAX_REL_DIST]
    return pl.pallas_call(
        _attn_bias_kernel,
        grid=(heads,),
        in_specs=[pl.BlockSpec((1, 1, width), lambda h: (h, 0, 0))],
        out_specs=pl.BlockSpec((ATTN_K_BLOCKS, 1, nk, ATTN_Q_TILE), lambda h: (0, h, 0, 0)),
        out_shape=jax.ShapeDtypeStruct((ATTN_K_BLOCKS, heads, nk, ATTN_Q_TILE), F32),
        compiler_params=pltpu.CompilerParams(dimension_semantics=("arbitrary",)),
        name="attn_bias",
    )(f.reshape(heads, 1, width))


def _attention(q, k, v, rel_bias, *, batch, seq):
    t, inner = q.shape
    heads = rel_bias.shape[0]
    tq = ATTN_Q_TILE
    rows = ATTN_TILES_PER_STEP * tq
    assert ATTN_TILES_PER_STEP == 2 and ATTN_K_BLOCKS == ATTN_TILES_PER_STEP + 1
    assert seq % rows == 0 and inner == heads * ATTN_HEAD_DIM
    n_s = seq // rows
    bias = _attn_bias_table(rel_bias)
    nk = ATTN_K_BLOCKS * tq

    cur = pl.BlockSpec((rows, inner), lambda j, b: (b * n_s + j, 0))
    prev = pl.BlockSpec((rows, inner), lambda j, b: (b * n_s + jnp.maximum(j - 1, 0), 0))

    def bias_spec(tile):
        return pl.BlockSpec(
            (1, heads, nk, tq),
            lambda j, b: (jnp.minimum(j * ATTN_TILES_PER_STEP + tile, ATTN_K_BLOCKS - 1), 0, 0, 0),
            pipeline_mode=pl.Buffered(1))

    nbytes = (2 * 6 * rows * inner * 2 + ATTN_TILES_PER_STEP * heads * tq * nk * 4
              + ATTN_SLOTS * tq * nk * (4 + 2) + 4 * tq * nk * 4)
    return pl.pallas_call(
        functools.partial(_attn_kernel, heads=heads),
        grid=(n_s, batch),
        in_specs=[cur, prev, cur, prev, cur, bias_spec(0), bias_spec(1)],
        out_specs=cur,
        out_shape=jax.ShapeDtypeStruct((t, inner), BF16),
        scratch_shapes=[pltpu.VMEM((ATTN_SLOTS, nk, tq), F32), pltpu.VMEM((ATTN_SLOTS, nk, tq), BF16)],
        compiler_params=pltpu.CompilerParams(
            dimension_semantics=("arbitrary", "arbitrary"), vmem_limit_bytes=_vmem_limit(nbytes)),
        name="band_attn",
    )(q, k, k, v, v, bias, bias)


def _merge_kernel(x_ref, ys_ref, ya_ref, g_ref, ws_ref, wa_ref, wo_ref, o_ref):
    d = x_ref.shape[1]
    half = x_ref.shape[0] // 2
    halves = (slice(0, half), slice(half, 2 * half))

    def branches(rows):
        return (jnp.dot(ys_ref[rows, :], ws_ref[...], preferred_element_type=F32),
                jnp.dot(ya_ref[rows, :], wa_ref[...], preferred_element_type=F32))

    def gate(rows, bs, ba):
        g = g_ref[rows, :].astype(F32)
        return (_sigmoid(g[:, :d]) * bs + _sigmoid(g[:, d:]) * ba).astype(BF16)

    def project(rows, merged):
        o_ref[rows, :] = x_ref[rows, :] + jnp.dot(merged, wo_ref[...], preferred_element_type=F32)

    b0 = branches(halves[0])
    b1 = branches(halves[1])
    m0 = gate(halves[0], *b0)
    project(halves[0], m0)
    m1 = gate(halves[1], *b1)
    project(halves[1], m1)


def _merge(x2d, y_ssm, y_attn, g, w_bs, w_ba, w_out):
    t, d = x2d.shape
    tm = FFN_ROW_TILE
    assert t % tm == 0
    row = lambda w: pl.BlockSpec((tm, w), lambda i: (i, 0))
    nbytes = (w_bs.size + w_ba.size + w_out.size) * 2 + 2 * tm * (2 * d * 4 + 4 * d * 2) + 6 * tm * d * 4
    return pl.pallas_call(
        _merge_kernel,
        grid=(t // tm,),
        in_specs=[row(d), row(y_ssm.shape[1]), row(y_attn.shape[1]), row(2 * d),
                  _const_spec(w_bs.shape), _const_spec(w_ba.shape), _const_spec(w_out.shape)],
        out_specs=row(d),
        out_shape=jax.ShapeDtypeStruct((t, d), F32),
        compiler_params=pltpu.CompilerParams(
            dimension_semantics=("arbitrary",), vmem_limit_bytes=_vmem_limit(nbytes)),
        name="merge",
    )(x2d, y_ssm, y_attn, g, w_bs.astype(BF16), w_ba.astype(BF16), w_out.astype(BF16))


def kernel(x, ffn1_norm_w, ffn1_w_gu, ffn1_w_down, mix_norm_w, w_in, conv_w, conv_b, dt_bias, A_log,
           D_skip, ssm_norm_w, rel_bias, w_branch_ssm, w_branch_attn, w_out, ffn2_norm_w, ffn2_w_gu,
           ffn2_w_down, final_norm_w):
    batch, seq, d = x.shape
    depth = ffn1_w_gu.shape[0]
    assert depth >= 1
    ssm_heads = A_log.shape[1]
    ssm_inner = ssm_heads * SSM_HEAD_DIM
    conv_dim = conv_w.shape[2]
    attn_inner = rel_bias.shape[1] * ATTN_HEAD_DIM
    xf = x.reshape(batch * seq, d)
    for l in range(depth):
        last = l == depth - 1
        xf, h_mix = _ffn(xf, ffn1_norm_w[l], ffn1_w_gu[l], ffn1_w_down[l], mix_norm_w[l], final_norm=False)
        zg, xbc, dt_raw, q, k, v, g = _in_proj(
            h_mix, w_in[l], ssm_inner=ssm_inner, conv_dim=conv_dim,
            ssm_heads=ssm_heads, attn_inner=attn_inner)
        y_ssm = _ssd(xbc, zg, dt_raw, conv_w[l], conv_b[l], dt_bias[l], A_log[l], D_skip[l],
                     batch=batch, seq=seq)
        y_attn = _attention(q, k, v, rel_bias[l], batch=batch, seq=seq)
        w_bs = ssm_norm_w[l].astype(F32)[:, None] * w_branch_ssm[l]
        xf = _merge(xf, y_ssm, y_attn, g, w_bs, w_branch_attn[l], w_out[l])
        if last:
            xf = _ffn(xf, ffn2_norm_w[l], ffn2_w_gu[l], ffn2_w_down[l], final_norm_w, final_norm=True)
        else:
            xf, _ = _ffn(xf, ffn2_norm_w[l], ffn2_w_gu[l], ffn2_w_down[l], ffn1_norm_w[l + 1],
                         final_norm=False)
    return xf.reshape(batch, seq, d)
```

```python
import functools

import jax
import jax.numpy as jnp
from jax import lax
from jax.experimental import pallas as pl
from jax.experimental.pallas import tpu as pltpu

F32 = jnp.float32
BF16 = jnp.bfloat16

CHUNK = 64
SSM_HEAD_DIM = 64
SSM_GROUPS = 2
SSM_STATE = 128
CONV_WIDTH = 4
ATTN_HEAD_DIM = 64
LEFT_CHUNKS = 8
MAX_REL_DIST = 128
FFN_RES_SCALE = 0.5
RMS_EPS = 1e-6
LOG2_E = 1.4426950408889634

LANES = 128
SUBLANES = 8
BF16_ROWS = 16
MXU_DIM = 256
VMEM_BYTES = 64 * 1024 * 1024

ROW_TILE = 512
FFN_ROW_TILE = 1024
FFN_COL_TILE = 512
SSD_ROW_TILE = 1024
CONV_ROW_BLOCK = 128
ATTN_Q_CHUNKS = 4
ATTN_Q_TILE = ATTN_Q_CHUNKS * CHUNK
ATTN_TILES_PER_STEP = 2
ATTN_SLOTS = 2
ATTN_K_BLOCKS = (LEFT_CHUNKS + ATTN_Q_CHUNKS) * CHUNK // ATTN_Q_TILE


def _vmem_limit(nbytes):
    return int(min(nbytes * 1.25 + (8 << 20), VMEM_BYTES - (6 << 20)))


def _const_spec(shape):
    zeros = (0,) * len(shape)
    return pl.BlockSpec(shape, lambda *_: zeros, pipeline_mode=pl.Buffered(1))


def _rmsnorm(x, w):
    ms = jnp.mean(x * x, axis=-1, keepdims=True)
    return x * lax.rsqrt(ms + RMS_EPS) * w


def _sigmoid(x):
    return 0.5 + 0.5 * jnp.tanh(0.5 * x)


def _silu(x):
    h = 0.5 * x
    return h + h * jnp.tanh(h)


SPLIT_PIECES = 3


def _split3(x):
    hi = x.astype(BF16)
    r1 = x - hi.astype(F32)
    mid = r1.astype(BF16)
    lo = (r1 - mid.astype(F32)).astype(BF16)
    return hi, mid, lo


def _ffn_kernel(x_ref, nw_ref, wgu_ref, wd_ref, fnw_ref, o_ref, h_ref, a_ref, *, d_ff, final_norm):
    half = x_ref.shape[0] // 2
    halves = (slice(0, half), slice(half, 2 * half))
    col_tiles = [(c, min(FFN_COL_TILE, d_ff - c)) for c in range(0, d_ff, FFN_COL_TILE)]

    def prologue(rows):
        h_ref[rows, :] = _rmsnorm(x_ref[rows, :], nw_ref[...]).astype(BF16)

    def gate_up(rows, c, w):
        h = h_ref[rows, :]
        g = jnp.dot(h, wgu_ref[:, c:c + w], preferred_element_type=F32)
        u = jnp.dot(h, wgu_ref[:, d_ff + c:d_ff + c + w], preferred_element_type=F32)
        a_ref[rows, c:c + w] = (_silu(g) * u).astype(BF16)

    def down(rows):
        return jnp.dot(a_ref[rows, :], wd_ref[...], preferred_element_type=F32)

    def epilogue(rows, acc):
        y = x_ref[rows, :] + FFN_RES_SCALE * acc
        if final_norm:
            y = _rmsnorm(y, fnw_ref[...])
        o_ref[rows, :] = y

    prologue(halves[0])
    for i, (c, w) in enumerate(col_tiles):
        gate_up(halves[0], c, w)
        if i == 0:
            prologue(halves[1])
    acc0 = down(halves[0])
    for i, (c, w) in enumerate(col_tiles):
        gate_up(halves[1], c, w)
        if i == 0:
            epilogue(halves[0], acc0)
    epilogue(halves[1], down(halves[1]))


def _ffn(x2d, norm_w, w_gu, w_down, final_norm_w, *, final_norm):
    t, d = x2d.shape
    d_ff = w_down.shape[0]
    assert t % FFN_ROW_TILE == 0 and d_ff % LANES == 0
    tm = FFN_ROW_TILE
    nbytes = (3 * d * d_ff * 2 + 4 * tm * d * 4 + tm * d * 2 + tm * d_ff * 2
              + 2 * tm * FFN_COL_TILE * 4 + 2 * tm * d * 4)
    return pl.pallas_call(
        functools.partial(_ffn_kernel, d_ff=d_ff, final_norm=final_norm),
        grid=(t // tm,),
        in_specs=[
            pl.BlockSpec((tm, d), lambda i: (i, 0)),
            _const_spec((1, d)),
            _const_spec((d, 2 * d_ff)),
            _const_spec((d_ff, d)),
            _const_spec((1, d)),
        ],
        out_specs=pl.BlockSpec((tm, d), lambda i: (i, 0)),
        out_shape=jax.ShapeDtypeStruct((t, d), F32),
        scratch_shapes=[pltpu.VMEM((tm, d), BF16), pltpu.VMEM((tm, d_ff), BF16)],
        compiler_params=pltpu.CompilerParams(
            dimension_semantics=("arbitrary",), vmem_limit_bytes=_vmem_limit(nbytes)),
        name="ffn_final" if final_norm else "ffn",
    )(x2d, norm_w.reshape(1, d), w_gu.astype(BF16), w_down.astype(BF16), final_norm_w.reshape(1, d))


def _inproj_kernel(x_ref, nw_ref, wa_ref, wb_ref, wdt_ref, z_ref, xbc_ref, q_ref, k_ref, v_ref, g_ref,
                   dt_ref, h_ref, *, q_scale):
    h_ref[...] = _rmsnorm(x_ref[...], nw_ref[...]).astype(BF16)
    for w_ref, o_refs in ((wa_ref, (z_ref, xbc_ref)), (wb_ref, (q_ref, k_ref, v_ref, g_ref)),
                          (wdt_ref, (dt_ref,))):
        col = 0
        for o_ref in o_refs:
            n = o_ref.shape[1]
            step = min(n, 2 * MXU_DIM)
            for c in range(0, n, step):
                r = jnp.dot(h_ref[...], w_ref[:, col + c:col + c + step], preferred_element_type=F32)
                if o_ref is q_ref:
                    r = r * q_scale
                if o_ref is z_ref:
                    r = _silu(r)
                o_ref[:, c:c + step] = r.astype(o_ref.dtype)
            col += n


def _in_proj(x2d, norm_w, w_in, *, ssm_inner, conv_dim, ssm_heads, attn_inner):
    t, d = x2d.shape
    tm = ROW_TILE
    dt_lo = ssm_inner + conv_dim
    dt_hi = dt_lo + ssm_heads
    assert dt_lo % LANES == 0
    ws = [w_in[:, :dt_lo].astype(BF16), w_in[:, dt_hi:].astype(BF16),
          jnp.tile(w_in[:, dt_lo:dt_hi], (1, SPLIT_PIECES)).astype(BF16)]
    widths = [ssm_inner, conv_dim, attn_inner, attn_inner, attn_inner, w_in.shape[1] - dt_hi - 3 * attn_inner,
              SPLIT_PIECES * ssm_heads]
    out_dtypes = [BF16, BF16, BF16, BF16, BF16, BF16, F32]
    nbytes = (sum(widths) * d * 2 + 2 * tm * d * 4 + tm * d * 2
              + 2 * sum(tm * wd * jnp.dtype(dt).itemsize for wd, dt in zip(widths, out_dtypes))
              + 2 * tm * 2 * MXU_DIM * 4)
    z, xbc, q, k, v, g, dt_raw = pl.pallas_call(
        functools.partial(_inproj_kernel, q_scale=ATTN_HEAD_DIM ** -0.5 * LOG2_E),
        grid=(t // tm,),
        in_specs=[pl.BlockSpec((tm, d), lambda i: (i, 0)), _const_spec((1, d))]
                 + [_const_spec(w.shape) for w in ws],
        out_specs=[pl.BlockSpec((tm, wd), lambda i: (i, 0)) for wd in widths],
        out_shape=[jax.ShapeDtypeStruct((t, wd), dt) for wd, dt in zip(widths, out_dtypes)],
        scratch_shapes=[pltpu.VMEM((tm, d), BF16)],
        compiler_params=pltpu.CompilerParams(
            dimension_semantics=("arbitrary",), vmem_limit_bytes=_vmem_limit(nbytes)),
        name="in_proj",
    )(x2d, norm_w.reshape(1, d), *ws)
    return z, xbc, dt_raw, q, k, v, g


def _ssd_kernel(xbc_ref, zg_ref, dtr_ref, convw_ref, convb_ref, dtb_ref, alog_ref, dskip_ref,
                expand_ref, o_ref,
                xpad_ref, xs_ref, bc_ref, dt_ref, state_ref,
                *, n_chunks, ssm_inner, heads):
    s_idx = pl.program_id(1)
    tm = xs_ref.shape[0]
    gn = SSM_GROUPS * SSM_STATE
    gw = ssm_inner // SSM_GROUPS

    halo = xpad_ref.shape[0] - tm

    @pl.when(s_idx == 0)
    def _():
        state_ref[...] = jnp.zeros_like(state_ref)
        xpad_ref[0:halo, :] = jnp.zeros((halo, xpad_ref.shape[1]), BF16)

    @pl.when(s_idx > 0)
    def _():
        xpad_ref[0:halo, :] = xpad_ref[tm:tm + halo, :]

    xpad_ref[halo:, :] = xbc_ref[...]

    blk = CONV_ROW_BLOCK
    srow = lax.broadcasted_iota(jnp.int32, ((CONV_WIDTH - 1) * blk, blk + halo), 0)
    scol = lax.broadcasted_iota(jnp.int32, ((CONV_WIDTH - 1) * blk, blk + halo), 1)
    shift_stack = (scol == (srow % blk) + halo - (srow // blk + 1)).astype(BF16)
    def shifted_rows(r0):
        return jnp.dot(shift_stack, xpad_ref[r0:r0 + blk + halo, :], preferred_element_type=F32)

    shifted_next = shifted_rows(0)
    for r0 in range(0, tm, blk):
        shifted = shifted_next
        if r0 + blk < tm:
            shifted_next = shifted_rows(r0 + blk)
        xblk = xpad_ref[r0 + halo:r0 + halo + blk, :]
        conv = xblk.astype(F32) * convw_ref[CONV_WIDTH - 1:CONV_WIDTH, :] + convb_ref[...]
        for k in range(1, CONV_WIDTH):
            conv = conv + shifted[(k - 1) * blk:k * blk, :] * convw_ref[CONV_WIDTH - 1 - k:CONV_WIDTH - k, :]
        conv = _silu(conv)
        xs_ref[r0:r0 + blk, :] = conv[:, :ssm_inner]
        bc_ref[r0:r0 + blk, :] = conv[:, ssm_inner:]
    dt_ref[...] = jax.nn.softplus(dtr_ref[...] + dtb_ref[...])

    a_head = -jnp.exp(alog_ref[...]) * LOG2_E
    expand = expand_ref[...]
    lane_group = lax.broadcasted_iota(jnp.int32, (CHUNK, SPLIT_PIECES * heads), 1) // heads

    def pieces(x):
        parts = _split3(x)
        out = parts[SPLIT_PIECES - 1]
        for i in range(SPLIT_PIECES - 2, -1, -1):
            out = jnp.where(lane_group == i, parts[i], out)
        return out

    row = lax.broadcasted_iota(jnp.int32, (CHUNK, SPLIT_PIECES * CHUNK), 0)
    col = lax.broadcasted_iota(jnp.int32, (CHUNK, SPLIT_PIECES * CHUNK), 1)
    tril = ((col % CHUNK) <= row).astype(BF16)
    prow = lax.broadcasted_iota(jnp.int32, (CHUNK, LANES), 0)
    plane = lax.broadcasted_iota(jnp.int32, (CHUNK, LANES), 1)
    causal_pair = (plane % SSM_HEAD_DIM) <= prow
    first_head = plane < SSM_HEAD_DIM
    drow = lax.broadcasted_iota(jnp.int32, (CHUNK, ssm_inner), 0)
    dlane = lax.broadcasted_iota(jnp.int32, (CHUNK, ssm_inner), 1)
    diag = (dlane % SSM_HEAD_DIM) == drow

    def decay_stage(c):
        rows = pl.ds(c * CHUNK, CHUNK)
        dt = dt_ref[rows, :]
        bc = bc_ref[rows, :].astype(BF16)
        dt_e = jnp.dot(pieces(dt), expand, preferred_element_type=F32)
        a_parts = jnp.concatenate(_split3(dt * a_head), axis=0)
        a_cum_h = jnp.dot(tril, a_parts, preferred_element_type=F32)
        a_cum = jnp.dot(pieces(a_cum_h), expand, preferred_element_type=F32)
        cbs = []
        for g in range(SSM_GROUPS):
            b_g = bc[:, g * SSM_STATE:(g + 1) * SSM_STATE]
            c_g = bc[:, gn + g * SSM_STATE:gn + (g + 1) * SSM_STATE]
            cbs.append(lax.dot_general(c_g, b_g, (((1,), (1,)), ((), ())),
                                       preferred_element_type=F32))
        return dt_e, a_cum, cbs

    def operand_stage(c, dt_e, a_cum, cbs):
        rows = pl.ds(c * CHUNK, CHUNK)
        xs = xs_ref[rows, :]
        a_last = a_cum[CHUNK - 1:CHUNK, :]
        xdt = xs * dt_e
        xdec = (xdt * jnp.exp2(a_last - a_cum)).astype(BF16)
        xdt_b = xdt.astype(BF16)
        a_t = jnp.sum(jnp.where(diag, a_cum, 0.0), axis=0, keepdims=True)
        ms, rhss = [], []
        for g in range(SSM_GROUPS):
            cb2 = jnp.concatenate([cbs[g], cbs[g]], axis=1)
            for j in range(gw // LANES):
                c0 = g * gw + j * LANES
                seg = a_cum[:, c0:c0 + LANES] - a_t[:, c0:c0 + LANES]
                ldec = jnp.exp2(jnp.where(causal_pair, seg, -jnp.inf))
                ms.append((cb2 * ldec).astype(BF16))
                xp = xdt_b[:, c0:c0 + LANES]
                zero = jnp.zeros_like(xp)
                rhss.append(jnp.concatenate([jnp.where(first_head, xp, zero),
                                             jnp.where(first_head, zero, xp)], axis=0))
        return xdec, ms, rhss

    def matmul_stage(c, a_cum, xdec, ms, rhss):
        rows = pl.ds(c * CHUNK, CHUNK)
        bc = bc_ref[rows, :]
        a_last = a_cum[CHUNK - 1:CHUNK, :]
        y_offs, y_diags = [], []
        for g in range(SSM_GROUPS):
            g0 = g * gw
            b_g = bc[:, g * SSM_STATE:(g + 1) * SSM_STATE]
            c_g = bc[:, gn + g * SSM_STATE:gn + (g + 1) * SSM_STATE].astype(BF16)
            prev = state_ref[:, g0:g0 + gw]
            y_offs.append(jnp.dot(c_g, prev.astype(BF16), preferred_element_type=F32))
            new = jnp.dot(b_g.T.astype(BF16), xdec[:, g0:g0 + gw], preferred_element_type=F32)
            state_ref[:, g0:g0 + gw] = prev * jnp.exp2(a_last[:, g0:g0 + gw]) + new
        for m, rhs in zip(ms, rhss):
            y_diags.append(jnp.dot(m, rhs, preferred_element_type=F32))
        return jnp.concatenate(y_offs, axis=1), jnp.concatenate(y_diags, axis=1)

    def output_stage(c, a_cum, y_off, y_diag):
        rows = pl.ds(c * CHUNK, CHUNK)
        y = y_diag + y_off * jnp.exp2(a_cum) + dskip_ref[...] * xs_ref[rows, :]
        y = y * zg_ref[rows, :].astype(F32)
        outs = []
        for g in range(SSM_GROUPS):
            yg = y[:, g * gw:(g + 1) * gw]
            outs.append(yg * lax.rsqrt(jnp.mean(yg * yg, axis=-1, keepdims=True) + RMS_EPS))
        o_ref[rows, :] = jnp.concatenate(outs, axis=1).astype(o_ref.dtype)

    decays = {c: decay_stage(c) for c in range(min(2, n_chunks))}
    operands = {0: operand_stage(0, *decays[0])}
    for c in range(n_chunks):
        a_cum = decays[c][1]
        y_off, y_diag = matmul_stage(c, a_cum, *operands.pop(c))
        if c + 1 < n_chunks:
            operands[c + 1] = operand_stage(c + 1, *decays[c + 1])
        if c + 2 < n_chunks:
            decays[c + 2] = decay_stage(c + 2)
        output_stage(c, a_cum, y_off, y_diag)
        del decays[c]


def _ssd(xbc, zg, dt_raw, conv_w, conv_b, dt_bias, a_log, d_skip, *, batch, seq):
    t, conv_dim = xbc.shape
    inner = zg.shape[1]
    heads = a_log.shape[0]
    hrep = dt_raw.shape[1]
    tm = SSD_ROW_TILE
    assert seq % tm == 0 and inner == heads * SSM_HEAD_DIM and hrep == SPLIT_PIECES * heads
    n_s = seq // tm
    rep = lambda p: jnp.repeat(p.astype(F32), SSM_HEAD_DIM).reshape(1, inner)
    rep_h = lambda p: jnp.tile(p.astype(F32), SPLIT_PIECES).reshape(1, hrep)
    expand = jnp.tile(jnp.repeat(jnp.eye(heads, dtype=BF16), SSM_HEAD_DIM, axis=1),
                      (SPLIT_PIECES, 1))
    row_map = lambda b, s: (b * n_s + s, 0)
    nbytes = (2 * tm * (conv_dim + 2 * inner) * 2 + 2 * tm * LANES * 4
              + (tm + SUBLANES) * conv_dim * 4 + tm * conv_dim * 4 + tm * LANES * 4
              + SSM_STATE * inner * 4 + 3 * tm * conv_dim * 4 + 24 * CHUNK * inner * 4)
    return pl.pallas_call(
        functools.partial(_ssd_kernel, n_chunks=tm // CHUNK, ssm_inner=inner, heads=heads),
        grid=(batch, n_s),
        in_specs=[
            pl.BlockSpec((tm, conv_dim), row_map),
            pl.BlockSpec((tm, inner), row_map),
            pl.BlockSpec((tm, hrep), row_map),
            _const_spec((CONV_WIDTH, conv_dim)),
            _const_spec((1, conv_dim)),
            _const_spec((1, hrep)),
            _const_spec((1, hrep)),
            _const_spec((1, inner)),
            _const_spec((hrep, inner)),
        ],
        out_specs=pl.BlockSpec((tm, inner), row_map),
        out_shape=jax.ShapeDtypeStruct((t, inner), BF16),
        scratch_shapes=[
            pltpu.VMEM((tm + BF16_ROWS, conv_dim), BF16),
            pltpu.VMEM((tm, inner), F32),
            pltpu.VMEM((tm, conv_dim - inner), F32),
            pltpu.VMEM((tm, hrep), F32),
            pltpu.VMEM((SSM_STATE, inner), F32),
        ],
        compiler_params=pltpu.CompilerParams(
            dimension_semantics=("arbitrary", "arbitrary"), vmem_limit_bytes=_vmem_limit(nbytes)),
        name="ssd",
    )(xbc, zg, dt_raw, conv_w.astype(F32), conv_b.reshape(1, conv_dim).astype(F32),
      rep_h(dt_bias), rep_h(a_log), rep(d_skip), expand)


def _attn_key_chunks(qv):
    q_per_group = LANES // CHUNK
    return range(qv * q_per_group, qv * q_per_group + LEFT_CHUNKS + q_per_group)


def _attn_kernel(q_ref, kprev_ref, kcur_ref, vprev_ref, vcur_ref, bias0_ref, bias1_ref, o_ref,
                 s_ref, p_ref, *, heads):
    tq = s_ref.shape[2]
    n_slots = s_ref.shape[0]
    n_kc = s_ref.shape[1] // CHUNK
    n_qv = tq // LANES
    pair = LANES // ATTN_HEAD_DIM
    step_rows = q_ref.shape[0]
    bias_refs = (bias0_ref, bias1_ref)
    lane_head = lax.broadcasted_iota(jnp.int32, (tq, LANES), 1) // ATTN_HEAD_DIM

    def window(prev_ref, cur_ref, tile, lo):
        first = tile * tq
        last = first + n_kc * CHUNK - step_rows
        return jnp.concatenate([prev_ref[first:step_rows, lo:lo + LANES],
                                cur_ref[0:last, lo:lo + LANES]], axis=0)
    for slot in range(n_slots):
        for qv in range(n_qv):
            for kc in range(n_kc):
                if kc not in _attn_key_chunks(qv):
                    p_ref[slot, kc * CHUNK:(kc + 1) * CHUNK, qv * LANES:(qv + 1) * LANES] = jnp.zeros(
                        (CHUNK, LANES), BF16)

    items = [(tile, h) for tile in range(step_rows // tq) for h in range(heads)]

    def scores(u):
        tile, h = items[u]
        lo = (h // pair) * LANES
        qp = q_ref[tile * tq:(tile + 1) * tq, lo:lo + LANES]
        kp = window(kprev_ref, kcur_ref, tile, lo)
        qm = jnp.where(lane_head == h % pair, qp, jnp.zeros_like(qp))
        s_ref[u % n_slots] = lax.dot_general(kp, qm, (((1,), (1,)), ((), ())),
                                             preferred_element_type=F32)

    def softmax(u, qv):
        tile, h = items[u]
        bias_ref = bias_refs[tile]
        slot = u % n_slots
        cols = slice(qv * LANES, (qv + 1) * LANES)
        mx = None
        for kc in _attn_key_chunks(qv):
            rows = slice(kc * CHUNK, (kc + 1) * CHUNK)
            t = s_ref[slot, rows, cols] + bias_ref[0, h, rows, cols]
            mx = t if mx is None else jnp.maximum(mx, t)
        m = jnp.max(mx, axis=0, keepdims=True)
        acc = None
        for kc in _attn_key_chunks(qv):
            rows = slice(kc * CHUNK, (kc + 1) * CHUNK)
            p = jnp.exp2(s_ref[slot, rows, cols] + bias_ref[0, h, rows, cols] - m)
            acc = p if acc is None else acc + p
            p_ref[slot, rows, cols] = p.astype(BF16)
        return 1.0 / jnp.sum(acc, axis=0, keepdims=True)

    def weighted_values(u, inv):
        tile, h = items[u]
        lo = (h // pair) * LANES
        d0 = (h % pair) * ATTN_HEAD_DIM
        vpt = window(vprev_ref, vcur_ref, tile, lo).T
        o_t = jnp.dot(vpt[d0:d0 + ATTN_HEAD_DIM, :], p_ref[u % n_slots],
                      preferred_element_type=F32)
        return o_t * inv

    outs = []

    def finish(u, inv):
        tile, h = items[u]
        outs.append(weighted_values(u, inv))
        if len(outs) == pair:
            lo = (h // pair) * LANES
            o_ref[tile * tq:(tile + 1) * tq, lo:lo + LANES] = jnp.concatenate(
                outs, axis=0).T.astype(o_ref.dtype)
            outs.clear()

    scores(0)
    inv = None
    for u in range(len(items)):
        if u + 1 < len(items):
            scores(u + 1)
        if u > 0:
            finish(u - 1, inv)
        inv = jnp.concatenate([softmax(u, qv) for qv in range(n_qv)], axis=1)
    finish(len(items) - 1, inv)


def _attn_bias_kernel(f_ref, o_ref):
    n_var, _, nk, tq = o_ref.shape
    width = f_ref.shape[2]
    x = jnp.broadcast_to(f_ref[0], (nk, width))
    rolled = pltpu.roll(x, 0, 1, stride=1, stride_axis=0)
    t = rolled[:, nk:] * LOG2_E
    key = lax.broadcasted_iota(jnp.int32, (nk, tq), 0)
    kc = key // CHUNK
    qc = lax.broadcasted_iota(jnp.int32, (nk, tq), 1) // CHUNK
    band = (kc >= qc) & (kc <= qc + LEFT_CHUNKS)
    for v in range(n_var):
        ok = band & (key >= (n_var - 1 - v) * tq)
        o_ref[v, 0] = jnp.where(ok, t, -jnp.inf)


def _attn_bias_table(rel_bias):
    heads = rel_bias.shape[0]
    nk = ATTN_K_BLOCKS * ATTN_Q_TILE
    width = nk + ATTN_Q_TILE
    dist = jnp.arange(width) - nk + LEFT_CHUNKS * CHUNK
    f = rel_bias.astype(F32)[:, jnp.clip(dist, -MAX_REL_DIST, MAX_REL_DIST) + MAX_REL_DIST]
    return pl.pallas_call(
        _attn_bias_kernel,
        grid=(heads,),
        in_specs=[pl.BlockSpec((1, 1, width), lambda h: (h, 0, 0))],
        out_specs=pl.BlockSpec((ATTN_K_BLOCKS, 1, nk, ATTN_Q_TILE), lambda h: (0, h, 0, 0)),
        out_shape=jax.ShapeDtypeStruct((ATTN_K_BLOCKS, heads, nk, ATTN_Q_TILE), F32),
        compiler_params=pltpu.CompilerParams(dimension_semantics=("arbitrary",)),
        name="attn_bias",
    )(f.reshape(heads, 1, width))


def _attention(q, k, v, rel_bias, *, batch, seq):
    t, inner = q.shape
    heads = rel_bias.shape[0]
    tq = ATTN_Q_TILE
    rows = ATTN_TILES_PER_STEP * tq
    assert ATTN_TILES_PER_STEP == 2 and ATTN_K_BLOCKS == ATTN_TILES_PER_STEP + 1
    assert seq % rows == 0 and inner == heads * ATTN_HEAD_DIM
    n_s = seq // rows
    bias = _attn_bias_table(rel_bias)
    nk = ATTN_K_BLOCKS * tq

    cur = pl.BlockSpec((rows, inner), lambda j, b: (b * n_s + j, 0))
    prev = pl.BlockSpec((rows, inner), lambda j, b: (b * n_s + jnp.maximum(j - 1, 0), 0))

    def bias_spec(tile):
        return pl.BlockSpec(
            (1, heads, nk, tq),
            lambda j, b: (jnp.minimum(j * ATTN_TILES_PER_STEP + tile, ATTN_K_BLOCKS - 1), 0, 0, 0),
            pipeline_mode=pl.Buffered(1))

    nbytes = (2 * 6 * rows * inner * 2 + ATTN_TILES_PER_STEP * heads * tq * nk * 4
              + ATTN_SLOTS * tq * nk * (4 + 2) + 4 * tq * nk * 4)
    return pl.pallas_call(
        functools.partial(_attn_kernel, heads=heads),
        grid=(n_s, batch),
        in_specs=[cur, prev, cur, prev, cur, bias_spec(0), bias_spec(1)],
        out_specs=cur,
        out_shape=jax.ShapeDtypeStruct((t, inner), BF16),
        scratch_shapes=[pltpu.VMEM((ATTN_SLOTS, nk, tq), F32), pltpu.VMEM((ATTN_SLOTS, nk, tq), BF16)],
        compiler_params=pltpu.CompilerParams(
            dimension_semantics=("arbitrary", "arbitrary"), vmem_limit_bytes=_vmem_limit(nbytes)),
        name="band_attn",
    )(q, k, k, v, v, bias, bias)


def _merge_kernel(x_ref, ys_ref, ya_ref, g_ref, ws_ref, wa_ref, wo_ref, o_ref):
    d = x_ref.shape[1]
    half = x_ref.shape[0] // 2
    halves = (slice(0, half), slice(half, 2 * half))

    def branches(rows):
        return (jnp.dot(ys_ref[rows, :], ws_ref[...], preferred_element_type=F32),
                jnp.dot(ya_ref[rows, :], wa_ref[...], preferred_element_type=F32))

    def gate(rows, bs, ba):
        g = g_ref[rows, :].astype(F32)
        return (_sigmoid(g[:, :d]) * bs + _sigmoid(g[:, d:]) * ba).astype(BF16)

    def project(rows, merged):
        o_ref[rows, :] = x_ref[rows, :] + jnp.dot(merged, wo_ref[...], preferred_element_type=F32)

    b0 = branches(halves[0])
    b1 = branches(halves[1])
    m0 = gate(halves[0], *b0)
    project(halves[0], m0)
    m1 = gate(halves[1], *b1)
    project(halves[1], m1)


def _merge(x2d, y_ssm, y_attn, g, w_bs, w_ba, w_out):
    t, d = x2d.shape
    tm = FFN_ROW_TILE
    assert t % tm == 0
    row = lambda w: pl.BlockSpec((tm, w), lambda i: (i, 0))
    nbytes = (w_bs.size + w_ba.size + w_out.size) * 2 + 2 * tm * (2 * d * 4 + 4 * d * 2) + 6 * tm * d * 4
    return pl.pallas_call(
        _merge_kernel,
        grid=(t // tm,),
        in_specs=[row(d), row(y_ssm.shape[1]), row(y_attn.shape[1]), row(2 * d),
                  _const_spec(w_bs.shape), _const_spec(w_ba.shape), _const_spec(w_out.shape)],
        out_specs=row(d),
        out_shape=jax.ShapeDtypeStruct((t, d), F32),
        compiler_params=pltpu.CompilerParams(
            dimension_semantics=("arbitrary",), vmem_limit_bytes=_vmem_limit(nbytes)),
        name="merge",
    )(x2d, y_ssm, y_attn, g, w_bs.astype(BF16), w_ba.astype(BF16), w_out.astype(BF16))


def kernel(x, ffn1_norm_w, ffn1_w_gu, ffn1_w_down, mix_norm_w, w_in, conv_w, conv_b, dt_bias, A_log,
           D_skip, ssm_norm_w, rel_bias, w_branch_ssm, w_branch_attn, w_out, ffn2_norm_w, ffn2_w_gu,
           ffn2_w_down, final_norm_w):
    batch, seq, d = x.shape
    depth = ffn1_w_gu.shape[0]
    assert depth >= 1
    ssm_heads = A_log.shape[1]
    ssm_inner = ssm_heads * SSM_HEAD_DIM
    conv_dim = conv_w.shape[2]
    attn_inner = rel_bias.shape[1] * ATTN_HEAD_DIM
    xf = x.reshape(batch * seq, d)
    for l in range(depth):
        last = l == depth - 1
        xf = _ffn(xf, ffn1_norm_w[l], ffn1_w_gu[l], ffn1_w_down[l], final_norm_w, final_norm=False)
        zg, xbc, dt_raw, q, k, v, g = _in_proj(
            xf, mix_norm_w[l], w_in[l], ssm_inner=ssm_inner, conv_dim=conv_dim,
            ssm_heads=ssm_heads, attn_inner=attn_inner)
        y_ssm = _ssd(xbc, zg, dt_raw, conv_w[l], conv_b[l], dt_bias[l], A_log[l], D_skip[l],
                     batch=batch, seq=seq)
        y_attn = _attention(q, k, v, rel_bias[l], batch=batch, seq=seq)
        w_bs = ssm_norm_w[l].astype(F32)[:, None] * w_branch_ssm[l]
        xf = _merge(xf, y_ssm, y_attn, g, w_bs, w_branch_attn[l], w_out[l])
        xf = _ffn(xf, ffn2_norm_w[l], ffn2_w_gu[l], ffn2_w_down[l], final_norm_w, final_norm=last)
    return xf.reshape(batch, seq, d)
```

```python
import functools

import jax
import jax.numpy as jnp
from jax import lax
from jax.experimental import pallas as pl
from jax.experimental.pallas import tpu as pltpu

F32 = jnp.float32
BF16 = jnp.bfloat16

CHUNK = 64
SSM_HEAD_DIM = 64
SSM_GROUPS = 2
SSM_STATE = 128
CONV_WIDTH = 4
ATTN_HEAD_DIM = 64
LEFT_CHUNKS = 8
MAX_REL_DIST = 128
FFN_RES_SCALE = 0.5
RMS_EPS = 1e-6
LOG2_E = 1.4426950408889634

LANES = 128
SUBLANES = 8
BF16_ROWS = 16
MXU_DIM = 256
VMEM_BYTES = 64 * 1024 * 1024

ROW_TILE = 512
FFN_ROW_TILE = 1024
FFN_COL_TILE = 256
SSD_ROW_TILE = 1024
CONV_ROW_BLOCK = 128
ATTN_Q_CHUNKS = 4
ATTN_Q_TILE = ATTN_Q_CHUNKS * CHUNK
ATTN_TILES_PER_STEP = 2
ATTN_SLOTS = 2
ATTN_K_BLOCKS = (LEFT_CHUNKS + ATTN_Q_CHUNKS) * CHUNK // ATTN_Q_TILE


def _vmem_limit(nbytes):
    return int(min(nbytes * 1.25 + (8 << 20), VMEM_BYTES - (6 << 20)))


def _const_spec(shape):
    zeros = (0,) * len(shape)
    return pl.BlockSpec(shape, lambda *_: zeros, pipeline_mode=pl.Buffered(1))


def _rmsnorm(x, w):
    ms = jnp.mean(x * x, axis=-1, keepdims=True)
    return x * lax.rsqrt(ms + RMS_EPS) * w


def _sigmoid(x):
    return 0.5 + 0.5 * jnp.tanh(0.5 * x)


def _silu(x):
    h = 0.5 * x
    return h + h * jnp.tanh(h)


SPLIT_PIECES = 3


def _split3(x):
    hi = x.astype(BF16)
    r1 = x - hi.astype(F32)
    mid = r1.astype(BF16)
    lo = (r1 - mid.astype(F32)).astype(BF16)
    return hi, mid, lo


def _ffn_kernel(x_ref, nw_ref, wgu_ref, wd_ref, fnw_ref, o_ref, h_ref, a_ref, *, d_ff, final_norm):
    half = x_ref.shape[0] // 2
    halves = (slice(0, half), slice(half, 2 * half))
    col_tiles = [(c, min(FFN_COL_TILE, d_ff - c)) for c in range(0, d_ff, FFN_COL_TILE)]

    def prologue(rows):
        h_ref[rows, :] = _rmsnorm(x_ref[rows, :], nw_ref[...]).astype(BF16)

    def gate_up(rows, c, w):
        h = h_ref[rows, :]
        g = jnp.dot(h, wgu_ref[:, c:c + w], preferred_element_type=F32)
        u = jnp.dot(h, wgu_ref[:, d_ff + c:d_ff + c + w], preferred_element_type=F32)
        a_ref[rows, c:c + w] = (_silu(g) * u).astype(BF16)

    def down(rows):
        return jnp.dot(a_ref[rows, :], wd_ref[...], preferred_element_type=F32)

    def epilogue(rows, acc):
        y = x_ref[rows, :] + FFN_RES_SCALE * acc
        if final_norm:
            y = _rmsnorm(y, fnw_ref[...])
        o_ref[rows, :] = y

    prologue(halves[0])
    for i, (c, w) in enumerate(col_tiles):
        gate_up(halves[0], c, w)
        if i == 0:
            prologue(halves[1])
    acc0 = down(halves[0])
    for i, (c, w) in enumerate(col_tiles):
        gate_up(halves[1], c, w)
        if i == 0:
            epilogue(halves[0], acc0)
    epilogue(halves[1], down(halves[1]))


def _ffn(x2d, norm_w, w_gu, w_down, final_norm_w, *, final_norm):
    t, d = x2d.shape
    d_ff = w_down.shape[0]
    assert t % FFN_ROW_TILE == 0 and d_ff % LANES == 0
    tm = FFN_ROW_TILE
    nbytes = (3 * d * d_ff * 2 + 4 * tm * d * 4 + tm * d * 2 + tm * d_ff * 2
              + 2 * tm * FFN_COL_TILE * 4 + 2 * tm * d * 4)
    return pl.pallas_call(
        functools.partial(_ffn_kernel, d_ff=d_ff, final_norm=final_norm),
        grid=(t // tm,),
        in_specs=[
            pl.BlockSpec((tm, d), lambda i: (i, 0)),
            _const_spec((1, d)),
            _const_spec((d, 2 * d_ff)),
            _const_spec((d_ff, d)),
            _const_spec((1, d)),
        ],
        out_specs=pl.BlockSpec((tm, d), lambda i: (i, 0)),
        out_shape=jax.ShapeDtypeStruct((t, d), F32),
        scratch_shapes=[pltpu.VMEM((tm, d), BF16), pltpu.VMEM((tm, d_ff), BF16)],
        compiler_params=pltpu.CompilerParams(
            dimension_semantics=("arbitrary",), vmem_limit_bytes=_vmem_limit(nbytes)),
        name="ffn_final" if final_norm else "ffn",
    )(x2d, norm_w.reshape(1, d), w_gu.astype(BF16), w_down.astype(BF16), final_norm_w.reshape(1, d))


def _inproj_kernel(x_ref, nw_ref, wa_ref, wb_ref, wdt_ref, z_ref, xbc_ref, q_ref, k_ref, v_ref, g_ref,
                   dt_ref, h_ref, *, q_scale):
    h_ref[...] = _rmsnorm(x_ref[...], nw_ref[...]).astype(BF16)
    for w_ref, o_refs in ((wa_ref, (z_ref, xbc_ref)), (wb_ref, (q_ref, k_ref, v_ref, g_ref)),
                          (wdt_ref, (dt_ref,))):
        col = 0
        for o_ref in o_refs:
            n = o_ref.shape[1]
            step = min(n, 2 * MXU_DIM)
            for c in range(0, n, step):
                r = jnp.dot(h_ref[...], w_ref[:, col + c:col + c + step], preferred_element_type=F32)
                if o_ref is q_ref:
                    r = r * q_scale
                if o_ref is z_ref:
                    r = _silu(r)
                o_ref[:, c:c + step] = r.astype(o_ref.dtype)
            col += n


def _in_proj(x2d, norm_w, w_in, *, ssm_inner, conv_dim, ssm_heads, attn_inner):
    t, d = x2d.shape
    tm = ROW_TILE
    dt_lo = ssm_inner + conv_dim
    dt_hi = dt_lo + ssm_heads
    assert dt_lo % LANES == 0
    ws = [w_in[:, :dt_lo].astype(BF16), w_in[:, dt_hi:].astype(BF16),
          jnp.tile(w_in[:, dt_lo:dt_hi], (1, SPLIT_PIECES)).astype(BF16)]
    widths = [ssm_inner, conv_dim, attn_inner, attn_inner, attn_inner, w_in.shape[1] - dt_hi - 3 * attn_inner,
              SPLIT_PIECES * ssm_heads]
    out_dtypes = [BF16, BF16, BF16, BF16, BF16, BF16, F32]
    nbytes = (sum(widths) * d * 2 + 2 * tm * d * 4 + tm * d * 2
              + 2 * sum(tm * wd * jnp.dtype(dt).itemsize for wd, dt in zip(widths, out_dtypes))
              + 2 * tm * 2 * MXU_DIM * 4)
    z, xbc, q, k, v, g, dt_raw = pl.pallas_call(
        functools.partial(_inproj_kernel, q_scale=ATTN_HEAD_DIM ** -0.5 * LOG2_E),
        grid=(t // tm,),
        in_specs=[pl.BlockSpec((tm, d), lambda i: (i, 0)), _const_spec((1, d))]
                 + [_const_spec(w.shape) for w in ws],
        out_specs=[pl.BlockSpec((tm, wd), lambda i: (i, 0)) for wd in widths],
        out_shape=[jax.ShapeDtypeStruct((t, wd), dt) for wd, dt in zip(widths, out_dtypes)],
        scratch_shapes=[pltpu.VMEM((tm, d), BF16)],
        compiler_params=pltpu.CompilerParams(
            dimension_semantics=("arbitrary",), vmem_limit_bytes=_vmem_limit(nbytes)),
        name="in_proj",
    )(x2d, norm_w.reshape(1, d), *ws)
    return z, xbc, dt_raw, q, k, v, g


def _ssd_kernel(xbc_ref, zg_ref, dtr_ref, convw_ref, convb_ref, dtb_ref, alog_ref, dskip_ref,
                expand_ref, o_ref,
                xpad_ref, xs_ref, bc_ref, dt_ref, state_ref,
                *, n_chunks, ssm_inner, heads):
    s_idx = pl.program_id(1)
    tm = xs_ref.shape[0]
    gn = SSM_GROUPS * SSM_STATE
    gw = ssm_inner // SSM_GROUPS

    halo = xpad_ref.shape[0] - tm

    @pl.when(s_idx == 0)
    def _():
        state_ref[...] = jnp.zeros_like(state_ref)
        xpad_ref[0:halo, :] = jnp.zeros((halo, xpad_ref.shape[1]), BF16)

    @pl.when(s_idx > 0)
    def _():
        xpad_ref[0:halo, :] = xpad_ref[tm:tm + halo, :]

    xpad_ref[halo:, :] = xbc_ref[...]

    blk = CONV_ROW_BLOCK
    srow = lax.broadcasted_iota(jnp.int32, ((CONV_WIDTH - 1) * blk, blk + halo), 0)
    scol = lax.broadcasted_iota(jnp.int32, ((CONV_WIDTH - 1) * blk, blk + halo), 1)
    shift_stack = (scol == (srow % blk) + halo - (srow // blk + 1)).astype(BF16)
    def shifted_rows(r0):
        return jnp.dot(shift_stack, xpad_ref[r0:r0 + blk + halo, :], preferred_element_type=F32)

    shifted_next = shifted_rows(0)
    for r0 in range(0, tm, blk):
        shifted = shifted_next
        if r0 + blk < tm:
            shifted_next = shifted_rows(r0 + blk)
        xblk = xpad_ref[r0 + halo:r0 + halo + blk, :]
        conv = xblk.astype(F32) * convw_ref[CONV_WIDTH - 1:CONV_WIDTH, :] + convb_ref[...]
        for k in range(1, CONV_WIDTH):
            conv = conv + shifted[(k - 1) * blk:k * blk, :] * convw_ref[CONV_WIDTH - 1 - k:CONV_WIDTH - k, :]
        conv = _silu(conv)
        xs_ref[r0:r0 + blk, :] = conv[:, :ssm_inner]
        bc_ref[r0:r0 + blk, :] = conv[:, ssm_inner:]
    dt_ref[...] = jax.nn.softplus(dtr_ref[...] + dtb_ref[...])

    a_head = -jnp.exp(alog_ref[...]) * LOG2_E
    expand = expand_ref[...]
    lane_group = lax.broadcasted_iota(jnp.int32, (CHUNK, SPLIT_PIECES * heads), 1) // heads

    def pieces(x):
        parts = _split3(x)
        out = parts[SPLIT_PIECES - 1]
        for i in range(SPLIT_PIECES - 2, -1, -1):
            out = jnp.where(lane_group == i, parts[i], out)
        return out

    row = lax.broadcasted_iota(jnp.int32, (CHUNK, SPLIT_PIECES * CHUNK), 0)
    col = lax.broadcasted_iota(jnp.int32, (CHUNK, SPLIT_PIECES * CHUNK), 1)
    tril = ((col % CHUNK) <= row).astype(BF16)
    prow = lax.broadcasted_iota(jnp.int32, (CHUNK, LANES), 0)
    plane = lax.broadcasted_iota(jnp.int32, (CHUNK, LANES), 1)
    causal_pair = (plane % SSM_HEAD_DIM) <= prow
    first_head = plane < SSM_HEAD_DIM
    drow = lax.broadcasted_iota(jnp.int32, (CHUNK, ssm_inner), 0)
    dlane = lax.broadcasted_iota(jnp.int32, (CHUNK, ssm_inner), 1)
    diag = (dlane % SSM_HEAD_DIM) == drow

    def decay_stage(c):
        rows = pl.ds(c * CHUNK, CHUNK)
        dt = dt_ref[rows, :]
        bc = bc_ref[rows, :].astype(BF16)
        dt_e = jnp.dot(pieces(dt), expand, preferred_element_type=F32)
        a_parts = jnp.concatenate(_split3(dt * a_head), axis=0)
        a_cum_h = jnp.dot(tril, a_parts, preferred_element_type=F32)
        a_cum = jnp.dot(pieces(a_cum_h), expand, preferred_element_type=F32)
        cbs = []
        for g in range(SSM_GROUPS):
            b_g = bc[:, g * SSM_STATE:(g + 1) * SSM_STATE]
            c_g = bc[:, gn + g * SSM_STATE:gn + (g + 1) * SSM_STATE]
            cbs.append(lax.dot_general(c_g, b_g, (((1,), (1,)), ((), ())),
                                       preferred_element_type=F32))
        return dt_e, a_cum, cbs

    def operand_stage(c, dt_e, a_cum, cbs):
        rows = pl.ds(c * CHUNK, CHUNK)
        xs = xs_ref[rows, :]
        a_last = a_cum[CHUNK - 1:CHUNK, :]
        xdt = xs * dt_e
        xdec = (xdt * jnp.exp2(a_last - a_cum)).astype(BF16)
        xdt_b = xdt.astype(BF16)
        a_t = jnp.sum(jnp.where(diag, a_cum, 0.0), axis=0, keepdims=True)
        ms, rhss = [], []
        for g in range(SSM_GROUPS):
            cb2 = jnp.concatenate([cbs[g], cbs[g]], axis=1)
            for j in range(gw // LANES):
                c0 = g * gw + j * LANES
                seg = a_cum[:, c0:c0 + LANES] - a_t[:, c0:c0 + LANES]
                ldec = jnp.exp2(jnp.where(causal_pair, seg, -jnp.inf))
                ms.append((cb2 * ldec).astype(BF16))
                xp = xdt_b[:, c0:c0 + LANES]
                zero = jnp.zeros_like(xp)
                rhss.append(jnp.concatenate([jnp.where(first_head, xp, zero),
                                             jnp.where(first_head, zero, xp)], axis=0))
        return xdec, ms, rhss

    def matmul_stage(c, a_cum, xdec, ms, rhss):
        rows = pl.ds(c * CHUNK, CHUNK)
        bc = bc_ref[rows, :]
        a_last = a_cum[CHUNK - 1:CHUNK, :]
        y_offs, y_diags = [], []
        for g in range(SSM_GROUPS):
            g0 = g * gw
            b_g = bc[:, g * SSM_STATE:(g + 1) * SSM_STATE]
            c_g = bc[:, gn + g * SSM_STATE:gn + (g + 1) * SSM_STATE].astype(BF16)
            prev = state_ref[:, g0:g0 + gw]
            y_offs.append(jnp.dot(c_g, prev.astype(BF16), preferred_element_type=F32))
            new = jnp.dot(b_g.T.astype(BF16), xdec[:, g0:g0 + gw], preferred_element_type=F32)
            state_ref[:, g0:g0 + gw] = prev * jnp.exp2(a_last[:, g0:g0 + gw]) + new
        for m, rhs in zip(ms, rhss):
            y_diags.append(jnp.dot(m, rhs, preferred_element_type=F32))
        return jnp.concatenate(y_offs, axis=1), jnp.concatenate(y_diags, axis=1)

    def output_stage(c, a_cum, y_off, y_diag):
        rows = pl.ds(c * CHUNK, CHUNK)
        y = y_diag + y_off * jnp.exp2(a_cum) + dskip_ref[...] * xs_ref[rows, :]
        y = y * zg_ref[rows, :].astype(F32)
        outs = []
        for g in range(SSM_GROUPS):
            yg = y[:, g * gw:(g + 1) * gw]
            outs.append(yg * lax.rsqrt(jnp.mean(yg * yg, axis=-1, keepdims=True) + RMS_EPS))
        o_ref[rows, :] = jnp.concatenate(outs, axis=1).astype(o_ref.dtype)

    decays = {c: decay_stage(c) for c in range(min(2, n_chunks))}
    operands = {0: operand_stage(0, *decays[0])}
    for c in range(n_chunks):
        a_cum = decays[c][1]
        y_off, y_diag = matmul_stage(c, a_cum, *operands.pop(c))
        if c + 1 < n_chunks:
            operands[c + 1] = operand_stage(c + 1, *decays[c + 1])
        if c + 2 < n_chunks:
            decays[c + 2] = decay_stage(c + 2)
        output_stage(c, a_cum, y_off, y_diag)
        del decays[c]


def _ssd(xbc, zg, dt_raw, conv_w, conv_b, dt_bias, a_log, d_skip, *, batch, seq):
    t, conv_dim = xbc.shape
    inner = zg.shape[1]
    heads = a_log.shape[0]
    hrep = dt_raw.shape[1]
    tm = SSD_ROW_TILE
    assert seq % tm == 0 and inner == heads * SSM_HEAD_DIM and hrep == SPLIT_PIECES * heads
    n_s = seq // tm
    rep = lambda p: jnp.repeat(p.astype(F32), SSM_HEAD_DIM).reshape(1, inner)
    rep_h = lambda p: jnp.tile(p.astype(F32), SPLIT_PIECES).reshape(1, hrep)
    expand = jnp.tile(jnp.repeat(jnp.eye(heads, dtype=BF16), SSM_HEAD_DIM, axis=1),
                      (SPLIT_PIECES, 1))
    row_map = lambda b, s: (b * n_s + s, 0)
    nbytes = (2 * tm * (conv_dim + 2 * inner) * 2 + 2 * tm * LANES * 4
              + (tm + SUBLANES) * conv_dim * 4 + tm * conv_dim * 4 + tm * LANES * 4
              + SSM_STATE * inner * 4 + 3 * tm * conv_dim * 4 + 24 * CHUNK * inner * 4)
    return pl.pallas_call(
        functools.partial(_ssd_kernel, n_chunks=tm // CHUNK, ssm_inner=inner, heads=heads),
        grid=(batch, n_s),
        in_specs=[
            pl.BlockSpec((tm, conv_dim), row_map),
            pl.BlockSpec((tm, inner), row_map),
            pl.BlockSpec((tm, hrep), row_map),
            _const_spec((CONV_WIDTH, conv_dim)),
            _const_spec((1, conv_dim)),
            _const_spec((1, hrep)),
            _const_spec((1, hrep)),
            _const_spec((1, inner)),
            _const_spec((hrep, inner)),
        ],
        out_specs=pl.BlockSpec((tm, inner), row_map),
        out_shape=jax.ShapeDtypeStruct((t, inner), BF16),
        scratch_shapes=[
            pltpu.VMEM((tm + BF16_ROWS, conv_dim), BF16),
            pltpu.VMEM((tm, inner), F32),
            pltpu.VMEM((tm, conv_dim - inner), F32),
            pltpu.VMEM((tm, hrep), F32),
            pltpu.VMEM((SSM_STATE, inner), F32),
        ],
        compiler_params=pltpu.CompilerParams(
            dimension_semantics=("arbitrary", "arbitrary"), vmem_limit_bytes=_vmem_limit(nbytes)),
        name="ssd",
    )(xbc, zg, dt_raw, conv_w.astype(F32), conv_b.reshape(1, conv_dim).astype(F32),
      rep_h(dt_bias), rep_h(a_log), rep(d_skip), expand)


def _attn_key_chunks(qv):
    q_per_group = LANES // CHUNK
    return range(qv * q_per_group, qv * q_per_group + LEFT_CHUNKS + q_per_group)


def _attn_kernel(q_ref, kprev_ref, kcur_ref, vprev_ref, vcur_ref, bias0_ref, bias1_ref, o_ref,
                 s_ref, p_ref, *, heads):
    tq = s_ref.shape[2]
    n_slots = s_ref.shape[0]
    n_kc = s_ref.shape[1] // CHUNK
    n_qv = tq // LANES
    pair = LANES // ATTN_HEAD_DIM
    step_rows = q_ref.shape[0]
    bias_refs = (bias0_ref, bias1_ref)
    lane_head = lax.broadcasted_iota(jnp.int32, (tq, LANES), 1) // ATTN_HEAD_DIM

    def window(prev_ref, cur_ref, tile, lo):
        first = tile * tq
        last = first + n_kc * CHUNK - step_rows
        return jnp.concatenate([prev_ref[first:step_rows, lo:lo + LANES],
                                cur_ref[0:last, lo:lo + LANES]], axis=0)
    for slot in range(n_slots):
        for qv in range(n_qv):
            for kc in range(n_kc):
                if kc not in _attn_key_chunks(qv):
                    p_ref[slot, kc * CHUNK:(kc + 1) * CHUNK, qv * LANES:(qv + 1) * LANES] = jnp.zeros(
                        (CHUNK, LANES), BF16)

    items = [(tile, h) for tile in range(step_rows // tq) for h in range(heads)]

    def scores(u):
        tile, h = items[u]
        lo = (h // pair) * LANES
        qp = q_ref[tile * tq:(tile + 1) * tq, lo:lo + LANES]
        kp = window(kprev_ref, kcur_ref, tile, lo)
        qm = jnp.where(lane_head == h % pair, qp, jnp.zeros_like(qp))
        s_ref[u % n_slots] = lax.dot_general(kp, qm, (((1,), (1,)), ((), ())),
                                             preferred_element_type=F32)

    def softmax(u, qv):
        tile, h = items[u]
        bias_ref = bias_refs[tile]
        slot = u % n_slots
        cols = slice(qv * LANES, (qv + 1) * LANES)
        mx = None
        for kc in _attn_key_chunks(qv):
            rows = slice(kc * CHUNK, (kc + 1) * CHUNK)
            t = s_ref[slot, rows, cols] + bias_ref[0, h, rows, cols]
            mx = t if mx is None else jnp.maximum(mx, t)
        m = jnp.max(mx, axis=0, keepdims=True)
        acc = None
        for kc in _attn_key_chunks(qv):
            rows = slice(kc * CHUNK, (kc + 1) * CHUNK)
            p = jnp.exp2(s_ref[slot, rows, cols] + bias_ref[0, h, rows, cols] - m)
            acc = p if acc is None else acc + p
            p_ref[slot, rows, cols] = p.astype(BF16)
        return 1.0 / jnp.sum(acc, axis=0, keepdims=True)

    def weighted_values(u, inv):
        tile, h = items[u]
        lo = (h // pair) * LANES
        d0 = (h % pair) * ATTN_HEAD_DIM
        vpt = window(vprev_ref, vcur_ref, tile, lo).T
        o_t = jnp.dot(vpt[d0:d0 + ATTN_HEAD_DIM, :], p_ref[u % n_slots],
                      preferred_element_type=F32)
        return o_t * inv

    outs = []

    def finish(u, inv):
        tile, h = items[u]
        outs.append(weighted_values(u, inv))
        if len(outs) == pair:
            lo = (h // pair) * LANES
            o_ref[tile * tq:(tile + 1) * tq, lo:lo + LANES] = jnp.concatenate(
                outs, axis=0).T.astype(o_ref.dtype)
            outs.clear()

    scores(0)
    inv = None
    for u in range(len(items)):
        if u + 1 < len(items):
            scores(u + 1)
        if u > 0:
            finish(u - 1, inv)
        inv = jnp.concatenate([softmax(u, qv) for qv in range(n_qv)], axis=1)
    finish(len(items) - 1, inv)


def _attn_bias_kernel(f_ref, o_ref):
    n_var, _, nk, tq = o_ref.shape
    width = f_ref.shape[2]
    x = jnp.broadcast_to(f_ref[0], (nk, width))
    rolled = pltpu.roll(x, 0, 1, stride=1, stride_axis=0)
    t = rolled[:, nk:] * LOG2_E
    key = lax.broadcasted_iota(jnp.int32, (nk, tq), 0)
    kc = key // CHUNK
    qc = lax.broadcasted_iota(jnp.int32, (nk, tq), 1) // CHUNK
    band = (kc >= qc) & (kc <= qc + LEFT_CHUNKS)
    for v in range(n_var):
        ok = band & (key >= (n_var - 1 - v) * tq)
        o_ref[v, 0] = jnp.where(ok, t, -jnp.inf)


def _attn_bias_table(rel_bias):
    heads = rel_bias.shape[0]
    nk = ATTN_K_BLOCKS * ATTN_Q_TILE
    width = nk + ATTN_Q_TILE
    dist = jnp.arange(width) - nk + LEFT_CHUNKS * CHUNK
    f = rel_bias.astype(F32)[:, jnp.clip(dist, -MAX_REL_DIST, MAX_REL_DIST) + MAX_REL_DIST]
    return pl.pallas_call(
        _attn_bias_kernel,
        grid=(heads,),
        in_specs=[pl.BlockSpec((1, 1, width), lambda h: (h, 0, 0))],
        out_specs=pl.BlockSpec((ATTN_K_BLOCKS, 1, nk, ATTN_Q_TILE), lambda h: (0, h, 0, 0)),
        out_shape=jax.ShapeDtypeStruct((ATTN_K_BLOCKS, heads, nk, ATTN_Q_TILE), F32),
        compiler_params=pltpu.CompilerParams(dimension_semantics=("arbitrary",)),
        name="attn_bias",
    )(f.reshape(heads, 1, width))


def _attention(q, k, v, rel_bias, *, batch, seq):
    t, inner = q.shape
    heads = rel_bias.shape[0]
    tq = ATTN_Q_TILE
    rows = ATTN_TILES_PER_STEP * tq
    assert ATTN_TILES_PER_STEP == 2 and ATTN_K_BLOCKS == ATTN_TILES_PER_STEP + 1
    assert seq % rows == 0 and inner == heads * ATTN_HEAD_DIM
    n_s = seq // rows
    bias = _attn_bias_table(rel_bias)
    nk = ATTN_K_BLOCKS * tq

    cur = pl.BlockSpec((rows, inner), lambda j, b: (b * n_s + j, 0))
    prev = pl.BlockSpec((rows, inner), lambda j, b: (b * n_s + jnp.maximum(j - 1, 0), 0))

    def bias_spec(tile):
        return pl.BlockSpec(
            (1, heads, nk, tq),
            lambda j, b: (jnp.minimum(j * ATTN_TILES_PER_STEP + tile, ATTN_K_BLOCKS - 1), 0, 0, 0),
            pipeline_mode=pl.Buffered(1))

    nbytes = (2 * 6 * rows * inner * 2 + ATTN_TILES_PER_STEP * heads * tq * nk * 4
              + ATTN_SLOTS * tq * nk * (4 + 2) + 4 * tq * nk * 4)
    return pl.pallas_call(
        functools.partial(_attn_kernel, heads=heads),
        grid=(n_s, batch),
        in_specs=[cur, prev, cur, prev, cur, bias_spec(0), bias_spec(1)],
        out_specs=cur,
        out_shape=jax.ShapeDtypeStruct((t, inner), BF16),
        scratch_shapes=[pltpu.VMEM((ATTN_SLOTS, nk, tq), F32), pltpu.VMEM((ATTN_SLOTS, nk, tq), BF16)],
        compiler_params=pltpu.CompilerParams(
            dimension_semantics=("arbitrary", "arbitrary"), vmem_limit_bytes=_vmem_limit(nbytes)),
        name="band_attn",
    )(q, k, k, v, v, bias, bias)


def _merge_kernel(x_ref, ys_ref, ya_ref, g_ref, ws_ref, wa_ref, wo_ref, o_ref):
    d = x_ref.shape[1]
    half = x_ref.shape[0] // 2
    halves = (slice(0, half), slice(half, 2 * half))

    def branches(rows):
        return (jnp.dot(ys_ref[rows, :], ws_ref[...], preferred_element_type=F32),
                jnp.dot(ya_ref[rows, :], wa_ref[...], preferred_element_type=F32))

    def gate(rows, bs, ba):
        g = g_ref[rows, :].astype(F32)
        return (_sigmoid(g[:, :d]) * bs + _sigmoid(g[:, d:]) * ba).astype(BF16)

    def project(rows, merged):
        o_ref[rows, :] = x_ref[rows, :] + jnp.dot(merged, wo_ref[...], preferred_element_type=F32)

    b0 = branches(halves[0])
    b1 = branches(halves[1])
    m0 = gate(halves[0], *b0)
    project(halves[0], m0)
    m1 = gate(halves[1], *b1)
    project(halves[1], m1)


def _merge(x2d, y_ssm, y_attn, g, w_bs, w_ba, w_out):
    t, d = x2d.shape
    tm = FFN_ROW_TILE
    assert t % tm == 0
    row = lambda w: pl.BlockSpec((tm, w), lambda i: (i, 0))
    nbytes = (w_bs.size + w_ba.size + w_out.size) * 2 + 2 * tm * (2 * d * 4 + 4 * d * 2) + 6 * tm * d * 4
    return pl.pallas_call(
        _merge_kernel,
        grid=(t // tm,),
        in_specs=[row(d), row(y_ssm.shape[1]), row(y_attn.shape[1]), row(2 * d),
                  _const_spec(w_bs.shape), _const_spec(w_ba.shape), _const_spec(w_out.shape)],
        out_specs=row(d),
        out_shape=jax.ShapeDtypeStruct((t, d), F32),
        compiler_params=pltpu.CompilerParams(
            dimension_semantics=("arbitrary",), vmem_limit_bytes=_vmem_limit(nbytes)),
        name="merge",
    )(x2d, y_ssm, y_attn, g, w_bs.astype(BF16), w_ba.astype(BF16), w_out.astype(BF16))


def kernel(x, ffn1_norm_w, ffn1_w_gu, ffn1_w_down, mix_norm_w, w_in, conv_w, conv_b, dt_bias, A_log,
           D_skip, ssm_norm_w, rel_bias, w_branch_ssm, w_branch_attn, w_out, ffn2_norm_w, ffn2_w_gu,
           ffn2_w_down, final_norm_w):
    batch, seq, d = x.shape
    depth = ffn1_w_gu.shape[0]
    assert depth >= 1
    ssm_heads = A_log.shape[1]
    ssm_inner = ssm_heads * SSM_HEAD_DIM
    conv_dim = conv_w.shape[2]
    attn_inner = rel_bias.shape[1] * ATTN_HEAD_DIM
    xf = x.reshape(batch * seq, d)
    for l in range(depth):
        last = l == depth - 1
        xf = _ffn(xf, ffn1_norm_w[l], ffn1_w_gu[l], ffn1_w_down[l], final_norm_w, final_norm=False)
        zg, xbc, dt_raw, q, k, v, g = _in_proj(
            xf, mix_norm_w[l], w_in[l], ssm_inner=ssm_inner, conv_dim=conv_dim,
            ssm_heads=ssm_heads, attn_inner=attn_inner)
        y_ssm = _ssd(xbc, zg, dt_raw, conv_w[l], conv_b[l], dt_bias[l], A_log[l], D_skip[l],
                     batch=batch, seq=seq)
        y_attn = _attention(q, k, v, rel_bias[l], batch=batch, seq=seq)
        w_bs = ssm_norm_w[l].astype(F32)[:, None] * w_branch_ssm[l]
        xf = _merge(xf, y_ssm, y_attn, g, w_bs, w_branch_attn[l], w_out[l])
        xf = _ffn(xf, ffn2_norm_w[l], ffn2_w_gu[l], ffn2_w_down[l], final_norm_w, final_norm=last)
    return xf.reshape(batch, seq, d)
```
